```python
import math
import jax, jax.numpy as jnp
from jax import lax
import numpy as np

D_MODEL = 2048
BATCH = 4
SEQ = 4096
DEPTH = 4

CHUNK = 64
Q_BLOCK = 128
N_MEM = 256
D_MIX = D_MODEL
LRU_WIDTH = D_MIX // 2
LRU_BLOCKS = 8
LRU_BLOCK_W = LRU_WIDTH // LRU_BLOCKS
CONV_W = 4
RGLRU_C = 8.0
DIFF_HEADS = 4
DIFF_DH = D_MIX // 4 // (2 * DIFF_HEADS)
DIFF_VD = 2 * DIFF_DH
FOX_HEADS = 4
FOX_DH = D_MIX // 4 // FOX_HEADS
XATTN_HEADS = 4
XATTN_DH = D_MODEL // XATTN_HEADS
D_FF = ((8 * D_MODEL // 3 + 255) // 256) * 256
FFN_RESIDUAL_WEIGHT = 0.5
REL_BUCKETS = 32
REL_MAX_DIST = 128
NORM_EPS = 1e-6
NEG_INF = -1e30
N_NORMS = 8

IN_SIZES = (
    LRU_WIDTH, LRU_WIDTH,
    DIFF_HEADS * 2 * DIFF_DH, DIFF_HEADS * 2 * DIFF_DH, DIFF_HEADS * DIFF_VD,
    FOX_HEADS * FOX_DH, FOX_HEADS * FOX_DH, FOX_HEADS * FOX_DH,
    FOX_HEADS,
)
D_IN = sum(IN_SIZES)
IN_SPLITS = tuple(int(v) for v in np.cumsum(IN_SIZES)[:-1])
D_CAT = LRU_WIDTH + DIFF_HEADS * DIFF_VD + FOX_HEADS * FOX_DH

kernel_name = "hybrid_rglru_diffattn_fox_macaron"


def rms_norm(x, g):
    x32 = x.astype(jnp.float32)
    return x32 * lax.rsqrt(jnp.mean(x32 * x32, axis=-1, keepdims=True) + NORM_EPS) * g


def swiglu(h, w_in, w_out):
    gate, up = jnp.split(h @ w_in, 2, axis=-1)
    return (jax.nn.silu(gate) * up) @ w_out


def causal_conv(x, w, b):
    S = x.shape[1]
    xp = jnp.pad(x, ((0, 0), (CONV_W - 1, 0), (0, 0)))
    return b + sum(xp[:, k:k + S] * w[k] for k in range(CONV_W))


def _linear_recurrence_combine(c1, c2):
    a1, b1 = c1
    a2, b2 = c2
    return a1 * a2, a2 * b1 + b2


def rg_lru(x, gate_w, gate_b, lam):
    B, S, W = x.shape
    xb = x.reshape(B, S, LRU_BLOCKS, LRU_BLOCK_W)
    gates = jnp.einsum('bsnd,gnde->gbsne', xb, gate_w).reshape(2, B, S, W) + gate_b[:, None, None, :]
    r = jax.nn.sigmoid(gates[0])
    i = jax.nn.sigmoid(gates[1])
    log_a = -RGLRU_C * r * jax.nn.softplus(-lam)
    a = jnp.exp(log_a)
    b = jnp.sqrt(-jnp.expm1(2.0 * log_a)) * (i * x)
    _, hs = lax.associative_scan(_linear_recurrence_combine, (a, b), axis=1)
    return hs


def t5_bucket(rel):
    nb = REL_BUCKETS // 2
    max_exact = nb // 2
    n = jnp.abs(rel)
    large = max_exact + (jnp.log(jnp.maximum(n, max_exact).astype(jnp.float32) / max_exact)
                         / math.log(REL_MAX_DIST / max_exact) * (nb - max_exact)).astype(jnp.int32)
    large = jnp.minimum(large, nb - 1)
    return jnp.where(rel > 0, nb, 0) + jnp.where(n < max_exact, n, large)


def diff_attention(q, k, v, lam, rel_bias):
    B, S = q.shape[:2]
    scale = DIFF_DH ** -0.5
    kpos = jnp.arange(S)

    def block(i):
        start = i * Q_BLOCK
        qb = lax.dynamic_slice_in_dim(q, start, Q_BLOCK, axis=1)
        qpos = start + jnp.arange(Q_BLOCK)
        bias = rel_bias[t5_bucket(kpos[None, :] - qpos[:, None])].transpose(2, 0, 1)
        allowed = (kpos[None, :] // CHUNK) <= (qpos[:, None] // CHUNK)
        logits = jnp.einsum('bqhcd,bkhcd->bchqk', qb, k) * scale + bias[None, None]
        p = jax.nn.softmax(jnp.where(allowed, logits, NEG_INF), axis=-1)
        attn = p[:, 0] - lam * p[:, 1]
        return jnp.einsum('bhqk,bkhe->bqhe', attn, v)

    out = lax.map(block, jnp.arange(S // Q_BLOCK))
    return out.transpose(1, 0, 2, 3, 4).reshape(B, S, DIFF_HEADS, DIFF_VD)


def forgetting_attention(q, k, v, log_f):
    B, S = q.shape[:2]
    scale = FOX_DH ** -0.5
    cum = jnp.cumsum(log_f, axis=1).transpose(0, 2, 1)
    kpos = jnp.arange(S)

    def block(i):
        start = i * Q_BLOCK
        qb = lax.dynamic_slice_in_dim(q, start, Q_BLOCK, axis=1)
        cq = lax.dynamic_slice_in_dim(cum, start, Q_BLOCK, axis=2)
        qpos = start + jnp.arange(Q_BLOCK)
        allowed = kpos[None, :] <= qpos[:, None]
        logits = jnp.einsum('bqhd,bkhd->bhqk', qb, k) * scale + cq[..., :, None] - cum[..., None, :]
        p = jax.nn.softmax(jnp.where(allowed, logits, NEG_INF), axis=-1)
        return jnp.einsum('bhqk,bkhd->bqhd', p, v)

    out = lax.map(block, jnp.arange(S // Q_BLOCK))
    return out.transpose(1, 0, 2, 3, 4).reshape(B, S, FOX_HEADS, FOX_DH)


def token_mixer(h, w_in, conv_w, conv_b, gate_w, gate_b, lru_lambda, diff_lambda, subln_g,
                f_bias, rel_bias, w_out, lam_init):
    B, S, _ = h.shape
    a_x, a_y, dq, dk, dv, fq, fk, fv, fl = jnp.split(h @ w_in, IN_SPLITS, axis=-1)
    a_out = rg_lru(causal_conv(a_x, conv_w, conv_b), gate_w, gate_b, lru_lambda) * jax.nn.gelu(a_y)
    lam = (jnp.exp(jnp.sum(diff_lambda[0] * diff_lambda[1]))
           - jnp.exp(jnp.sum(diff_lambda[2] * diff_lambda[3])) + lam_init)
    d_out = diff_attention(dq.reshape(B, S, DIFF_HEADS, 2, DIFF_DH),
                           dk.reshape(B, S, DIFF_HEADS, 2, DIFF_DH),
                           dv.reshape(B, S, DIFF_HEADS, DIFF_VD), lam, rel_bias)
    d_out = rms_norm(d_out, subln_g) * (1.0 - lam_init)
    log_f = jax.nn.log_sigmoid(fl + f_bias)
    f_out = forgetting_attention(fq.reshape(B, S, FOX_HEADS, FOX_DH), fk.reshape(B, S, FOX_HEADS, FOX_DH),
                                 fv.reshape(B, S, FOX_HEADS, FOX_DH), log_f)
    cat = jnp.concatenate([a_out, d_out.reshape(B, S, -1), f_out.reshape(B, S, -1)], axis=-1)
    return cat @ w_out


def memory_cross_attention(h, mem, w_q, w_kv, w_o):
    B, S, _ = h.shape
    M = mem.shape[1]
    q = (h @ w_q).reshape(B, S, XATTN_HEADS, XATTN_DH)
    k, v = jnp.split(mem @ w_kv, 2, axis=-1)
    k = k.reshape(B, M, XATTN_HEADS, XATTN_DH)
    v = v.reshape(B, M, XATTN_HEADS, XATTN_DH)
    p = jax.nn.softmax(jnp.einsum('bqhd,bkhd->bhqk', q, k) * XATTN_DH ** -0.5, axis=-1)
    return jnp.einsum('bhqk,bkhd->bqhd', p, v).reshape(B, S, D_MODEL) @ w_o


def setup_inputs(seed: int = 0) -> dict:
    key = jax.random.key(seed)
    ks = jax.random.split(key, 22)
    nrm = jax.random.normal
    f32 = jnp.float32
    u = jax.random.uniform(ks[10], (DEPTH, LRU_WIDTH), f32, 0.9, 0.999)
    p = u ** (1.0 / RGLRU_C)
    return {
        "x": nrm(ks[0], (BATCH, SEQ, D_MODEL), f32),
        "mem": nrm(ks[1], (BATCH, N_MEM, D_MODEL), f32),
        "norm_g": 1.0 + 0.05 * nrm(ks[2], (DEPTH, N_NORMS, D_MODEL), f32),
        "ffn1_w_in": nrm(ks[3], (DEPTH, D_MODEL, 2 * D_FF), f32) * D_MODEL ** -0.5,
        "ffn1_w_out": nrm(ks[4], (DEPTH, D_FF, D_MODEL), f32) * D_FF ** -0.5,
        "w_in": nrm(ks[5], (DEPTH, D_MODEL, D_IN), f32) * D_MODEL ** -0.5,
        "conv_w": nrm(ks[6], (DEPTH, CONV_W, LRU_WIDTH), f32) * CONV_W ** -0.5,
        "conv_b": 0.01 * nrm(ks[7], (DEPTH, LRU_WIDTH), f32),
        "lru_gate_w": nrm(ks[8], (DEPTH, 2, LRU_BLOCKS, LRU_BLOCK_W, LRU_BLOCK_W), f32) * LRU_BLOCK_W ** -0.5,
        "lru_gate_b": 0.01 * nrm(ks[9], (DEPTH, 2, LRU_WIDTH), f32),
        "lru_lambda": jnp.log(p) - jnp.log1p(-p),
        "diff_lambda": 0.1 * nrm(ks[11], (DEPTH, 4, DIFF_DH), f32),
        "diff_subln_g": 1.0 + 0.05 * nrm(ks[12], (DEPTH, DIFF_VD), f32),
        "fox_f_bias": jax.random.uniform(ks[13], (DEPTH, FOX_HEADS), f32, 1.0, 5.0),
        "rel_bias": 0.5 * nrm(ks[14], (REL_BUCKETS, DIFF_HEADS), f32),
        "w_out": nrm(ks[15], (DEPTH, D_CAT, D_MODEL), f32) * D_CAT ** -0.5,
        "xattn_w_q": nrm(ks[16], (DEPTH, D_MODEL, D_MODEL), f32) * D_MODEL ** -0.5,
        "xattn_w_kv": nrm(ks[17], (DEPTH, D_MODEL, 2 * D_MODEL), f32) * D_MODEL ** -0.5,
        "xattn_w_o": nrm(ks[18], (DEPTH, D_MODEL, D_MODEL), f32) * D_MODEL ** -0.5,
        "ffn2_w_in": nrm(ks[19], (DEPTH, D_MODEL, 2 * D_FF), f32) * D_MODEL ** -0.5,
        "ffn2_w_out": nrm(ks[20], (DEPTH, D_FF, D_MODEL), f32) * D_FF ** -0.5,
    }


def reference(x, mem, norm_g, ffn1_w_in, ffn1_w_out, w_in, conv_w, conv_b, lru_gate_w, lru_gate_b,
              lru_lambda, diff_lambda, diff_subln_g, fox_f_bias, rel_bias, w_out, xattn_w_q,
              xattn_w_kv, xattn_w_o, ffn2_w_in, ffn2_w_out):
    out_dtype = x.dtype
    h = x.astype(jnp.float32)
    mem32 = mem.astype(jnp.float32)
    for l in range(DEPTH):
        g = norm_g[l]
        lam_init = 0.8 - 0.6 * math.exp(-0.3 * l)
        h = h + FFN_RESIDUAL_WEIGHT * rms_norm(swiglu(rms_norm(h, g[0]), ffn1_w_in[l], ffn1_w_out[l]), g[1])
        mix = token_mixer(rms_norm(h, g[2]), w_in[l], conv_w[l], conv_b[l], lru_gate_w[l], lru_gate_b[l],
                          lru_lambda[l], diff_lambda[l], diff_subln_g[l], fox_f_bias[l], rel_bias,
                          w_out[l], lam_init)
        h = h + rms_norm(mix, g[3])
        h = h + rms_norm(memory_cross_attention(rms_norm(h, g[4]), mem32, xattn_w_q[l], xattn_w_kv[l],
                                                xattn_w_o[l]), g[5])
        h = h + FFN_RESIDUAL_WEIGHT * rms_norm(swiglu(rms_norm(h, g[6]), ffn2_w_in[l], ffn2_w_out[l]), g[7])
    return h.astype(out_dtype)
```

```python
import functools
import math

import jax
import jax.numpy as jnp
import numpy as np
from jax import lax
from jax.experimental import pallas as pl
from jax.experimental.pallas import tpu as pltpu

F32 = jnp.float32
BF16 = jnp.bfloat16

CHUNK = 64
CONV_W = 4
LRU_BLOCK_W = 128
RGLRU_C = 8.0
DIFF_HEADS = 4
DIFF_DH = 64
DIFF_VD = 128
FOX_HEADS = 4
FOX_DH = 128
XATTN_HEADS = 4
REL_BUCKETS = 32
REL_MAX_DIST = 128
NORM_EPS = 1e-6
NEG_INF = -1e30
FFN_RESIDUAL_WEIGHT = 0.5

LANES = 128
SUBLANES = 8
V7X_VMEM_BYTES = 64 * 1024 * 1024
VMEM_LIMIT_BYTES = (V7X_VMEM_BYTES * 3) // 4


def _params(*semantics):
    return pltpu.CompilerParams(dimension_semantics=semantics, vmem_limit_bytes=VMEM_LIMIT_BYTES)


def _rms(x, g):
    return x * lax.rsqrt(jnp.mean(x * x, axis=-1, keepdims=True) + NORM_EPS) * g


def _dot(a, b):
    return jnp.dot(a, b, preferred_element_type=F32)


def _dot_nt(a, b):
    return lax.dot_general(a, b, (((1,), (1,)), ((), ())), preferred_element_type=F32)


def _ffn_kernel(x_ref, gpre_ref, gpost_ref, wg_ref, wu_ref, wo_ref, o_ref, xn_ref, acc_ref):
    j = pl.program_id(1)

    @pl.when(j == 0)
    def _():
        xn_ref[...] = _rms(x_ref[...], gpre_ref[...]).astype(BF16)
        acc_ref[...] = jnp.zeros_like(acc_ref)

    xn = xn_ref[...]
    gate = _dot(xn, wg_ref[...])
    up = _dot(xn, wu_ref[...])
    act = (gate * jax.nn.sigmoid(gate) * up).astype(BF16)
    acc_ref[...] += _dot(act, wo_ref[...])

    @pl.when(j == pl.num_programs(1) - 1)
    def _():
        o_ref[...] = x_ref[...] + FFN_RESIDUAL_WEIGHT * _rms(acc_ref[...], gpost_ref[...])


def _ffn(h, g_pre, g_post, w_in, w_out, layer, *, tm=512, tf=512):
    T, D = h.shape
    F = w_out.shape[1]
    nf = F // tf
    return pl.pallas_call(
        _ffn_kernel,
        grid=(T // tm, nf),
        in_specs=[
            pl.BlockSpec((tm, D), lambda i, j: (i, 0)),
            pl.BlockSpec((1, D), lambda i, j: (0, 0)),
            pl.BlockSpec((1, D), lambda i, j: (0, 0)),
            pl.BlockSpec((None, D, tf), lambda i, j: (layer, 0, j)),
            pl.BlockSpec((None, D, tf), lambda i, j: (layer, 0, j + nf)),
            pl.BlockSpec((None, tf, D), lambda i, j: (layer, j, 0)),
        ],
        out_specs=pl.BlockSpec((tm, D), lambda i, j: (i, 0)),
        out_shape=jax.ShapeDtypeStruct((T, D), F32),
        scratch_shapes=[pltpu.VMEM((tm, D), BF16), pltpu.VMEM((tm, D), F32)],
        compiler_params=_params("parallel", "arbitrary"),
        name="ffn",
    )(h, g_pre, g_post, w_in, w_in, w_out)


def _norm_matmul_kernel(x_ref, g_ref, w_ref, o_ref, xn_ref, *, normalize):
    @pl.when(pl.program_id(1) == 0)
    def _():
        x = x_ref[...]
        if normalize:
            x = _rms(x, g_ref[...])
        xn_ref[...] = x.astype(BF16)

    o_ref[...] = _dot(xn_ref[...], w_ref[...]).astype(o_ref.dtype)


def _norm_matmul(x, g, w, layer, col0, n_cols, out_dtype, *, normalize=True, tm=512, tn=512):
    T, K = x.shape
    tn = min(tn, n_cols)
    c0 = col0 // tn
    return pl.pallas_call(
        functools.partial(_norm_matmul_kernel, normalize=normalize),
        grid=(T // tm, n_cols // tn),
        in_specs=[
            pl.BlockSpec((tm, K), lambda i, j: (i, 0)),
            pl.BlockSpec((1, K), lambda i, j: (0, 0)),
            pl.BlockSpec((None, K, tn), lambda i, j: (layer, 0, c0 + j)),
        ],
        out_specs=pl.BlockSpec((tm, tn), lambda i, j: (i, j)),
        out_shape=jax.ShapeDtypeStruct((T, n_cols), out_dtype),
        scratch_shapes=[pltpu.VMEM((tm, K), BF16)],
        compiler_params=_params("parallel", "arbitrary"),
        name="norm_matmul",
    )(x, g, w)


def _softplus(x):
    return jnp.maximum(x, 0.0) + jnp.log1p(jnp.exp(-jnp.abs(x)))


def _rglru_kernel(ax_ref, ay_ref, cw_ref, cb_ref, gw_ref, gb_ref, lam_ref, o_ref,
                  xc_ref, tail_ref, hc_ref, *, ts):
    @pl.when(pl.program_id(1) == 0)
    def _():
        tail_ref[...] = jnp.zeros_like(tail_ref)
        hc_ref[...] = jnp.zeros_like(hc_ref)

    x = ax_ref[...]
    cw = cw_ref[...]
    cb = cb_ref[...]
    xc = cb + cw[CONV_W - 1:CONV_W] * x
    for d in range(1, CONV_W):
        xc = xc + cw[CONV_W - 1 - d:CONV_W - d] * pltpu.roll(x, d, axis=0)
    xc_ref[...] = xc
    x8 = x[0:SUBLANES]
    t8 = tail_ref[...]
    row8 = lax.broadcasted_iota(jnp.int32, x8.shape, 0)
    xc8 = cb + cw[CONV_W - 1:CONV_W] * x8
    for d in range(1, CONV_W):
        sh = jnp.where(row8 < d, pltpu.roll(t8, d, axis=0), pltpu.roll(x8, d, axis=0))
        xc8 = xc8 + cw[CONV_W - 1 - d:CONV_W - d] * sh
    xc_ref[0:SUBLANES, :] = xc8
    tail_ref[...] = x[ts - SUBLANES:ts]

    sp = _softplus(-lam_ref[...])
    row = lax.broadcasted_iota(jnp.int32, (ts, LRU_BLOCK_W), 0)
    for n in range(x.shape[1] // LRU_BLOCK_W):
        sl = slice(n * LRU_BLOCK_W, (n + 1) * LRU_BLOCK_W)
        xcn = xc_ref[:, sl]
        xb = xcn.astype(BF16)
        r = jax.nn.sigmoid(_dot(xb, gw_ref[0, n]) + gb_ref[0:1, sl])
        i = jax.nn.sigmoid(_dot(xb, gw_ref[1, n]) + gb_ref[1:2, sl])
        log_a = -RGLRU_C * r * sp[:, sl]
        a = jnp.exp(log_a)
        b = jnp.sqrt(-jnp.tanh(log_a) * (a * a + 1.0)) * (i * xcn)
        d = 1
        while d < ts:
            a_sh = jnp.where(row < d, 1.0, pltpu.roll(a, d, axis=0))
            b_sh = jnp.where(row < d, 0.0, pltpu.roll(b, d, axis=0))
            b = a * b_sh + b
            a = a * a_sh
            d *= 2
        hs = b + a * hc_ref[:, sl]
        hc_ref[:, sl] = hs[ts - 1:ts]
        o_ref[:, sl] = (hs * jax.nn.gelu(ay_ref[:, sl])).astype(o_ref.dtype)


def _rglru(axy, conv_w, conv_b, gate_w, gate_b, lam, batch, *, ts=256):
    T, W2 = axy.shape
    W = W2 // 2
    ns = T // batch // ts
    return pl.pallas_call(
        functools.partial(_rglru_kernel, ts=ts),
        grid=(batch, ns),
        in_specs=[
            pl.BlockSpec((ts, W), lambda b, s: (b * ns + s, 0)),
            pl.BlockSpec((ts, W), lambda b, s: (b * ns + s, 1)),
            pl.BlockSpec((CONV_W, W), lambda b, s: (0, 0)),
            pl.BlockSpec((1, W), lambda b, s: (0, 0)),
            pl.BlockSpec(gate_w.shape, lambda b, s: (0, 0, 0, 0)),
            pl.BlockSpec((2, W), lambda b, s: (0, 0)),
            pl.BlockSpec((1, W), lambda b, s: (0, 0)),
        ],
        out_specs=pl.BlockSpec((ts, W), lambda b, s: (b * ns + s, 0)),
        out_shape=jax.ShapeDtypeStruct((T, W), BF16),
        scratch_shapes=[pltpu.VMEM((ts, W), F32), pltpu.VMEM((SUBLANES, W), F32), pltpu.VMEM((1, W), F32)],
        compiler_params=_params("parallel", "arbitrary"),
        name="rglru",
    )(axy, axy, conv_w, conv_b, gate_w, gate_b, lam)


def _fox_prep_kernel(fl_ref, fb_ref, col_ref, row_ref):
    x = fl_ref[...] + fb_ref[...]
    c = jnp.minimum(x, 0.0) - jnp.log1p(jnp.exp(-jnp.abs(x)))
    S = c.shape[0]
    row = lax.broadcasted_iota(jnp.int32, c.shape, 0)
    d = 1
    while d < S:
        c = c + jnp.where(row < d, 0.0, pltpu.roll(c, d, axis=0))
        d *= 2
    col_ref[...] = c
    for k in range(S // LANES):
        row_ref[:, k * LANES:(k + 1) * LANES] = c[k * LANES:(k + 1) * LANES, :].T


def _fox_prep(fl, f_bias, batch):
    T = fl.shape[0]
    S = T // batch
    return pl.pallas_call(
        _fox_prep_kernel,
        grid=(batch,),
        in_specs=[pl.BlockSpec((S, LANES), lambda b: (b, 0)), pl.BlockSpec((1, LANES), lambda b: (0, 0))],
        out_specs=[pl.BlockSpec((S, LANES), lambda b: (b, 0)), pl.BlockSpec((None, LANES, S), lambda b: (b, 0, 0))],
        out_shape=[jax.ShapeDtypeStruct((T, LANES), F32), jax.ShapeDtypeStruct((batch, LANES, S), F32)],
        compiler_params=_params("parallel"),
        name="fox_prep",
    )(fl, f_bias)


def _t5_bucket(rel):
    nb = REL_BUCKETS // 2
    max_exact = nb // 2
    n = jnp.abs(rel)
    large = max_exact + (jnp.log(jnp.maximum(n, max_exact).astype(jnp.float32) / max_exact)
                         / math.log(REL_MAX_DIST / max_exact) * (nb - max_exact)).astype(jnp.int32)
    large = jnp.minimum(large, nb - 1)
    return jnp.where(rel > 0, nb, 0) + jnp.where(n < max_exact, n, large)


def _bias_tile_types(tq, tk):
    return -(-(tk - 1 + REL_MAX_DIST) // tq) + 1


def _bucket_tiles(tq, tk):
    nt = _bias_tile_types(tq, tk)
    t = jnp.arange(nt, dtype=jnp.int32)[:, None, None]
    r = jnp.arange(tq, dtype=jnp.int32)[None, :, None] + t * tq
    c = jnp.arange(tk, dtype=jnp.int32)[None, None, :]
    allowed = (c // CHUNK) <= (r // CHUNK)
    return jnp.where(allowed, _t5_bucket(c - r), -1)


def _bias_table_kernel(rb_ref, bucket_ref, o_ref):
    h = pl.program_id(0)
    bucket = bucket_ref[...]
    out = jnp.full(bucket.shape, NEG_INF, F32)
    for b in range(REL_BUCKETS):
        out = jnp.where(bucket == b, rb_ref[b, h], out)
    o_ref[...] = out


def _bias_table(rel_bias, tq, tk):
    buckets = _bucket_tiles(tq, tk)
    nt = buckets.shape[0]
    H = rel_bias.shape[1]
    return pl.pallas_call(
        _bias_table_kernel,
        grid=(H, nt),
        in_specs=[
            pl.BlockSpec(memory_space=pltpu.SMEM),
            pl.BlockSpec((None, tq, tk), lambda h, t: (t, 0, 0)),
        ],
        out_specs=pl.BlockSpec((None, None, tq, tk), lambda h, t: (h, t, 0, 0)),
        out_shape=jax.ShapeDtypeStruct((H, nt, tq, tk), F32),
        compiler_params=_params("parallel", "parallel"),
        name="bias_table",
    )(rel_bias, buckets)


def _softmax_step(s, vb, carry):
    m, l, acc = carry
    m_new = jnp.maximum(m, jnp.max(s, axis=-1, keepdims=True))
    alpha = jnp.exp(m - m_new)
    p = jnp.exp(s - m_new)
    l = alpha * l + jnp.sum(p, axis=-1, keepdims=True)
    acc = alpha * acc + _dot(p.astype(BF16), vb)
    return m_new, l, acc


def _softmax_init(rows, width):
    return (jnp.full((rows, 1), NEG_INF, F32), jnp.zeros((rows, 1), F32), jnp.zeros((rows, width), F32))


def _diff_attn_kernel(q_ref, k_ref, v_ref, bias_ref, dl_ref, sg_ref, o_ref, *, tq, tk, nt, lam_init):
    q0 = pl.program_id(2) * tq
    q = q_ref[...] * (DIFF_DH ** -0.5)
    lane = lax.broadcasted_iota(jnp.int32, q.shape, 1)
    qq = jnp.concatenate([jnp.where(lane < DIFF_DH, q, 0), jnp.where(lane >= DIFF_DH, q, 0)], axis=0)

    def body(j, carry):
        k0 = pl.multiple_of(j * tk, tk)
        kb = k_ref[pl.ds(k0, tk), :]
        vb = v_ref[pl.ds(k0, tk), :]
        s = _dot_nt(qq, kb)
        bias = bias_ref[jnp.minimum((q0 - k0) // tq, nt - 1)]
        s = (s.reshape(2, tq, tk) + bias[None]).reshape(2 * tq, tk)
        return _softmax_step(s, vb, carry)

    _, l, acc = lax.fori_loop(0, q0 // tk + 1, body, _softmax_init(2 * tq, DIFF_VD))
    o = acc / l
    dl = dl_ref[...]
    lam = (jnp.exp(jnp.sum(dl[0:1] * dl[1:2], axis=-1, keepdims=True))
           - jnp.exp(jnp.sum(dl[2:3] * dl[3:4], axis=-1, keepdims=True)) + lam_init)
    d = o[:tq] - lam * o[tq:]
    o_ref[...] = (_rms(d, sg_ref[...]) * (1.0 - lam_init)).astype(o_ref.dtype)


def _diff_attn(qkv, bias_table, diff_lambda, subln_g, batch, lam_init, *, tq, tk):
    T = qkv.shape[0]
    S = T // batch
    nq = S // tq
    nt = bias_table.shape[1]
    H = DIFF_HEADS
    return pl.pallas_call(
        functools.partial(_diff_attn_kernel, tq=tq, tk=tk, nt=nt, lam_init=lam_init),
        grid=(batch, H, nq),
        in_specs=[
            pl.BlockSpec((tq, LANES), lambda b, h, i: (b * nq + i, h)),
            pl.BlockSpec((S, LANES), lambda b, h, i: (b, H + h)),
            pl.BlockSpec((S, LANES), lambda b, h, i: (b, 2 * H + h)),
            pl.BlockSpec((None, nt, tq, tk), lambda b, h, i: (h, 0, 0, 0)),
            pl.BlockSpec(diff_lambda.shape, lambda b, h, i: (0, 0)),
            pl.BlockSpec((1, DIFF_VD), lambda b, h, i: (0, 0)),
        ],
        out_specs=pl.BlockSpec((tq, DIFF_VD), lambda b, h, i: (b * nq + i, h)),
        out_shape=jax.ShapeDtypeStruct((T, H * DIFF_VD), BF16),
        compiler_params=_params("parallel", "parallel", "arbitrary"),
        name="diff_attn",
    )(qkv, qkv, qkv, bias_table, diff_lambda, subln_g)


def _fox_attn_kernel(q_ref, k_ref, v_ref, ccol_ref, crow_ref, o_ref, *, tq, tk):
    h = pl.program_id(1)
    q0 = pl.program_id(2) * tq
    q = q_ref[...]
    lane = lax.broadcasted_iota(jnp.int32, ccol_ref.shape, 1)
    cq = jnp.sum(jnp.where(lane == h, ccol_ref[...], 0.0), axis=-1, keepdims=True)

    def step(j, carry, masked):
        k0 = pl.multiple_of(j * tk, tk)
        kb = k_ref[pl.ds(k0, tk), :]
        vb = v_ref[pl.ds(k0, tk), :]
        s = _dot_nt(q, kb) * (FOX_DH ** -0.5) + (cq - crow_ref[:, pl.ds(k0, tk)])
        if masked:
            r = lax.broadcasted_iota(jnp.int32, s.shape, 0) + q0
            c = lax.broadcasted_iota(jnp.int32, s.shape, 1) + k0
            s = jnp.where(c <= r, s, NEG_INF)
        return _softmax_step(s, vb, carry)

    n_full = q0 // tk
    carry = lax.fori_loop(0, n_full, functools.partial(step, masked=False), _softmax_init(tq, FOX_DH))
    _, l, acc = step(n_full, carry, True)
    o_ref[...] = (acc / l).astype(o_ref.dtype)


def _fox_attn(qkv, cum_col, cum_row, batch, *, tq, tk):
    T = qkv.shape[0]
    S = T // batch
    nq = S // tq
    H = FOX_HEADS
    base = 3 * DIFF_HEADS
    return pl.pallas_call(
        functools.partial(_fox_attn_kernel, tq=tq, tk=tk),
        grid=(batch, H, nq),
        in_specs=[
            pl.BlockSpec((tq, LANES), lambda b, h, i: (b * nq + i, base + h)),
            pl.BlockSpec((S, LANES), lambda b, h, i: (b, base + H + h)),
            pl.BlockSpec((S, LANES), lambda b, h, i: (b, base + 2 * H + h)),
            pl.BlockSpec((tq, LANES), lambda b, h, i: (b * nq + i, 0)),
            pl.BlockSpec((None, None, 1, S), lambda b, h, i: (b, h, 0, 0)),
        ],
        out_specs=pl.BlockSpec((tq, FOX_DH), lambda b, h, i: (b * nq + i, h)),
        out_shape=jax.ShapeDtypeStruct((T, H * FOX_DH), BF16),
        compiler_params=_params("parallel", "parallel", "arbitrary"),
        name="fox_attn",
    )(qkv, qkv, qkv, cum_col, cum_row)


def _xattn_kernel(q_ref, k_ref, v_ref, o_ref, *, scale):
    s = _dot_nt(q_ref[...], k_ref[...]) * scale
    p = jnp.exp(s - jnp.max(s, axis=-1, keepdims=True))
    l = jnp.sum(p, axis=-1, keepdims=True)
    o_ref[...] = (_dot(p.astype(BF16), v_ref[...]) / l).astype(o_ref.dtype)


def _xattn(q, kv, batch, *, tq=512):
    T, D = q.shape
    M = kv.shape[0] // batch
    H = XATTN_HEADS
    dh = D // H
    nq = T // batch // tq
    return pl.pallas_call(
        functools.partial(_xattn_kernel, scale=dh ** -0.5),
        grid=(batch, nq, H),
        in_specs=[
            pl.BlockSpec((tq, dh), lambda b, i, h: (b * nq + i, h)),
            pl.BlockSpec((M, dh), lambda b, i, h: (b, h)),
            pl.BlockSpec((M, dh), lambda b, i, h: (b, H + h)),
        ],
        out_specs=pl.BlockSpec((tq, dh), lambda b, i, h: (b * nq + i, h)),
        out_shape=jax.ShapeDtypeStruct((T, D), BF16),
        compiler_params=_params("parallel", "parallel", "parallel"),
        name="xattn",
    )(q, kv, kv)


def _proj_norm_res_kernel(*refs, n_in):
    x_refs, w_refs = refs[:n_in], refs[n_in:2 * n_in]
    g_ref, h_ref, o_ref = refs[2 * n_in:]
    y = _dot(x_refs[0][...], w_refs[0][...])
    for x_ref, w_ref in zip(x_refs[1:], w_refs[1:]):
        y = y + _dot(x_ref[...], w_ref[...])
    o_ref[...] = h_ref[...] + _rms(y, g_ref[...])


def _proj_norm_res(xs, w, layer, g, h, *, tm=512):
    T, N = h.shape
    n_in = len(xs)
    x_specs, w_specs, row0 = [], [], 0
    for x in xs:
        k = x.shape[1]
        x_specs.append(pl.BlockSpec((tm, k), lambda i: (i, 0)))
        w_specs.append(pl.BlockSpec((None, k, N), functools.partial(lambda i, r: (layer, r, 0), r=row0 // k)))
        row0 += k
    return pl.pallas_call(
        functools.partial(_proj_norm_res_kernel, n_in=n_in),
        grid=(T // tm,),
        in_specs=x_specs + w_specs + [pl.BlockSpec((1, N), lambda i: (0, 0)), pl.BlockSpec((tm, N), lambda i: (i, 0))],
        out_specs=pl.BlockSpec((tm, N), lambda i: (i, 0)),
        out_shape=jax.ShapeDtypeStruct((T, N), F32),
        compiler_params=_params("parallel"),
        name="proj_norm_res",
    )(*xs, *([w] * n_in), g, h)


def kernel(x, mem, norm_g, ffn1_w_in, ffn1_w_out, w_in, conv_w, conv_b, lru_gate_w, lru_gate_b, lru_lambda,
           diff_lambda, diff_subln_g, fox_f_bias, rel_bias, w_out, xattn_w_q, xattn_w_kv, xattn_w_o,
           ffn2_w_in, ffn2_w_out):
    B, S, D = x.shape
    depth = norm_g.shape[0]
    T = B * S
    lru_w = conv_w.shape[-1]
    n_axy = 2 * lru_w
    n_qkv = 3 * DIFF_HEADS * DIFF_VD + 3 * FOX_HEADS * FOX_DH
    attn_tq, attn_tk = 256, 512

    h = x.astype(F32).reshape(T, D)
    mem2 = mem.astype(F32).reshape(B * mem.shape[1], D)
    ones_g = jnp.ones((1, D), F32)

    bf = lambda a: a.astype(BF16)
    ffn1_w_in, ffn1_w_out, ffn2_w_in, ffn2_w_out = bf(ffn1_w_in), bf(ffn1_w_out), bf(ffn2_w_in), bf(ffn2_w_out)
    w_in_b, w_out_b, gate_w_b = bf(w_in), bf(w_out), bf(lru_gate_w)
    w_q_b, w_kv_b, w_o_b = bf(xattn_w_q), bf(xattn_w_kv), bf(xattn_w_o)
    w_fl = bf(jnp.pad(w_in[:, :, n_axy + n_qkv:], ((0, 0), (0, 0), (0, LANES - FOX_HEADS))))
    f_bias = jnp.pad(fox_f_bias, ((0, 0), (0, LANES - FOX_HEADS)))

    bias_table = _bias_table(rel_bias, attn_tq, attn_tk)

    for l in range(depth):
        g = norm_g[l]
        lam_init = 0.8 - 0.6 * math.exp(-0.3 * l)
        h = _ffn(h, g[0:1], g[1:2], ffn1_w_in, ffn1_w_out, l)

        axy = _norm_matmul(h, g[2:3], w_in_b, l, 0, n_axy, F32)
        qkv = _norm_matmul(h, g[2:3], w_in_b, l, n_axy, n_qkv, BF16)
        fl = _norm_matmul(h, g[2:3], w_fl, l, 0, LANES, F32)
        a_out = _rglru(axy, conv_w[l], conv_b[l][None], gate_w_b[l], lru_gate_b[l], lru_lambda[l][None], B)
        cum_col, cum_row = _fox_prep(fl, f_bias[l][None], B)
        cum_row = cum_row[:, :FOX_HEADS, None, :]
        d_out = _diff_attn(qkv, bias_table, diff_lambda[l], diff_subln_g[l][None], B, lam_init,
                           tq=attn_tq, tk=attn_tk)
        f_out = _fox_attn(qkv, cum_col, cum_row, B, tq=attn_tq, tk=attn_tk)
        h = _proj_norm_res([a_out, d_out, f_out], w_out_b, l, g[3:4], h)

        q = _norm_matmul(h, g[4:5], w_q_b, l, 0, D, BF16)
        kv = _norm_matmul(mem2, ones_g, w_kv_b, l, 0, 2 * D, BF16, normalize=False)
        xo = _xattn(q, kv, B)
        h = _proj_norm_res([xo], w_o_b, l, g[5:6], h)

        h = _ffn(h, g[6:7], g[7:8], ffn2_w_in, ffn2_w_out, l)
    return h.reshape(B, S, D).astype(x.dtype)
```

```python
import functools
import math

import jax
import jax.numpy as jnp
import numpy as np
from jax import lax
from jax.experimental import pallas as pl
from jax.experimental.pallas import tpu as pltpu

F32 = jnp.float32
BF16 = jnp.bfloat16

CHUNK = 64
CONV_W = 4
LRU_BLOCK_W = 128
RGLRU_C = 8.0
DIFF_HEADS = 4
DIFF_DH = 64
DIFF_VD = 128
FOX_HEADS = 4
FOX_DH = 128
XATTN_HEADS = 4
REL_BUCKETS = 32
REL_MAX_DIST = 128
NORM_EPS = 1e-6
NEG_INF = -1e30
FFN_RESIDUAL_WEIGHT = 0.5
LOG2E = math.log2(math.e)

LANES = 128
SUBLANES = 8
V7X_VMEM_BYTES = 64 * 1024 * 1024
VMEM_LIMIT_BYTES = (V7X_VMEM_BYTES * 3) // 4


def _params(*semantics):
    return pltpu.CompilerParams(dimension_semantics=semantics, vmem_limit_bytes=VMEM_LIMIT_BYTES)


def _rms(x, g):
    return x * lax.rsqrt(jnp.mean(x * x, axis=-1, keepdims=True) + NORM_EPS) * g


def _dot(a, b):
    return jnp.dot(a, b, preferred_element_type=F32)


def _dot_nt(a, b):
    return lax.dot_general(a, b, (((1,), (1,)), ((), ())), preferred_element_type=F32)


def _ffn_kernel(x_ref, gpre_ref, gpost_ref, wg_ref, wu_ref, wo_ref, o_ref, xn_ref, acc_ref):
    j = pl.program_id(1)

    @pl.when(j == 0)
    def _():
        xn_ref[...] = _rms(x_ref[...], gpre_ref[...]).astype(BF16)
        acc_ref[...] = jnp.zeros_like(acc_ref)

    xn = xn_ref[...]
    gate = _dot(xn, wg_ref[...])
    up = _dot(xn, wu_ref[...])
    act = (gate * jax.nn.sigmoid(gate) * up).astype(BF16)
    acc_ref[...] += _dot(act, wo_ref[...])

    @pl.when(j == pl.num_programs(1) - 1)
    def _():
        o_ref[...] = x_ref[...] + FFN_RESIDUAL_WEIGHT * _rms(acc_ref[...], gpost_ref[...])


def _ffn(h, g_pre, g_post, w_in, w_out, layer, *, tm=512, tf=512):
    T, D = h.shape
    F = w_out.shape[1]
    nf = F // tf
    return pl.pallas_call(
        _ffn_kernel,
        grid=(T // tm, nf),
        in_specs=[
            pl.BlockSpec((tm, D), lambda i, j: (i, 0)),
            pl.BlockSpec((1, D), lambda i, j: (0, 0)),
            pl.BlockSpec((1, D), lambda i, j: (0, 0)),
            pl.BlockSpec((None, D, tf), lambda i, j: (layer, 0, j)),
            pl.BlockSpec((None, D, tf), lambda i, j: (layer, 0, j + nf)),
            pl.BlockSpec((None, tf, D), lambda i, j: (layer, j, 0)),
        ],
        out_specs=pl.BlockSpec((tm, D), lambda i, j: (i, 0)),
        out_shape=jax.ShapeDtypeStruct((T, D), F32),
        scratch_shapes=[pltpu.VMEM((tm, D), BF16), pltpu.VMEM((tm, D), F32)],
        compiler_params=_params("parallel", "arbitrary"),
        name="ffn",
    )(h, g_pre, g_post, w_in, w_in, w_out)


def _norm_matmul_kernel(x_ref, g_ref, w_ref, o_ref, xn_ref, *, normalize):
    @pl.when(pl.program_id(1) == 0)
    def _():
        x = x_ref[...]
        if normalize:
            x = _rms(x, g_ref[...])
        xn_ref[...] = x.astype(BF16)

    o_ref[...] = _dot(xn_ref[...], w_ref[...]).astype(o_ref.dtype)


def _norm_matmul(x, g, w, layer, n_cols, out_dtype, *, normalize=True, tm=512, tn=1024):
    T, K = x.shape
    tn = min(tn, n_cols)
    return pl.pallas_call(
        functools.partial(_norm_matmul_kernel, normalize=normalize),
        grid=(T // tm, n_cols // tn),
        in_specs=[
            pl.BlockSpec((tm, K), lambda i, j: (i, 0)),
            pl.BlockSpec((1, K), lambda i, j: (0, 0)),
            pl.BlockSpec((None, K, tn), lambda i, j: (layer, 0, j)),
        ],
        out_specs=pl.BlockSpec((tm, tn), lambda i, j: (i, j)),
        out_shape=jax.ShapeDtypeStruct((T, n_cols), out_dtype),
        scratch_shapes=[pltpu.VMEM((tm, K), BF16)],
        compiler_params=_params("parallel", "arbitrary"),
        name="norm_matmul",
    )(x, g, w)


def _in_proj_kernel(x_ref, g_ref, w_ref, wfl_ref, cs_ref, axy_ref, qkv_ref, fl_ref, xn_ref, *, n_axy_tiles):
    j = pl.program_id(1)

    @pl.when(j == 0)
    def _():
        xn = _rms(x_ref[...], g_ref[...]).astype(BF16)
        xn_ref[...] = xn
        fl_ref[...] = _dot(xn, wfl_ref[...])

    @pl.when(j < n_axy_tiles)
    def _():
        axy_ref[...] = _dot(xn_ref[...], w_ref[...])

    @pl.when(j >= n_axy_tiles)
    def _():
        qkv_ref[...] = (_dot(xn_ref[...], w_ref[...]) * cs_ref[...]).astype(BF16)


def _in_proj(x, g, w, w_fl, col_scale, layer, n_axy, n_qkv, *, tm=512, tn=1024):
    T, K = x.shape
    na, nq = n_axy // tn, n_qkv // tn
    return pl.pallas_call(
        functools.partial(_in_proj_kernel, n_axy_tiles=na),
        grid=(T // tm, na + nq),
        in_specs=[
            pl.BlockSpec((tm, K), lambda i, j: (i, 0)),
            pl.BlockSpec((1, K), lambda i, j: (0, 0)),
            pl.BlockSpec((None, K, tn), lambda i, j: (layer, 0, j)),
            pl.BlockSpec((None, K, LANES), lambda i, j: (layer, 0, 0)),
            pl.BlockSpec((1, tn), lambda i, j: (0, jnp.maximum(j - na, 0))),
        ],
        out_specs=[
            pl.BlockSpec((tm, tn), lambda i, j: (i, jnp.minimum(j, na - 1))),
            pl.BlockSpec((tm, tn), lambda i, j: (i, jnp.maximum(j - na, 0))),
            pl.BlockSpec((tm, LANES), lambda i, j: (i, 0)),
        ],
        out_shape=[
            jax.ShapeDtypeStruct((T, n_axy), F32),
            jax.ShapeDtypeStruct((T, n_qkv), BF16),
            jax.ShapeDtypeStruct((T, LANES), F32),
        ],
        scratch_shapes=[pltpu.VMEM((tm, K), BF16)],
        compiler_params=_params("parallel", "arbitrary"),
        name="in_proj",
    )(x, g, w, w_fl, col_scale)


def _softplus(x):
    return jnp.maximum(x, 0.0) + jnp.log1p(jnp.exp(-jnp.abs(x)))


def _rglru_kernel(ax_ref, ay_ref, cw_ref, cb_ref, gw_ref, gb_ref, lam_ref, o_ref,
                  xc_ref, tail_ref, hc_ref, *, ts):
    @pl.when(pl.program_id(1) == 0)
    def _():
        tail_ref[...] = jnp.zeros_like(tail_ref)
        hc_ref[...] = jnp.zeros_like(hc_ref)

    x = ax_ref[...]
    cw = cw_ref[...]
    cb = cb_ref[...]
    xc = cb + cw[CONV_W - 1:CONV_W] * x
    for d in range(1, CONV_W):
        xc = xc + cw[CONV_W - 1 - d:CONV_W - d] * pltpu.roll(x, d, axis=0)
    xc_ref[...] = xc
    x8 = x[0:SUBLANES]
    t8 = tail_ref[...]
    row8 = lax.broadcasted_iota(jnp.int32, x8.shape, 0)
    xc8 = cb + cw[CONV_W - 1:CONV_W] * x8
    for d in range(1, CONV_W):
        sh = jnp.where(row8 < d, pltpu.roll(t8, d, axis=0), pltpu.roll(x8, d, axis=0))
        xc8 = xc8 + cw[CONV_W - 1 - d:CONV_W - d] * sh
    xc_ref[0:SUBLANES, :] = xc8
    tail_ref[...] = x[ts - SUBLANES:ts]

    sp = _softplus(-lam_ref[...])
    row = lax.broadcasted_iota(jnp.int32, (ts, LRU_BLOCK_W), 0)
    for n in range(x.shape[1] // LRU_BLOCK_W):
        sl = slice(n * LRU_BLOCK_W, (n + 1) * LRU_BLOCK_W)
        xcn = xc_ref[:, sl]
        xb = xcn.astype(BF16)
        r = jax.nn.sigmoid(_dot(xb, gw_ref[0, n]) + gb_ref[0:1, sl])
        i = jax.nn.sigmoid(_dot(xb, gw_ref[1, n]) + gb_ref[1:2, sl])
        log_a = -RGLRU_C * r * sp[:, sl]
        a = jnp.exp(log_a)
        b = jnp.sqrt(-jnp.tanh(log_a) * (a * a + 1.0)) * (i * xcn)
        d = 1
        while d < ts:
            a_sh = jnp.where(row < d, 1.0, pltpu.roll(a, d, axis=0))
            b_sh = jnp.where(row < d, 0.0, pltpu.roll(b, d, axis=0))
            b = a * b_sh + b
            a = a * a_sh
            d *= 2
        hs = b + a * hc_ref[:, sl]
        hc_ref[:, sl] = hs[ts - 1:ts]
        o_ref[:, sl] = (hs * jax.nn.gelu(ay_ref[:, sl])).astype(o_ref.dtype)


def _rglru(axy, conv_w, conv_b, gate_w, gate_b, lam, batch, *, ts=256):
    T, W2 = axy.shape
    W = W2 // 2
    ns = T // batch // ts
    return pl.pallas_call(
        functools.partial(_rglru_kernel, ts=ts),
        grid=(batch, ns),
        in_specs=[
            pl.BlockSpec((ts, W), lambda b, s: (b * ns + s, 0)),
            pl.BlockSpec((ts, W), lambda b, s: (b * ns + s, 1)),
            pl.BlockSpec((CONV_W, W), lambda b, s: (0, 0)),
            pl.BlockSpec((1, W), lambda b, s: (0, 0)),
            pl.BlockSpec(gate_w.shape, lambda b, s: (0, 0, 0, 0)),
            pl.BlockSpec((2, W), lambda b, s: (0, 0)),
            pl.BlockSpec((1, W), lambda b, s: (0, 0)),
        ],
        out_specs=pl.BlockSpec((ts, W), lambda b, s: (b * ns + s, 0)),
        out_shape=jax.ShapeDtypeStruct((T, W), BF16),
        scratch_shapes=[pltpu.VMEM((ts, W), F32), pltpu.VMEM((SUBLANES, W), F32), pltpu.VMEM((1, W), F32)],
        compiler_params=_params("parallel", "arbitrary"),
        name="rglru",
    )(axy, axy, conv_w, conv_b, gate_w, gate_b, lam)


def _fox_prep_kernel(fl_ref, fb_ref, colb_ref, row_ref):
    x = fl_ref[...] + fb_ref[...]
    c = jnp.minimum(x, 0.0) - jnp.log1p(jnp.exp(-jnp.abs(x)))
    S = c.shape[0]
    row = lax.broadcasted_iota(jnp.int32, c.shape, 0)
    d = 1
    while d < S:
        c = c + jnp.where(row < d, 0.0, pltpu.roll(c, d, axis=0))
        d *= 2
    c = c * LOG2E
    for h in range(FOX_HEADS):
        colb_ref[h] = jnp.broadcast_to(c[:, h:h + 1], c.shape)
    for k in range(S // LANES):
        row_ref[:, k * LANES:(k + 1) * LANES] = c[k * LANES:(k + 1) * LANES, :].T[0:SUBLANES]


def _fox_prep(fl, f_bias, batch):
    T = fl.shape[0]
    S = T // batch
    return pl.pallas_call(
        _fox_prep_kernel,
        grid=(batch,),
        in_specs=[pl.BlockSpec((S, LANES), lambda b: (b, 0)), pl.BlockSpec((1, LANES), lambda b: (0, 0))],
        out_specs=[pl.BlockSpec((None, FOX_HEADS, S, LANES), lambda b: (b, 0, 0, 0)),
                   pl.BlockSpec((None, SUBLANES, S), lambda b: (b, 0, 0))],
        out_shape=[jax.ShapeDtypeStruct((batch, FOX_HEADS, S, LANES), F32),
                   jax.ShapeDtypeStruct((batch, SUBLANES, S), F32)],
        compiler_params=_params("parallel"),
        name="fox_prep",
    )(fl, f_bias)


def _t5_bucket(rel):
    nb = REL_BUCKETS // 2
    max_exact = nb // 2
    n = jnp.abs(rel)
    large = max_exact + (jnp.log(jnp.maximum(n, max_exact).astype(jnp.float32) / max_exact)
                         / math.log(REL_MAX_DIST / max_exact) * (nb - max_exact)).astype(jnp.int32)
    large = jnp.minimum(large, nb - 1)
    return jnp.where(rel > 0, nb, 0) + jnp.where(n < max_exact, n, large)


def _bias_tile_types(tq, tk):
    return -(-(tk - 1 + REL_MAX_DIST) // tq) + 1


def _bucket_tiles(tq, tk):
    nt = _bias_tile_types(tq, tk)
    t = jnp.arange(nt, dtype=jnp.int32)[:, None, None]
    c = jnp.arange(tk, dtype=jnp.int32)[None, :, None]
    r = jnp.arange(tq, dtype=jnp.int32)[None, None, :] + t * tq
    allowed = (c // CHUNK) <= (r // CHUNK)
    return jnp.where(allowed, _t5_bucket(c - r), -1)


def _bias_table_kernel(rb_ref, bucket_ref, o_ref):
    h = pl.program_id(0)
    bucket = bucket_ref[...]
    out = jnp.full(bucket.shape, NEG_INF, F32)
    for b in range(REL_BUCKETS):
        out = jnp.where(bucket == b, rb_ref[b, h] * LOG2E, out)
    o_ref[...] = out


def _bias_table(rel_bias, tq, tk):
    buckets = _bucket_tiles(tq, tk)
    nt = buckets.shape[0]
    H = rel_bias.shape[1]
    return pl.pallas_call(
        _bias_table_kernel,
        grid=(H, nt),
        in_specs=[
            pl.BlockSpec(memory_space=pltpu.SMEM),
            pl.BlockSpec((None, tk, tq), lambda h, t: (t, 0, 0)),
        ],
        out_specs=pl.BlockSpec((None, None, tk, tq), lambda h, t: (h, t, 0, 0)),
        out_shape=jax.ShapeDtypeStruct((H, nt, tk, tq), F32),
        compiler_params=_params("parallel", "parallel"),
        name="bias_table",
    )(rel_bias, buckets)


ONES_ROWS = 16
ATTN_HEADS_PER_STEP = 2


def _flash_loop(n_steps, first_logits, next_logits, vt1, sa_ref, sb_ref, m_ref, acc_ref):
    heads = range(m_ref.shape[0])
    dv = acc_ref.shape[1] - ONES_ROWS

    def fill(s_ref, j):
        for h in heads:
            s_ref[h] = next_logits(h, j)

    def update(s_ref, j):
        for h in heads:
            s, m = s_ref[h], m_ref[h]
            m_new = jnp.maximum(m, jnp.max(s, axis=0, keepdims=True))
            p = jnp.exp2(s - m_new).astype(BF16)
            acc_ref[h] = jnp.exp2(m - m_new) * acc_ref[h] + _dot(vt1(h, j), p)
            m_ref[h] = m_new

    m_ref[...] = jnp.full(m_ref.shape, NEG_INF, F32)
    acc_ref[...] = jnp.zeros(acc_ref.shape, F32)
    for h in heads:
        sa_ref[h] = first_logits(h)

    def pair(i, _):
        fill(sb_ref, 2 * i + 1)
        update(sa_ref, 2 * i)
        fill(sa_ref, 2 * i + 2)
        update(sb_ref, 2 * i + 1)
        return 0

    n_pairs = (n_steps - 1) // 2
    lax.fori_loop(0, n_pairs, pair, 0)
    j = 2 * n_pairs

    @pl.when(j + 2 == n_steps)
    def _():
        fill(sb_ref, j + 1)
        update(sa_ref, j)
        update(sb_ref, j + 1)

    @pl.when(j + 1 == n_steps)
    def _():
        update(sa_ref, j)

    return [acc_ref[h, :dv, :] / acc_ref[h, dv:dv + 1, :] for h in heads]


def _flash_scratch(n_heads, tk, n, dv):
    return [pltpu.VMEM((n_heads, tk, n), F32), pltpu.VMEM((n_heads, tk, n), F32),
            pltpu.VMEM((n_heads, 1, n), F32), pltpu.VMEM((n_heads, dv + ONES_ROWS, n), F32)]


def _store_transposed(vt_ref, v_ref):
    n_heads, n_tiles, rows, tk = vt_ref.shape
    dv = rows - ONES_ROWS
    eye = (lax.broadcasted_iota(jnp.int32, (dv, dv), 0) == lax.broadcasted_iota(jnp.int32, (dv, dv), 1)).astype(BF16)
    for h in range(n_heads):
        for j in range(n_tiles):
            vt_ref[h, j, :dv, :] = _dot_nt(eye, v_ref[j * tk:(j + 1) * tk, h * dv:(h + 1) * dv]).astype(BF16)
            vt_ref[h, j, dv:, :] = jnp.ones((ONES_ROWS, tk), BF16)


def _diff_attn_kernel(q_ref, k_ref, v_ref, bias_ref, dl_ref, sg_ref, o_ref, vt_ref, sa_ref, sb_ref, m_ref, acc_ref,
                      *, tq, tk, nt, lam_init):
    @pl.when(pl.program_id(2) == 0)
    def _():
        _store_transposed(vt_ref, v_ref)

    nh = ATTN_HEADS_PER_STEP
    q0 = pl.program_id(2) * tq
    lane = lax.broadcasted_iota(jnp.int32, (tq, LANES), 1)
    qq = []
    for h in range(nh):
        q = q_ref[:, h * LANES:(h + 1) * LANES]
        qq.append(jnp.concatenate([jnp.where(lane < DIFF_DH, q, 0), jnp.where(lane >= DIFF_DH, q, 0)], axis=0))

    def logits(h, j):
        k0 = j * tk if isinstance(j, int) else pl.multiple_of(j * tk, tk)
        s = _dot_nt(k_ref[pl.ds(k0, tk), h * LANES:(h + 1) * LANES], qq[h])
        bias = bias_ref[h, jnp.minimum((q0 - k0) // tq, nt - 1)]
        return s + jnp.concatenate([bias, bias], axis=1)

    outs = _flash_loop(q0 // tk + 1, functools.partial(logits, j=0), logits, lambda h, j: vt_ref[h, j],
                       sa_ref, sb_ref, m_ref, acc_ref)
    dl = dl_ref[...]
    lam = (jnp.exp(jnp.sum(dl[0:1] * dl[1:2], axis=-1, keepdims=True))
           - jnp.exp(jnp.sum(dl[2:3] * dl[3:4], axis=-1, keepdims=True)) + lam_init)
    for h in range(nh):
        o = outs[h]
        d = o[:, :tq] - lam * o[:, tq:]
        y = d * lax.rsqrt(jnp.mean(d * d, axis=0, keepdims=True) + NORM_EPS) * sg_ref[...]
        o_ref[:, h * DIFF_VD:(h + 1) * DIFF_VD] = (y * (1.0 - lam_init)).T.astype(o_ref.dtype)


def _diff_attn(qkv, bias_table, diff_lambda, subln_g, batch, lam_init, *, tq, tk):
    T = qkv.shape[0]
    S = T // batch
    nq = S // tq
    nt = bias_table.shape[1]
    nh = ATTN_HEADS_PER_STEP
    G = DIFF_HEADS // nh
    W = nh * LANES
    return pl.pallas_call(
        functools.partial(_diff_attn_kernel, tq=tq, tk=tk, nt=nt, lam_init=lam_init),
        grid=(batch, G, nq),
        in_specs=[
            pl.BlockSpec((tq, W), lambda b, g, i: (b * nq + i, g)),
            pl.BlockSpec((S, W), lambda b, g, i: (b, G + g)),
            pl.BlockSpec((S, W), lambda b, g, i: (b, 2 * G + g)),
            pl.BlockSpec((nh, nt, tk, tq), lambda b, g, i: (g, 0, 0, 0)),
            pl.BlockSpec(diff_lambda.shape, lambda b, g, i: (0, 0)),
            pl.BlockSpec((DIFF_VD, 1), lambda b, g, i: (0, 0)),
        ],
        out_specs=pl.BlockSpec((tq, nh * DIFF_VD), lambda b, g, i: (b * nq + i, g)),
        out_shape=jax.ShapeDtypeStruct((T, DIFF_HEADS * DIFF_VD), BF16),
        scratch_shapes=([pltpu.VMEM((nh, S // tk, DIFF_VD + ONES_ROWS, tk), BF16)]
                        + _flash_scratch(nh, tk, 2 * tq, DIFF_VD)),
        compiler_params=_params("parallel", "parallel", "arbitrary"),
        name="diff_attn",
    )(qkv, qkv, qkv, bias_table, diff_lambda, subln_g)


def _fox_attn_kernel(q_ref, k_ref, v_ref, ck_ref, cq_ref, o_ref, vt_ref, sa_ref, sb_ref, m_ref, acc_ref, *, tq, tk):
    @pl.when(pl.program_id(2) == 0)
    def _():
        _store_transposed(vt_ref, v_ref)

    nh = ATTN_HEADS_PER_STEP
    q0 = pl.program_id(2) * tq
    h0 = pl.program_id(1) * nh
    qs = [q_ref[:, h * LANES:(h + 1) * LANES] for h in range(nh)]
    cqs = [cq_ref[pl.ds(h0 + h, 1), :] for h in range(nh)]

    n_full = q0 // tk

    def logits(h, tile, masked):
        k0 = pl.multiple_of(tile * tk, tk)
        ck = ck_ref[h, pl.ds(k0, tk), :]
        s = _dot_nt(k_ref[pl.ds(k0, tk), h * LANES:(h + 1) * LANES], qs[h])
        s = s + (cqs[h] - jnp.concatenate([ck] * (tq // LANES), axis=1))
        if masked:
            kpos = lax.broadcasted_iota(jnp.int32, s.shape, 0) + k0
            qpos = lax.broadcasted_iota(jnp.int32, s.shape, 1) + q0
            s = jnp.where(kpos <= qpos, s, NEG_INF)
        return s

    tile_of = lambda j: jnp.where(j == 0, n_full, j - 1)
    outs = _flash_loop(n_full + 1, lambda h: logits(h, n_full, True), lambda h, j: logits(h, j - 1, False),
                       lambda h, j: vt_ref[h, tile_of(j)], sa_ref, sb_ref, m_ref, acc_ref)
    for h in range(nh):
        o_ref[:, h * FOX_DH:(h + 1) * FOX_DH] = outs[h].T.astype(o_ref.dtype)


def _fox_attn(qkv, cum_colb, cum_row, batch, *, tq, tk):
    T = qkv.shape[0]
    S = T // batch
    nq = S // tq
    nh = ATTN_HEADS_PER_STEP
    G = FOX_HEADS // nh
    W = nh * LANES
    base = 3 * DIFF_HEADS // nh
    return pl.pallas_call(
        functools.partial(_fox_attn_kernel, tq=tq, tk=tk),
        grid=(batch, G, nq),
        in_specs=[
            pl.BlockSpec((tq, W), lambda b, g, i: (b * nq + i, base + g)),
            pl.BlockSpec((S, W), lambda b, g, i: (b, base + G + g)),
            pl.BlockSpec((S, W), lambda b, g, i: (b, base + 2 * G + g)),
            pl.BlockSpec((None, nh, S, LANES), lambda b, g, i: (b, g, 0, 0)),
            pl.BlockSpec((None, SUBLANES, tq), lambda b, g, i: (b, 0, i)),
        ],
        out_specs=pl.BlockSpec((tq, nh * FOX_DH), lambda b, g, i: (b * nq + i, g)),
        out_shape=jax.ShapeDtypeStruct((T, FOX_HEADS * FOX_DH), BF16),
        scratch_shapes=([pltpu.VMEM((nh, S // tk, FOX_DH + ONES_ROWS, tk), BF16)]
                        + _flash_scratch(nh, tk, tq, FOX_DH)),
        compiler_params=_params("parallel", "parallel", "arbitrary"),
        name="fox_attn",
    )(qkv, qkv, qkv, cum_colb, cum_row)


def _xattn_kernel(q_ref, k_ref, v_ref, o_ref, *, scale):
    s = _dot_nt(q_ref[...], k_ref[...]) * scale
    p = jnp.exp(s - jnp.max(s, axis=-1, keepdims=True))
    l = jnp.sum(p, axis=-1, keepdims=True)
    o_ref[...] = (_dot(p.astype(BF16), v_ref[...]) / l).astype(o_ref.dtype)


def _xattn(q, kv, batch, *, tq=512):
    T, D = q.shape
    M = kv.shape[0] // batch
    H = XATTN_HEADS
    dh = D // H
    nq = T // batch // tq
    return pl.pallas_call(
        functools.partial(_xattn_kernel, scale=dh ** -0.5),
        grid=(batch, nq, H),
        in_specs=[
            pl.BlockSpec((tq, dh), lambda b, i, h: (b * nq + i, h)),
            pl.BlockSpec((M, dh), lambda b, i, h: (b, h)),
            pl.BlockSpec((M, dh), lambda b, i, h: (b, H + h)),
        ],
        out_specs=pl.BlockSpec((tq, dh), lambda b, i, h: (b * nq + i, h)),
        out_shape=jax.ShapeDtypeStruct((T, D), BF16),
        compiler_params=_params("parallel", "parallel", "parallel"),
        name="xattn",
    )(q, kv, kv)


def _proj_norm_res_kernel(*refs, n_in):
    x_refs, w_refs = refs[:n_in], refs[n_in:2 * n_in]
    g_ref, h_ref, o_ref = refs[2 * n_in:]
    y = _dot(x_refs[0][...], w_refs[0][...])
    for x_ref, w_ref in zip(x_refs[1:], w_refs[1:]):
        y = y + _dot(x_ref[...], w_ref[...])
    o_ref[...] = h_ref[...] + _rms(y, g_ref[...])


def _proj_norm_res(xs, w, layer, g, h, *, tm=512):
    T, N = h.shape
    n_in = len(xs)
    x_specs, w_specs, row0 = [], [], 0
    for x in xs:
        k = x.shape[1]
        x_specs.append(pl.BlockSpec((tm, k), lambda i: (i, 0)))
        w_specs.append(pl.BlockSpec((None, k, N), functools.partial(lambda i, r: (layer, r, 0), r=row0 // k)))
        row0 += k
    return pl.pallas_call(
        functools.partial(_proj_norm_res_kernel, n_in=n_in),
        grid=(T // tm,),
        in_specs=x_specs + w_specs + [pl.BlockSpec((1, N), lambda i: (0, 0)), pl.BlockSpec((tm, N), lambda i: (i, 0))],
        out_specs=pl.BlockSpec((tm, N), lambda i: (i, 0)),
        out_shape=jax.ShapeDtypeStruct((T, N), F32),
        compiler_params=_params("parallel"),
        name="proj_norm_res",
    )(*xs, *([w] * n_in), g, h)


def kernel(x, mem, norm_g, ffn1_w_in, ffn1_w_out, w_in, conv_w, conv_b, lru_gate_w, lru_gate_b, lru_lambda,
           diff_lambda, diff_subln_g, fox_f_bias, rel_bias, w_out, xattn_w_q, xattn_w_kv, xattn_w_o,
           ffn2_w_in, ffn2_w_out):
    B, S, D = x.shape
    depth = norm_g.shape[0]
    T = B * S
    lru_w = conv_w.shape[-1]
    n_axy = 2 * lru_w
    n_qkv = 3 * DIFF_HEADS * DIFF_VD + 3 * FOX_HEADS * FOX_DH
    diff_tq, diff_tk = 512, 512
    fox_tq, fox_tk = 512, 512

    h = x.astype(F32).reshape(T, D)
    mem2 = mem.astype(F32).reshape(B * mem.shape[1], D)
    ones_g = jnp.ones((1, D), F32)

    bf = lambda a: a.astype(BF16)
    ffn1_w_in, ffn1_w_out, ffn2_w_in, ffn2_w_out = bf(ffn1_w_in), bf(ffn1_w_out), bf(ffn2_w_in), bf(ffn2_w_out)
    w_in_b, w_out_b, gate_w_b = bf(w_in), bf(w_out), bf(lru_gate_w)
    w_q_b, w_kv_b, w_o_b = bf(xattn_w_q), bf(xattn_w_kv), bf(xattn_w_o)
    w_fl = bf(jnp.pad(w_in[:, :, n_axy + n_qkv:], ((0, 0), (0, 0), (0, LANES - FOX_HEADS))))
    f_bias = jnp.pad(fox_f_bias, ((0, 0), (0, LANES - FOX_HEADS)))
    n_dq, n_fq = DIFF_HEADS * 2 * DIFF_DH, FOX_HEADS * FOX_DH
    fq0 = 3 * DIFF_HEADS * DIFF_VD
    col_scale = jnp.ones((1, n_qkv), F32)
    col_scale = col_scale.at[:, :n_dq].set(LOG2E * DIFF_DH ** -0.5).at[:, fq0:fq0 + n_fq].set(LOG2E * FOX_DH ** -0.5)

    bias_table = _bias_table(rel_bias, diff_tq, diff_tk)

    for l in range(depth):
        g = norm_g[l]
        lam_init = 0.8 - 0.6 * math.exp(-0.3 * l)
        h = _ffn(h, g[0:1], g[1:2], ffn1_w_in, ffn1_w_out, l)

        axy, qkv, fl = _in_proj(h, g[2:3], w_in_b, w_fl, col_scale, l, n_axy, n_qkv)
        a_out = _rglru(axy, conv_w[l], conv_b[l][None], gate_w_b[l], lru_gate_b[l], lru_lambda[l][None], B)
        cum_colb, cum_row = _fox_prep(fl, f_bias[l][None], B)
        d_out = _diff_attn(qkv, bias_table, diff_lambda[l], diff_subln_g[l][:, None], B, lam_init,
                           tq=diff_tq, tk=diff_tk)
        f_out = _fox_attn(qkv, cum_colb, cum_row, B, tq=fox_tq, tk=fox_tk)
        h = _proj_norm_res([a_out, d_out, f_out], w_out_b, l, g[3:4], h)

        q = _norm_matmul(h, g[4:5], w_q_b, l, D, BF16)
        kv = _norm_matmul(mem2, ones_g, w_kv_b, l, 2 * D, BF16, normalize=False)
        xo = _xattn(q, kv, B)
        h = _proj_norm_res([xo], w_o_b, l, g[5:6], h)

        h = _ffn(h, g[6:7], g[7:8], ffn2_w_in, ffn2_w_out, l)
    return h.reshape(B, S, D).astype(x.dtype)
```

```python
import functools
import math

import jax
import jax.numpy as jnp
import numpy as np
from jax import lax
from jax.experimental import pallas as pl
from jax.experimental.pallas import tpu as pltpu

F32 = jnp.float32
BF16 = jnp.bfloat16

CHUNK = 64
CONV_W = 4
LRU_BLOCK_W = 128
RGLRU_C = 8.0
DIFF_HEADS = 4
DIFF_DH = 64
DIFF_VD = 128
FOX_HEADS = 4
FOX_DH = 128
XATTN_HEADS = 4
REL_BUCKETS = 32
REL_MAX_DIST = 128
NORM_EPS = 1e-6
NEG_INF = -1e30
FFN_RESIDUAL_WEIGHT = 0.5
LOG2E = math.log2(math.e)

LANES = 128
SUBLANES = 8
V7X_VMEM_BYTES = 64 * 1024 * 1024
VMEM_LIMIT_BYTES = (V7X_VMEM_BYTES * 3) // 4


def _params(*semantics):
    return pltpu.CompilerParams(dimension_semantics=semantics, vmem_limit_bytes=VMEM_LIMIT_BYTES)


def _rms(x, g):
    return x * lax.rsqrt(jnp.mean(x * x, axis=-1, keepdims=True) + NORM_EPS) * g


def _dot(a, b):
    return jnp.dot(a, b, preferred_element_type=F32)


def _dot_nt(a, b):
    return lax.dot_general(a, b, (((1,), (1,)), ((), ())), preferred_element_type=F32)


def _ffn_kernel(x_ref, gpre_ref, gpost_ref, wg_ref, wu_ref, wo_ref, o_ref, xn_ref, acc_ref):
    j = pl.program_id(1)

    @pl.when(j == 0)
    def _():
        xn_ref[...] = _rms(x_ref[...], gpre_ref[...]).astype(BF16)
        acc_ref[...] = jnp.zeros_like(acc_ref)

    xn = xn_ref[...]
    gate = _dot(xn, wg_ref[...])
    up = _dot(xn, wu_ref[...])
    act = (gate * jax.nn.sigmoid(gate) * up).astype(BF16)
    acc_ref[...] += _dot(act, wo_ref[...])

    @pl.when(j == pl.num_programs(1) - 1)
    def _():
        o_ref[...] = x_ref[...] + FFN_RESIDUAL_WEIGHT * _rms(acc_ref[...], gpost_ref[...])


def _ffn(h, g_pre, g_post, w_in, w_out, layer, *, tm=512, tf=512):
    T, D = h.shape
    F = w_out.shape[1]
    nf = F // tf
    return pl.pallas_call(
        _ffn_kernel,
        grid=(T // tm, nf),
        in_specs=[
            pl.BlockSpec((tm, D), lambda i, j: (i, 0)),
            pl.BlockSpec((1, D), lambda i, j: (0, 0)),
            pl.BlockSpec((1, D), lambda i, j: (0, 0)),
            pl.BlockSpec((None, D, tf), lambda i, j: (layer, 0, j)),
            pl.BlockSpec((None, D, tf), lambda i, j: (layer, 0, j + nf)),
            pl.BlockSpec((None, tf, D), lambda i, j: (layer, j, 0)),
        ],
        out_specs=pl.BlockSpec((tm, D), lambda i, j: (i, 0)),
        out_shape=jax.ShapeDtypeStruct((T, D), F32),
        scratch_shapes=[pltpu.VMEM((tm, D), BF16), pltpu.VMEM((tm, D), F32)],
        compiler_params=_params("parallel", "arbitrary"),
        name="ffn",
    )(h, g_pre, g_post, w_in, w_in, w_out)


def _norm_matmul_kernel(x_ref, g_ref, w_ref, o_ref, xn_ref, *, normalize):
    @pl.when(pl.program_id(1) == 0)
    def _():
        x = x_ref[...]
        if normalize:
            x = _rms(x, g_ref[...])
        xn_ref[...] = x.astype(BF16)

    o_ref[...] = _dot(xn_ref[...], w_ref[...]).astype(o_ref.dtype)


def _norm_matmul(x, g, w, layer, n_cols, out_dtype, *, normalize=True, tm=512, tn=1024):
    T, K = x.shape
    tn = min(tn, n_cols)
    return pl.pallas_call(
        functools.partial(_norm_matmul_kernel, normalize=normalize),
        grid=(T // tm, n_cols // tn),
        in_specs=[
            pl.BlockSpec((tm, K), lambda i, j: (i, 0)),
            pl.BlockSpec((1, K), lambda i, j: (0, 0)),
            pl.BlockSpec((None, K, tn), lambda i, j: (layer, 0, j)),
        ],
        out_specs=pl.BlockSpec((tm, tn), lambda i, j: (i, j)),
        out_shape=jax.ShapeDtypeStruct((T, n_cols), out_dtype),
        scratch_shapes=[pltpu.VMEM((tm, K), BF16)],
        compiler_params=_params("parallel", "arbitrary"),
        name="norm_matmul",
    )(x, g, w)


def _in_proj_kernel(x_ref, g_ref, w_ref, wfl_ref, cs_ref, axy_ref, qkv_ref, fl_ref, xn_ref, *, n_axy_tiles):
    j = pl.program_id(1)

    @pl.when(j == 0)
    def _():
        xn = _rms(x_ref[...], g_ref[...]).astype(BF16)
        xn_ref[...] = xn
        fl_ref[...] = _dot(xn, wfl_ref[...])

    @pl.when(j < n_axy_tiles)
    def _():
        axy_ref[...] = _dot(xn_ref[...], w_ref[...])

    @pl.when(j >= n_axy_tiles)
    def _():
        qkv_ref[...] = (_dot(xn_ref[...], w_ref[...]) * cs_ref[...]).astype(BF16)


def _in_proj(x, g, w, w_fl, col_scale, layer, n_axy, n_qkv, *, tm=512, tn=1024):
    T, K = x.shape
    na, nq = n_axy // tn, n_qkv // tn
    return pl.pallas_call(
        functools.partial(_in_proj_kernel, n_axy_tiles=na),
        grid=(T // tm, na + nq),
        in_specs=[
            pl.BlockSpec((tm, K), lambda i, j: (i, 0)),
            pl.BlockSpec((1, K), lambda i, j: (0, 0)),
            pl.BlockSpec((None, K, tn), lambda i, j: (layer, 0, j)),
            pl.BlockSpec((None, K, LANES), lambda i, j: (layer, 0, 0)),
            pl.BlockSpec((1, tn), lambda i, j: (0, jnp.maximum(j - na, 0))),
        ],
        out_specs=[
            pl.BlockSpec((tm, tn), lambda i, j: (i, jnp.minimum(j, na - 1))),
            pl.BlockSpec((tm, tn), lambda i, j: (i, jnp.maximum(j - na, 0))),
            pl.BlockSpec((tm, LANES), lambda i, j: (i, 0)),
        ],
        out_shape=[
            jax.ShapeDtypeStruct((T, n_axy), F32),
            jax.ShapeDtypeStruct((T, n_qkv), BF16),
            jax.ShapeDtypeStruct((T, LANES), F32),
        ],
        scratch_shapes=[pltpu.VMEM((tm, K), BF16)],
        compiler_params=_params("parallel", "arbitrary"),
        name="in_proj",
    )(x, g, w, w_fl, col_scale)


def _softplus(x):
    return jnp.maximum(x, 0.0) + jnp.log1p(jnp.exp(-jnp.abs(x)))


def _rglru_kernel(ax_ref, ay_ref, cw_ref, cb_ref, gw_ref, gb_ref, lam_ref, o_ref,
                  xc_ref, tail_ref, hc_ref, *, ts):
    @pl.when(pl.program_id(1) == 0)
    def _():
        tail_ref[...] = jnp.zeros_like(tail_ref)
        hc_ref[...] = jnp.zeros_like(hc_ref)

    x = ax_ref[...]
    cw = cw_ref[...]
    cb = cb_ref[...]
    xc = cb + cw[CONV_W - 1:CONV_W] * x
    for d in range(1, CONV_W):
        xc = xc + cw[CONV_W - 1 - d:CONV_W - d] * pltpu.roll(x, d, axis=0)
    xc_ref[...] = xc
    x8 = x[0:SUBLANES]
    t8 = tail_ref[...]
    row8 = lax.broadcasted_iota(jnp.int32, x8.shape, 0)
    xc8 = cb + cw[CONV_W - 1:CONV_W] * x8
    for d in range(1, CONV_W):
        sh = jnp.where(row8 < d, pltpu.roll(t8, d, axis=0), pltpu.roll(x8, d, axis=0))
        xc8 = xc8 + cw[CONV_W - 1 - d:CONV_W - d] * sh
    xc_ref[0:SUBLANES, :] = xc8
    tail_ref[...] = x[ts - SUBLANES:ts]

    sp = _softplus(-lam_ref[...])
    row = lax.broadcasted_iota(jnp.int32, (ts // SUBLANES, SUBLANES, LRU_BLOCK_W), 1)
    for n in range(x.shape[1] // LRU_BLOCK_W):
        sl = slice(n * LRU_BLOCK_W, (n + 1) * LRU_BLOCK_W)
        xcn = xc_ref[:, sl]
        xb = xcn.astype(BF16)
        r = jax.nn.sigmoid(_dot(xb, gw_ref[0, n]) + gb_ref[0:1, sl])
        i = jax.nn.sigmoid(_dot(xb, gw_ref[1, n]) + gb_ref[1:2, sl])
        log_a = -RGLRU_C * r * sp[:, sl]
        a = jnp.exp(log_a)
        b = jnp.sqrt(-jnp.tanh(log_a) * (a * a + 1.0)) * (i * xcn)
        groups = ts // SUBLANES
        a = a.reshape(groups, SUBLANES, LRU_BLOCK_W)
        b = b.reshape(groups, SUBLANES, LRU_BLOCK_W)
        d = 1
        while d < SUBLANES:
            a_sh = jnp.where(row < d, 1.0, pltpu.roll(a, d, axis=1))
            b_sh = jnp.where(row < d, 0.0, pltpu.roll(b, d, axis=1))
            b = a * b_sh + b
            a = a * a_sh
            d *= 2
        carry = hc_ref[:, sl]
        hs = []
        for gi in range(groups):
            hs.append(b[gi] + a[gi] * carry)
            carry = hs[-1][SUBLANES - 1:SUBLANES]
        hc_ref[:, sl] = carry
        o_ref[:, sl] = (jnp.concatenate(hs, axis=0) * jax.nn.gelu(ay_ref[:, sl])).astype(o_ref.dtype)


def _rglru(axy, conv_w, conv_b, gate_w, gate_b, lam, batch, *, ts=256):
    T, W2 = axy.shape
    W = W2 // 2
    ns = T // batch // ts
    return pl.pallas_call(
        functools.partial(_rglru_kernel, ts=ts),
        grid=(batch, ns),
        in_specs=[
            pl.BlockSpec((ts, W), lambda b, s: (b * ns + s, 0)),
            pl.BlockSpec((ts, W), lambda b, s: (b * ns + s, 1)),
            pl.BlockSpec((CONV_W, W), lambda b, s: (0, 0)),
            pl.BlockSpec((1, W), lambda b, s: (0, 0)),
            pl.BlockSpec(gate_w.shape, lambda b, s: (0, 0, 0, 0)),
            pl.BlockSpec((2, W), lambda b, s: (0, 0)),
            pl.BlockSpec((1, W), lambda b, s: (0, 0)),
        ],
        out_specs=pl.BlockSpec((ts, W), lambda b, s: (b * ns + s, 0)),
        out_shape=jax.ShapeDtypeStruct((T, W), BF16),
        scratch_shapes=[pltpu.VMEM((ts, W), F32), pltpu.VMEM((SUBLANES, W), F32), pltpu.VMEM((1, W), F32)],
        compiler_params=_params("parallel", "arbitrary"),
        name="rglru",
    )(axy, axy, conv_w, conv_b, gate_w, gate_b, lam)


def _fox_prep_kernel(fl_ref, fb_ref, colb_ref, row_ref):
    x = fl_ref[...] + fb_ref[...]
    c = jnp.minimum(x, 0.0) - jnp.log1p(jnp.exp(-jnp.abs(x)))
    S = c.shape[0]
    row = lax.broadcasted_iota(jnp.int32, c.shape, 0)
    d = 1
    while d < S:
        c = c + jnp.where(row < d, 0.0, pltpu.roll(c, d, axis=0))
        d *= 2
    c = c * LOG2E
    for h in range(FOX_HEADS):
        colb_ref[h] = jnp.broadcast_to(c[:, h:h + 1], c.shape)
    for k in range(S // LANES):
        row_ref[:, k * LANES:(k + 1) * LANES] = c[k * LANES:(k + 1) * LANES, :].T[0:SUBLANES]


def _fox_prep(fl, f_bias, batch):
    T = fl.shape[0]
    S = T // batch
    return pl.pallas_call(
        _fox_prep_kernel,
        grid=(batch,),
        in_specs=[pl.BlockSpec((S, LANES), lambda b: (b, 0)), pl.BlockSpec((1, LANES), lambda b: (0, 0))],
        out_specs=[pl.BlockSpec((None, FOX_HEADS, S, LANES), lambda b: (b, 0, 0, 0)),
                   pl.BlockSpec((None, SUBLANES, S), lambda b: (b, 0, 0))],
        out_shape=[jax.ShapeDtypeStruct((batch, FOX_HEADS, S, LANES), F32),
                   jax.ShapeDtypeStruct((batch, SUBLANES, S), F32)],
        compiler_params=_params("parallel"),
        name="fox_prep",
    )(fl, f_bias)


def _t5_bucket(rel):
    nb = REL_BUCKETS // 2
    max_exact = nb // 2
    n = jnp.abs(rel)
    large = max_exact + (jnp.log(jnp.maximum(n, max_exact).astype(jnp.float32) / max_exact)
                         / math.log(REL_MAX_DIST / max_exact) * (nb - max_exact)).astype(jnp.int32)
    large = jnp.minimum(large, nb - 1)
    return jnp.where(rel > 0, nb, 0) + jnp.where(n < max_exact, n, large)


def _bias_tile_types(tq, tk):
    return -(-(tk - 1 + REL_MAX_DIST) // tq) + 1


def _bucket_tiles(tq, tk):
    nt = _bias_tile_types(tq, tk)
    t = jnp.arange(nt, dtype=jnp.int32)[:, None, None]
    c = jnp.arange(tk, dtype=jnp.int32)[None, :, None]
    r = jnp.arange(tq, dtype=jnp.int32)[None, None, :] + t * tq
    allowed = (c // CHUNK) <= (r // CHUNK)
    return jnp.where(allowed, _t5_bucket(c - r), -1)


def _bias_table_kernel(rb_ref, bucket_ref, o_ref):
    h = pl.program_id(0)
    bucket = bucket_ref[...]
    out = jnp.full(bucket.shape, NEG_INF, F32)
    for b in range(REL_BUCKETS):
        out = jnp.where(bucket == b, rb_ref[b, h] * LOG2E, out)
    o_ref[...] = out


def _bias_table(rel_bias, tq, tk):
    buckets = _bucket_tiles(tq, tk)
    nt = buckets.shape[0]
    H = rel_bias.shape[1]
    return pl.pallas_call(
        _bias_table_kernel,
        grid=(H, nt),
        in_specs=[
            pl.BlockSpec(memory_space=pltpu.SMEM),
            pl.BlockSpec((None, tk, tq), lambda h, t: (t, 0, 0)),
        ],
        out_specs=pl.BlockSpec((None, None, tk, tq), lambda h, t: (h, t, 0, 0)),
        out_shape=jax.ShapeDtypeStruct((H, nt, tk, tq), F32),
        compiler_params=_params("parallel", "parallel"),
        name="bias_table",
    )(rel_bias, buckets)


ONES_ROWS = 16
ATTN_HEADS_PER_STEP = 2


def _flash_loop(n_steps, first_logits, next_logits, vt1, sa_ref, sb_ref, m_ref, acc_ref):
    heads = range(m_ref.shape[0])
    dv = acc_ref.shape[1] - ONES_ROWS

    def fill(s_ref, j):
        for h in heads:
            s_ref[h] = next_logits(h, j)

    def update(s_ref, j):
        for h in heads:
            s, m = s_ref[h], m_ref[h]
            m_new = jnp.maximum(m, jnp.max(s, axis=0, keepdims=True))
            p = jnp.exp2(s - m_new).astype(BF16)
            acc_ref[h] = jnp.exp2(m - m_new) * acc_ref[h] + _dot(vt1(h, j), p)
            m_ref[h] = m_new

    m_ref[...] = jnp.full(m_ref.shape, NEG_INF, F32)
    acc_ref[...] = jnp.zeros(acc_ref.shape, F32)
    for h in heads:
        sa_ref[h] = first_logits(h)

    def pair(i, _):
        fill(sb_ref, 2 * i + 1)
        update(sa_ref, 2 * i)
        fill(sa_ref, 2 * i + 2)
        update(sb_ref, 2 * i + 1)
        return 0

    n_pairs = (n_steps - 1) // 2
    lax.fori_loop(0, n_pairs, pair, 0)
    j = 2 * n_pairs

    @pl.when(j + 2 == n_steps)
    def _():
        fill(sb_ref, j + 1)
        update(sa_ref, j)
        update(sb_ref, j + 1)

    @pl.when(j + 1 == n_steps)
    def _():
        update(sa_ref, j)

    return [acc_ref[h, :dv, :] / acc_ref[h, dv:dv + 1, :] for h in heads]


def _flash_scratch(n_heads, tk, n, dv):
    return [pltpu.VMEM((n_heads, tk, n), F32), pltpu.VMEM((n_heads, tk, n), F32),
            pltpu.VMEM((n_heads, 1, n), F32), pltpu.VMEM((n_heads, dv + ONES_ROWS, n), F32)]


def _store_transposed(vt_ref, v_ref):
    n_heads, n_tiles, rows, tk = vt_ref.shape
    dv = rows - ONES_ROWS
    eye = (lax.broadcasted_iota(jnp.int32, (dv, dv), 0) == lax.broadcasted_iota(jnp.int32, (dv, dv), 1)).astype(BF16)
    for h in range(n_heads):
        for j in range(n_tiles):
            vt_ref[h, j, :dv, :] = _dot_nt(eye, v_ref[j * tk:(j + 1) * tk, h * dv:(h + 1) * dv]).astype(BF16)
            vt_ref[h, j, dv:, :] = jnp.ones((ONES_ROWS, tk), BF16)


def _diff_attn_kernel(q_ref, k_ref, v_ref, bias_ref, dl_ref, sg_ref, o_ref, vt_ref, sa_ref, sb_ref, m_ref, acc_ref,
                      *, tq, tk, nt, lam_init):
    @pl.when(pl.program_id(2) == 0)
    def _():
        _store_transposed(vt_ref, v_ref)

    nh = ATTN_HEADS_PER_STEP
    q0 = pl.program_id(2) * tq
    lane = lax.broadcasted_iota(jnp.int32, (tq, LANES), 1)
    qq = []
    for h in range(nh):
        q = q_ref[:, h * LANES:(h + 1) * LANES]
        qq.append(jnp.concatenate([jnp.where(lane < DIFF_DH, q, 0), jnp.where(lane >= DIFF_DH, q, 0)], axis=0))

    def logits(h, j):
        k0 = j * tk if isinstance(j, int) else pl.multiple_of(j * tk, tk)
        s = _dot_nt(k_ref[pl.ds(k0, tk), h * LANES:(h + 1) * LANES], qq[h])
        bias = bias_ref[h, jnp.minimum((q0 - k0) // tq, nt - 1)]
        return s + jnp.concatenate([bias, bias], axis=1)

    outs = _flash_loop(q0 // tk + 1, functools.partial(logits, j=0), logits, lambda h, j: vt_ref[h, j],
                       sa_ref, sb_ref, m_ref, acc_ref)
    dl = dl_ref[...]
    lam = (jnp.exp(jnp.sum(dl[0:1] * dl[1:2], axis=-1, keepdims=True))
           - jnp.exp(jnp.sum(dl[2:3] * dl[3:4], axis=-1, keepdims=True)) + lam_init)
    for h in range(nh):
        o = outs[h]
        d = o[:, :tq] - lam * o[:, tq:]
        y = d * lax.rsqrt(jnp.mean(d * d, axis=0, keepdims=True) + NORM_EPS) * sg_ref[...]
        o_ref[:, h * DIFF_VD:(h + 1) * DIFF_VD] = (y * (1.0 - lam_init)).T.astype(o_ref.dtype)


def _diff_attn(qkv, bias_table, diff_lambda, subln_g, batch, lam_init, *, tq, tk):
    T = qkv.shape[0]
    S = T // batch
    nq = S // tq
    nt = bias_table.shape[1]
    nh = ATTN_HEADS_PER_STEP
    G = DIFF_HEADS // nh
    W = nh * LANES
    return pl.pallas_call(
        functools.partial(_diff_attn_kernel, tq=tq, tk=tk, nt=nt, lam_init=lam_init),
        grid=(batch, G, nq),
        in_specs=[
            pl.BlockSpec((tq, W), lambda b, g, i: (b * nq + i, g)),
            pl.BlockSpec((S, W), lambda b, g, i: (b, G + g)),
            pl.BlockSpec((S, W), lambda b, g, i: (b, 2 * G + g)),
            pl.BlockSpec((nh, nt, tk, tq), lambda b, g, i: (g, 0, 0, 0)),
            pl.BlockSpec(diff_lambda.shape, lambda b, g, i: (0, 0)),
            pl.BlockSpec((DIFF_VD, 1), lambda b, g, i: (0, 0)),
        ],
        out_specs=pl.BlockSpec((tq, nh * DIFF_VD), lambda b, g, i: (b * nq + i, g)),
        out_shape=jax.ShapeDtypeStruct((T, DIFF_HEADS * DIFF_VD), BF16),
        scratch_shapes=([pltpu.VMEM((nh, S // tk, DIFF_VD + ONES_ROWS, tk), BF16)]
                        + _flash_scratch(nh, tk, 2 * tq, DIFF_VD)),
        compiler_params=_params("parallel", "parallel", "arbitrary"),
        name="diff_attn",
    )(qkv, qkv, qkv, bias_table, diff_lambda, subln_g)


def _fox_attn_kernel(q_ref, k_ref, v_ref, ck_ref, cq_ref, o_ref, vt_ref, sa_ref, sb_ref, m_ref, acc_ref, *, tq, tk):
    @pl.when(pl.program_id(2) == 0)
    def _():
        _store_transposed(vt_ref, v_ref)

    nh = ATTN_HEADS_PER_STEP
    q0 = pl.program_id(2) * tq
    h0 = pl.program_id(1) * nh
    qs = [q_ref[:, h * LANES:(h + 1) * LANES] for h in range(nh)]
    cqs = [cq_ref[pl.ds(h0 + h, 1), :] for h in range(nh)]

    n_full = q0 // tk

    def logits(h, tile, masked):
        k0 = pl.multiple_of(tile * tk, tk)
        ck = ck_ref[h, pl.ds(k0, tk), :]
        s = _dot_nt(k_ref[pl.ds(k0, tk), h * LANES:(h + 1) * LANES], qs[h])
        s = s + (cqs[h] - jnp.concatenate([ck] * (tq // LANES), axis=1))
        if masked:
            kpos = lax.broadcasted_iota(jnp.int32, s.shape, 0) + k0
            qpos = lax.broadcasted_iota(jnp.int32, s.shape, 1) + q0
            s = jnp.where(kpos <= qpos, s, NEG_INF)
        return s

    tile_of = lambda j: jnp.where(j == 0, n_full, j - 1)
    outs = _flash_loop(n_full + 1, lambda h: logits(h, n_full, True), lambda h, j: logits(h, j - 1, False),
                       lambda h, j: vt_ref[h, tile_of(j)], sa_ref, sb_ref, m_ref, acc_ref)
    for h in range(nh):
        o_ref[:, h * FOX_DH:(h + 1) * FOX_DH] = outs[h].T.astype(o_ref.dtype)


def _fox_attn(qkv, cum_colb, cum_row, batch, *, tq, tk):
    T = qkv.shape[0]
    S = T // batch
    nq = S // tq
    nh = ATTN_HEADS_PER_STEP
    G = FOX_HEADS // nh
    W = nh * LANES
    base = 3 * DIFF_HEADS // nh
    return pl.pallas_call(
        functools.partial(_fox_attn_kernel, tq=tq, tk=tk),
        grid=(batch, G, nq),
        in_specs=[
            pl.BlockSpec((tq, W), lambda b, g, i: (b * nq + i, base + g)),
            pl.BlockSpec((S, W), lambda b, g, i: (b, base + G + g)),
            pl.BlockSpec((S, W), lambda b, g, i: (b, base + 2 * G + g)),
            pl.BlockSpec((None, nh, S, LANES), lambda b, g, i: (b, g, 0, 0)),
            pl.BlockSpec((None, SUBLANES, tq), lambda b, g, i: (b, 0, i)),
        ],
        out_specs=pl.BlockSpec((tq, nh * FOX_DH), lambda b, g, i: (b * nq + i, g)),
        out_shape=jax.ShapeDtypeStruct((T, FOX_HEADS * FOX_DH), BF16),
        scratch_shapes=([pltpu.VMEM((nh, S // tk, FOX_DH + ONES_ROWS, tk), BF16)]
                        + _flash_scratch(nh, tk, tq, FOX_DH)),
        compiler_params=_params("parallel", "parallel", "arbitrary"),
        name="fox_attn",
    )(qkv, qkv, qkv, cum_colb, cum_row)


def _xattn_kernel(h_ref, gq_ref, go_ref, wq_ref, k_ref, v_ref, wo_ref, o_ref, *, n_heads):
    x = h_ref[...]
    dh = x.shape[1] // n_heads
    xn = _rms(x, gq_ref[...]).astype(BF16)
    q = (_dot(xn, wq_ref[...]) * (LOG2E * dh ** -0.5)).astype(BF16)
    outs = []
    for hd in range(n_heads):
        sl = slice(hd * dh, (hd + 1) * dh)
        s = _dot_nt(q[:, sl], k_ref[:, sl])
        p = jnp.exp2(s - jnp.max(s, axis=-1, keepdims=True))
        l = jnp.sum(p, axis=-1, keepdims=True)
        outs.append((_dot(p.astype(BF16), v_ref[:, sl]) / l).astype(BF16))
    y = _dot(jnp.concatenate(outs, axis=1), wo_ref[...])
    o_ref[...] = x + _rms(y, go_ref[...])


def _xattn(h, g_q, g_o, w_q, kv, w_o, layer, batch, *, tm=256):
    T, D = h.shape
    M = kv.shape[0] // batch
    ns = T // batch // tm
    resident = pl.Buffered(1)
    return pl.pallas_call(
        functools.partial(_xattn_kernel, n_heads=XATTN_HEADS),
        grid=(batch, ns),
        in_specs=[
            pl.BlockSpec((tm, D), lambda b, i: (b * ns + i, 0)),
            pl.BlockSpec((1, D), lambda b, i: (0, 0)),
            pl.BlockSpec((1, D), lambda b, i: (0, 0)),
            pl.BlockSpec((None, D, D), lambda b, i: (layer, 0, 0), pipeline_mode=resident),
            pl.BlockSpec((M, D), lambda b, i: (b, 0)),
            pl.BlockSpec((M, D), lambda b, i: (b, 1)),
            pl.BlockSpec((None, D, D), lambda b, i: (layer, 0, 0), pipeline_mode=resident),
        ],
        out_specs=pl.BlockSpec((tm, D), lambda b, i: (b * ns + i, 0)),
        out_shape=jax.ShapeDtypeStruct((T, D), F32),
        compiler_params=_params("parallel", "arbitrary"),
        name="xattn",
    )(h, g_q, g_o, w_q, kv, kv, w_o)


def _proj_norm_res_kernel(*refs, n_in):
    x_refs, w_refs = refs[:n_in], refs[n_in:2 * n_in]
    g_ref, h_ref, o_ref = refs[2 * n_in:]
    y = _dot(x_refs[0][...], w_refs[0][...])
    for x_ref, w_ref in zip(x_refs[1:], w_refs[1:]):
        y = y + _dot(x_ref[...], w_ref[...])
    o_ref[...] = h_ref[...] + _rms(y, g_ref[...])


def _proj_norm_res(xs, w, layer, g, h, *, tm=512):
    T, N = h.shape
    n_in = len(xs)
    x_specs, w_specs, row0 = [], [], 0
    for x in xs:
        k = x.shape[1]
        x_specs.append(pl.BlockSpec((tm, k), lambda i: (i, 0)))
        w_specs.append(pl.BlockSpec((None, k, N), functools.partial(lambda i, r: (layer, r, 0), r=row0 // k)))
        row0 += k
    return pl.pallas_call(
        functools.partial(_proj_norm_res_kernel, n_in=n_in),
        grid=(T // tm,),
        in_specs=x_specs + w_specs + [pl.BlockSpec((1, N), lambda i: (0, 0)), pl.BlockSpec((tm, N), lambda i: (i, 0))],
        out_specs=pl.BlockSpec((tm, N), lambda i: (i, 0)),
        out_shape=jax.ShapeDtypeStruct((T, N), F32),
        compiler_params=_params("parallel"),
        name="proj_norm_res",
    )(*xs, *([w] * n_in), g, h)


def kernel(x, mem, norm_g, ffn1_w_in, ffn1_w_out, w_in, conv_w, conv_b, lru_gate_w, lru_gate_b, lru_lambda,
           diff_lambda, diff_subln_g, fox_f_bias, rel_bias, w_out, xattn_w_q, xattn_w_kv, xattn_w_o,
           ffn2_w_in, ffn2_w_out):
    B, S, D = x.shape
    depth = norm_g.shape[0]
    T = B * S
    lru_w = conv_w.shape[-1]
    n_axy = 2 * lru_w
    n_qkv = 3 * DIFF_HEADS * DIFF_VD + 3 * FOX_HEADS * FOX_DH
    diff_tq, diff_tk = 512, 512
    fox_tq, fox_tk = 512, 512

    h = x.astype(F32).reshape(T, D)
    mem2 = mem.astype(F32).reshape(B * mem.shape[1], D)
    ones_g = jnp.ones((1, D), F32)

    bf = lambda a: a.astype(BF16)
    ffn1_w_in, ffn1_w_out, ffn2_w_in, ffn2_w_out = bf(ffn1_w_in), bf(ffn1_w_out), bf(ffn2_w_in), bf(ffn2_w_out)
    w_in_b, w_out_b, gate_w_b = bf(w_in), bf(w_out), bf(lru_gate_w)
    w_q_b, w_kv_b, w_o_b = bf(xattn_w_q), bf(xattn_w_kv), bf(xattn_w_o)
    w_fl = bf(jnp.pad(w_in[:, :, n_axy + n_qkv:], ((0, 0), (0, 0), (0, LANES - FOX_HEADS))))
    f_bias = jnp.pad(fox_f_bias, ((0, 0), (0, LANES - FOX_HEADS)))
    n_dq, n_fq = DIFF_HEADS * 2 * DIFF_DH, FOX_HEADS * FOX_DH
    fq0 = 3 * DIFF_HEADS * DIFF_VD
    col_scale = jnp.ones((1, n_qkv), F32)
    col_scale = col_scale.at[:, :n_dq].set(LOG2E * DIFF_DH ** -0.5).at[:, fq0:fq0 + n_fq].set(LOG2E * FOX_DH ** -0.5)

    bias_table = _bias_table(rel_bias, diff_tq, diff_tk)

    for l in range(depth):
        g = norm_g[l]
        lam_init = 0.8 - 0.6 * math.exp(-0.3 * l)
        h = _ffn(h, g[0:1], g[1:2], ffn1_w_in, ffn1_w_out, l)

        axy, qkv, fl = _in_proj(h, g[2:3], w_in_b, w_fl, col_scale, l, n_axy, n_qkv)
        a_out = _rglru(axy, conv_w[l], conv_b[l][None], gate_w_b[l], lru_gate_b[l], lru_lambda[l][None], B)
        cum_colb, cum_row = _fox_prep(fl, f_bias[l][None], B)
        d_out = _diff_attn(qkv, bias_table, diff_lambda[l], diff_subln_g[l][:, None], B, lam_init,
                           tq=diff_tq, tk=diff_tk)
        f_out = _fox_attn(qkv, cum_colb, cum_row, B, tq=fox_tq, tk=fox_tk)
        h = _proj_norm_res([a_out, d_out, f_out], w_out_b, l, g[3:4], h)

        kv = _norm_matmul(mem2, ones_g, w_kv_b, l, 2 * D, BF16, normalize=False)
        h = _xattn(h, g[4:5], g[5:6], w_q_b, kv, w_o_b, l, B)

        h = _ffn(h, g[6:7], g[7:8], ffn2_w_in, ffn2_w_out, l)
    return h.reshape(B, S, D).astype(x.dtype)
```

```python
import functools
import math

import jax
import jax.numpy as jnp
import numpy as np
from jax import lax
from jax.experimental import pallas as pl
from jax.experimental.pallas import tpu as pltpu

F32 = jnp.float32
BF16 = jnp.bfloat16

CHUNK = 64
CONV_W = 4
LRU_BLOCK_W = 128
RGLRU_C = 8.0
DIFF_HEADS = 4
DIFF_DH = 64
DIFF_VD = 128
FOX_HEADS = 4
FOX_DH = 128
XATTN_HEADS = 4
REL_BUCKETS = 32
REL_MAX_DIST = 128
NORM_EPS = 1e-6
NEG_INF = -1e30
FFN_RESIDUAL_WEIGHT = 0.5
LOG2E = math.log2(math.e)

LANES = 128
SUBLANES = 8
V7X_VMEM_BYTES = 64 * 1024 * 1024
VMEM_LIMIT_BYTES = (V7X_VMEM_BYTES * 3) // 4
FFN_VMEM_LIMIT_BYTES = (V7X_VMEM_BYTES * 7) // 8


def _params(*semantics, vmem_limit_bytes=VMEM_LIMIT_BYTES):
    return pltpu.CompilerParams(dimension_semantics=semantics, vmem_limit_bytes=vmem_limit_bytes)


def _rms(x, g):
    return x * lax.rsqrt(jnp.mean(x * x, axis=-1, keepdims=True) + NORM_EPS) * g


def _dot(a, b):
    return jnp.dot(a, b, preferred_element_type=F32)


def _dot_nt(a, b):
    return lax.dot_general(a, b, (((1,), (1,)), ((), ())), preferred_element_type=F32)


def _ffn_kernel(x_ref, gpre_ref, gpost_ref, wg_ref, wu_ref, wo_ref, o_ref, xn_ref):
    j = pl.program_id(1)

    @pl.when(j == 0)
    def _():
        xn_ref[...] = _rms(x_ref[...], gpre_ref[...]).astype(BF16)
        o_ref[...] = jnp.zeros_like(o_ref)

    xn = xn_ref[...]
    gate = _dot(xn, wg_ref[...])
    up = _dot(xn, wu_ref[...])
    act = (gate * jax.nn.sigmoid(gate) * up).astype(BF16)
    o_ref[...] += _dot(act, wo_ref[...])

    @pl.when(j == pl.num_programs(1) - 1)
    def _():
        o_ref[...] = x_ref[...] + FFN_RESIDUAL_WEIGHT * _rms(o_ref[...], gpost_ref[...])


def _ffn(h, g_pre, g_post, w_in, w_out, layer, *, tm=1024, tf=256):
    T, D = h.shape
    F = w_out.shape[1]
    nf = F // tf
    return pl.pallas_call(
        _ffn_kernel,
        grid=(T // tm, nf),
        in_specs=[
            pl.BlockSpec((tm, D), lambda i, j: (i, 0)),
            pl.BlockSpec((1, D), lambda i, j: (0, 0)),
            pl.BlockSpec((1, D), lambda i, j: (0, 0)),
            pl.BlockSpec((None, D, tf), lambda i, j: (layer, 0, j)),
            pl.BlockSpec((None, D, tf), lambda i, j: (layer, 0, j + nf)),
            pl.BlockSpec((None, tf, D), lambda i, j: (layer, j, 0)),
        ],
        out_specs=pl.BlockSpec((tm, D), lambda i, j: (i, 0)),
        out_shape=jax.ShapeDtypeStruct((T, D), F32),
        scratch_shapes=[pltpu.VMEM((tm, D), BF16)],
        compiler_params=_params("parallel", "arbitrary", vmem_limit_bytes=FFN_VMEM_LIMIT_BYTES),
        name="ffn",
    )(h, g_pre, g_post, w_in, w_in, w_out)


def _norm_matmul_kernel(x_ref, g_ref, w_ref, o_ref, xn_ref, *, normalize):
    @pl.when(pl.program_id(1) == 0)
    def _():
        x = x_ref[...]
        if normalize:
            x = _rms(x, g_ref[...])
        xn_ref[...] = x.astype(BF16)

    o_ref[...] = _dot(xn_ref[...], w_ref[...]).astype(o_ref.dtype)


def _norm_matmul(x, g, w, layer, n_cols, out_dtype, *, normalize=True, tm=512, tn=1024):
    T, K = x.shape
    tn = min(tn, n_cols)
    return pl.pallas_call(
        functools.partial(_norm_matmul_kernel, normalize=normalize),
        grid=(T // tm, n_cols // tn),
        in_specs=[
            pl.BlockSpec((tm, K), lambda i, j: (i, 0)),
            pl.BlockSpec((1, K), lambda i, j: (0, 0)),
            pl.BlockSpec((None, K, tn), lambda i, j: (layer, 0, j)),
        ],
        out_specs=pl.BlockSpec((tm, tn), lambda i, j: (i, j)),
        out_shape=jax.ShapeDtypeStruct((T, n_cols), out_dtype),
        scratch_shapes=[pltpu.VMEM((tm, K), BF16)],
        compiler_params=_params("parallel", "arbitrary"),
        name="norm_matmul",
    )(x, g, w)


def _in_proj_kernel(x_ref, g_ref, w_ref, wfl_ref, cs_ref, axy_ref, qkv_ref, fl_ref, xn_ref, *, n_axy_tiles):
    j = pl.program_id(1)

    @pl.when(j == 0)
    def _():
        xn = _rms(x_ref[...], g_ref[...]).astype(BF16)
        xn_ref[...] = xn
        fl_ref[...] = _dot(xn, wfl_ref[...])

    @pl.when(j < n_axy_tiles)
    def _():
        axy_ref[...] = _dot(xn_ref[...], w_ref[...])

    @pl.when(j >= n_axy_tiles)
    def _():
        qkv_ref[...] = (_dot(xn_ref[...], w_ref[...]) * cs_ref[...]).astype(BF16)


def _in_proj(x, g, w, w_fl, col_scale, layer, n_axy, n_qkv, *, tm=1024, tn=512):
    T, K = x.shape
    na, nq = n_axy // tn, n_qkv // tn
    return pl.pallas_call(
        functools.partial(_in_proj_kernel, n_axy_tiles=na),
        grid=(T // tm, na + nq),
        in_specs=[
            pl.BlockSpec((tm, K), lambda i, j: (i, 0)),
            pl.BlockSpec((1, K), lambda i, j: (0, 0)),
            pl.BlockSpec((None, K, tn), lambda i, j: (layer, 0, j)),
            pl.BlockSpec((None, K, LANES), lambda i, j: (layer, 0, 0)),
            pl.BlockSpec((1, tn), lambda i, j: (0, jnp.maximum(j - na, 0))),
        ],
        out_specs=[
            pl.BlockSpec((tm, tn), lambda i, j: (i, jnp.minimum(j, na - 1))),
            pl.BlockSpec((tm, tn), lambda i, j: (i, jnp.maximum(j - na, 0))),
            pl.BlockSpec((tm, LANES), lambda i, j: (i, 0)),
        ],
        out_shape=[
            jax.ShapeDtypeStruct((T, n_axy), F32),
            jax.ShapeDtypeStruct((T, n_qkv), BF16),
            jax.ShapeDtypeStruct((T, LANES), F32),
        ],
        scratch_shapes=[pltpu.VMEM((tm, K), BF16)],
        compiler_params=_params("parallel", "arbitrary"),
        name="in_proj",
    )(x, g, w, w_fl, col_scale)


def _softplus(x):
    return jnp.maximum(x, 0.0) + jnp.log1p(jnp.exp(-jnp.abs(x)))


def _rglru_kernel(ax_ref, ay_ref, cw_ref, cb_ref, gw_ref, gb_ref, lam_ref, o_ref,
                  xc_ref, tail_ref, hc_ref, *, ts):
    @pl.when(pl.program_id(1) == 0)
    def _():
        tail_ref[...] = jnp.zeros_like(tail_ref)
        hc_ref[...] = jnp.zeros_like(hc_ref)

    x = ax_ref[...]
    cw = cw_ref[...]
    cb = cb_ref[...]
    xc = cb + cw[CONV_W - 1:CONV_W] * x
    for d in range(1, CONV_W):
        xc = xc + cw[CONV_W - 1 - d:CONV_W - d] * pltpu.roll(x, d, axis=0)
    xc_ref[...] = xc
    x8 = x[0:SUBLANES]
    t8 = tail_ref[...]
    row8 = lax.broadcasted_iota(jnp.int32, x8.shape, 0)
    xc8 = cb + cw[CONV_W - 1:CONV_W] * x8
    for d in range(1, CONV_W):
        sh = jnp.where(row8 < d, pltpu.roll(t8, d, axis=0), pltpu.roll(x8, d, axis=0))
        xc8 = xc8 + cw[CONV_W - 1 - d:CONV_W - d] * sh
    xc_ref[0:SUBLANES, :] = xc8
    tail_ref[...] = x[ts - SUBLANES:ts]

    sp = _softplus(-lam_ref[...])
    row = lax.broadcasted_iota(jnp.int32, (ts // SUBLANES, SUBLANES, LRU_BLOCK_W), 1)
    for n in range(x.shape[1] // LRU_BLOCK_W):
        sl = slice(n * LRU_BLOCK_W, (n + 1) * LRU_BLOCK_W)
        xcn = xc_ref[:, sl]
        xb = xcn.astype(BF16)
        r = jax.nn.sigmoid(_dot(xb, gw_ref[0, n]) + gb_ref[0:1, sl])
        i = jax.nn.sigmoid(_dot(xb, gw_ref[1, n]) + gb_ref[1:2, sl])
        log_a = -RGLRU_C * r * sp[:, sl]
        a = jnp.exp(log_a)
        b = jnp.sqrt(-jnp.tanh(log_a) * (a * a + 1.0)) * (i * xcn)
        groups = ts // SUBLANES
        a = a.reshape(groups, SUBLANES, LRU_BLOCK_W)
        b = b.reshape(groups, SUBLANES, LRU_BLOCK_W)
        d = 1
        while d < SUBLANES:
            a_sh = jnp.where(row < d, 1.0, pltpu.roll(a, d, axis=1))
            b_sh = jnp.where(row < d, 0.0, pltpu.roll(b, d, axis=1))
            b = a * b_sh + b
            a = a * a_sh
            d *= 2
        carry = hc_ref[:, sl]
        hs = []
        for gi in range(groups):
            hs.append(b[gi] + a[gi] * carry)
            carry = hs[-1][SUBLANES - 1:SUBLANES]
        hc_ref[:, sl] = carry
        o_ref[:, sl] = (jnp.concatenate(hs, axis=0) * jax.nn.gelu(ay_ref[:, sl])).astype(o_ref.dtype)


def _rglru(axy, conv_w, conv_b, gate_w, gate_b, lam, batch, *, ts=256):
    T, W2 = axy.shape
    W = W2 // 2
    ns = T // batch // ts
    return pl.pallas_call(
        functools.partial(_rglru_kernel, ts=ts),
        grid=(batch, ns),
        in_specs=[
            pl.BlockSpec((ts, W), lambda b, s: (b * ns + s, 0)),
            pl.BlockSpec((ts, W), lambda b, s: (b * ns + s, 1)),
            pl.BlockSpec((CONV_W, W), lambda b, s: (0, 0)),
            pl.BlockSpec((1, W), lambda b, s: (0, 0)),
            pl.BlockSpec(gate_w.shape, lambda b, s: (0, 0, 0, 0)),
            pl.BlockSpec((2, W), lambda b, s: (0, 0)),
            pl.BlockSpec((1, W), lambda b, s: (0, 0)),
        ],
        out_specs=pl.BlockSpec((ts, W), lambda b, s: (b * ns + s, 0)),
        out_shape=jax.ShapeDtypeStruct((T, W), BF16),
        scratch_shapes=[pltpu.VMEM((ts, W), F32), pltpu.VMEM((SUBLANES, W), F32), pltpu.VMEM((1, W), F32)],
        compiler_params=_params("parallel", "arbitrary"),
        name="rglru",
    )(axy, axy, conv_w, conv_b, gate_w, gate_b, lam)


def _fox_prep_kernel(fl_ref, fb_ref, colb_ref, row_ref):
    x = fl_ref[...] + fb_ref[...]
    c = jnp.minimum(x, 0.0) - jnp.log1p(jnp.exp(-jnp.abs(x)))
    S = c.shape[0]
    row = lax.broadcasted_iota(jnp.int32, c.shape, 0)
    d = 1
    while d < S:
        c = c + jnp.where(row < d, 0.0, pltpu.roll(c, d, axis=0))
        d *= 2
    c = c * LOG2E
    for h in range(FOX_HEADS):
        colb_ref[h] = jnp.broadcast_to(c[:, h:h + 1], c.shape)
    for k in range(S // LANES):
        row_ref[:, k * LANES:(k + 1) * LANES] = c[k * LANES:(k + 1) * LANES, :].T[0:SUBLANES]


def _fox_prep(fl, f_bias, batch):
    T = fl.shape[0]
    S = T // batch
    return pl.pallas_call(
        _fox_prep_kernel,
        grid=(batch,),
        in_specs=[pl.BlockSpec((S, LANES), lambda b: (b, 0)), pl.BlockSpec((1, LANES), lambda b: (0, 0))],
        out_specs=[pl.BlockSpec((None, FOX_HEADS, S, LANES), lambda b: (b, 0, 0, 0)),
                   pl.BlockSpec((None, SUBLANES, S), lambda b: (b, 0, 0))],
        out_shape=[jax.ShapeDtypeStruct((batch, FOX_HEADS, S, LANES), F32),
                   jax.ShapeDtypeStruct((batch, SUBLANES, S), F32)],
        compiler_params=_params("parallel"),
        name="fox_prep",
    )(fl, f_bias)


def _t5_bucket(rel):
    nb = REL_BUCKETS // 2
    max_exact = nb // 2
    n = jnp.abs(rel)
    large = max_exact + (jnp.log(jnp.maximum(n, max_exact).astype(jnp.float32) / max_exact)
                         / math.log(REL_MAX_DIST / max_exact) * (nb - max_exact)).astype(jnp.int32)
    large = jnp.minimum(large, nb - 1)
    return jnp.where(rel > 0, nb, 0) + jnp.where(n < max_exact, n, large)


def _bias_tile_types(tq, tk):
    return -(-(tk - 1 + REL_MAX_DIST) // tq) + 1


def _bucket_tiles(tq, tk):
    nt = _bias_tile_types(tq, tk)
    t = jnp.arange(nt, dtype=jnp.int32)[:, None, None]
    c = jnp.arange(tk, dtype=jnp.int32)[None, :, None]
    r = jnp.arange(tq, dtype=jnp.int32)[None, None, :] + t * tq
    allowed = (c // CHUNK) <= (r // CHUNK)
    return jnp.where(allowed, _t5_bucket(c - r), -1)


def _bias_table_kernel(rb_ref, bucket_ref, o_ref):
    h = pl.program_id(0)
    bucket = bucket_ref[...]
    out = jnp.full(bucket.shape, NEG_INF, F32)
    for b in range(REL_BUCKETS):
        out = jnp.where(bucket == b, rb_ref[b, h] * LOG2E, out)
    o_ref[...] = out


def _bias_table(rel_bias, tq, tk):
    buckets = _bucket_tiles(tq, tk)
    nt = buckets.shape[0]
    H = rel_bias.shape[1]
    return pl.pallas_call(
        _bias_table_kernel,
        grid=(H, nt),
        in_specs=[
            pl.BlockSpec(memory_space=pltpu.SMEM),
            pl.BlockSpec((None, tk, tq), lambda h, t: (t, 0, 0)),
        ],
        out_specs=pl.BlockSpec((None, None, tk, tq), lambda h, t: (h, t, 0, 0)),
        out_shape=jax.ShapeDtypeStruct((H, nt, tk, tq), F32),
        compiler_params=_params("parallel", "parallel"),
        name="bias_table",
    )(rel_bias, buckets)


ONES_ROWS = 16
ATTN_HEADS_PER_STEP = 2


def _flash_loop(n_steps, first_logits, next_logits, vt1, sa_ref, sb_ref, m_ref, acc_ref):
    heads = range(m_ref.shape[0])
    dv = acc_ref.shape[1] - ONES_ROWS

    def fill(s_ref, j):
        for h in heads:
            s_ref[h] = next_logits(h, j)

    def update(s_ref, j):
        for h in heads:
            s, m = s_ref[h], m_ref[h]
            m_new = jnp.maximum(m, jnp.max(s, axis=0, keepdims=True))
            p = jnp.exp2(s - m_new).astype(BF16)
            acc_ref[h] = jnp.exp2(m - m_new) * acc_ref[h] + _dot(vt1(h, j), p)
            m_ref[h] = m_new

    m_ref[...] = jnp.full(m_ref.shape, NEG_INF, F32)
    acc_ref[...] = jnp.zeros(acc_ref.shape, F32)
    for h in heads:
        sa_ref[h] = first_logits(h)

    def pair(i, _):
        fill(sb_ref, 2 * i + 1)
        update(sa_ref, 2 * i)
        fill(sa_ref, 2 * i + 2)
        update(sb_ref, 2 * i + 1)
        return 0

    n_pairs = (n_steps - 1) // 2
    lax.fori_loop(0, n_pairs, pair, 0)
    j = 2 * n_pairs

    @pl.when(j + 2 == n_steps)
    def _():
        fill(sb_ref, j + 1)
        update(sa_ref, j)
        update(sb_ref, j + 1)

    @pl.when(j + 1 == n_steps)
    def _():
        update(sa_ref, j)

    return [acc_ref[h, :dv, :] / acc_ref[h, dv:dv + 1, :] for h in heads]


def _flash_scratch(n_heads, tk, n, dv):
    return [pltpu.VMEM((n_heads, tk, n), F32), pltpu.VMEM((n_heads, tk, n), F32),
            pltpu.VMEM((n_heads, 1, n), F32), pltpu.VMEM((n_heads, dv + ONES_ROWS, n), F32)]


def _store_transposed(vt_ref, v_ref):
    n_heads, n_tiles, rows, tk = vt_ref.shape
    dv = rows - ONES_ROWS
    eye = (lax.broadcasted_iota(jnp.int32, (dv, dv), 0) == lax.broadcasted_iota(jnp.int32, (dv, dv), 1)).astype(BF16)
    for h in range(n_heads):
        for j in range(n_tiles):
            vt_ref[h, j, :dv, :] = _dot_nt(eye, v_ref[j * tk:(j + 1) * tk, h * dv:(h + 1) * dv]).astype(BF16)
            vt_ref[h, j, dv:, :] = jnp.ones((ONES_ROWS, tk), BF16)


def _diff_attn_kernel(q_ref, k_ref, v_ref, bias_ref, dl_ref, sg_ref, o_ref, vt_ref, sa_ref, sb_ref, m_ref, acc_ref,
                      *, tq, tk, nt, lam_init):
    @pl.when(pl.program_id(2) == 0)
    def _():
        _store_transposed(vt_ref, v_ref)

    nh = ATTN_HEADS_PER_STEP
    q0 = pl.program_id(2) * tq
    lane = lax.broadcasted_iota(jnp.int32, (tq, LANES), 1)
    qq = []
    for h in range(nh):
        q = q_ref[:, h * LANES:(h + 1) * LANES]
        qq.append(jnp.concatenate([jnp.where(lane < DIFF_DH, q, 0), jnp.where(lane >= DIFF_DH, q, 0)], axis=0))

    def logits(h, j):
        k0 = j * tk if isinstance(j, int) else pl.multiple_of(j * tk, tk)
        s = _dot_nt(k_ref[pl.ds(k0, tk), h * LANES:(h + 1) * LANES], qq[h])
        bias = bias_ref[h, jnp.minimum((q0 - k0) // tq, nt - 1)]
        return s + jnp.concatenate([bias, bias], axis=1)

    outs = _flash_loop(q0 // tk + 1, functools.partial(logits, j=0), logits, lambda h, j: vt_ref[h, j],
                       sa_ref, sb_ref, m_ref, acc_ref)
    dl = dl_ref[...]
    lam = (jnp.exp(jnp.sum(dl[0:1] * dl[1:2], axis=-1, keepdims=True))
           - jnp.exp(jnp.sum(dl[2:3] * dl[3:4], axis=-1, keepdims=True)) + lam_init)
    for h in range(nh):
        o = outs[h]
        d = o[:, :tq] - lam * o[:, tq:]
        y = d * lax.rsqrt(jnp.mean(d * d, axis=0, keepdims=True) + NORM_EPS) * sg_ref[...]
        o_ref[:, h * DIFF_VD:(h + 1) * DIFF_VD] = (y * (1.0 - lam_init)).T.astype(o_ref.dtype)


def _diff_attn(qkv, bias_table, diff_lambda, subln_g, batch, lam_init, *, tq, tk):
    T = qkv.shape[0]
    S = T // batch
    nq = S // tq
    nt = bias_table.shape[1]
    nh = ATTN_HEADS_PER_STEP
    G = DIFF_HEADS // nh
    W = nh * LANES
    return pl.pallas_call(
        functools.partial(_diff_attn_kernel, tq=tq, tk=tk, nt=nt, lam_init=lam_init),
        grid=(batch, G, nq),
        in_specs=[
            pl.BlockSpec((tq, W), lambda b, g, i: (b * nq + i, g)),
            pl.BlockSpec((S, W), lambda b, g, i: (b, G + g)),
            pl.BlockSpec((S, W), lambda b, g, i: (b, 2 * G + g)),
            pl.BlockSpec((nh, nt, tk, tq), lambda b, g, i: (g, 0, 0, 0)),
            pl.BlockSpec(diff_lambda.shape, lambda b, g, i: (0, 0)),
            pl.BlockSpec((DIFF_VD, 1), lambda b, g, i: (0, 0)),
        ],
        out_specs=pl.BlockSpec((tq, nh * DIFF_VD), lambda b, g, i: (b * nq + i, g)),
        out_shape=jax.ShapeDtypeStruct((T, DIFF_HEADS * DIFF_VD), BF16),
        scratch_shapes=([pltpu.VMEM((nh, S // tk, DIFF_VD + ONES_ROWS, tk), BF16)]
                        + _flash_scratch(nh, tk, 2 * tq, DIFF_VD)),
        compiler_params=_params("parallel", "parallel", "arbitrary"),
        name="diff_attn",
    )(qkv, qkv, qkv, bias_table, diff_lambda, subln_g)


def _fox_attn_kernel(q_ref, k_ref, v_ref, ck_ref, cq_ref, o_ref, vt_ref, sa_ref, sb_ref, m_ref, acc_ref, *, tq, tk):
    @pl.when(pl.program_id(2) == 0)
    def _():
        _store_transposed(vt_ref, v_ref)

    nh = ATTN_HEADS_PER_STEP
    q0 = pl.program_id(2) * tq
    h0 = pl.program_id(1) * nh
    qs = [q_ref[:, h * LANES:(h + 1) * LANES] for h in range(nh)]
    cqs = [cq_ref[pl.ds(h0 + h, 1), :] for h in range(nh)]

    n_full = q0 // tk

    def logits(h, tile, masked):
        k0 = pl.multiple_of(tile * tk, tk)
        ck = ck_ref[h, pl.ds(k0, tk), :]
        s = _dot_nt(k_ref[pl.ds(k0, tk), h * LANES:(h + 1) * LANES], qs[h])
        s = s + (cqs[h] - jnp.concatenate([ck] * (tq // LANES), axis=1))
        if masked:
            kpos = lax.broadcasted_iota(jnp.int32, s.shape, 0) + k0
            qpos = lax.broadcasted_iota(jnp.int32, s.shape, 1) + q0
            s = jnp.where(kpos <= qpos, s, NEG_INF)
        return s

    tile_of = lambda j: jnp.where(j == 0, n_full, j - 1)
    outs = _flash_loop(n_full + 1, lambda h: logits(h, n_full, True), lambda h, j: logits(h, j - 1, False),
                       lambda h, j: vt_ref[h, tile_of(j)], sa_ref, sb_ref, m_ref, acc_ref)
    for h in range(nh):
        o_ref[:, h * FOX_DH:(h + 1) * FOX_DH] = outs[h].T.astype(o_ref.dtype)


def _fox_attn(qkv, cum_colb, cum_row, batch, *, tq, tk):
    T = qkv.shape[0]
    S = T // batch
    nq = S // tq
    nh = ATTN_HEADS_PER_STEP
    G = FOX_HEADS // nh
    W = nh * LANES
    base = 3 * DIFF_HEADS // nh
    return pl.pallas_call(
        functools.partial(_fox_attn_kernel, tq=tq, tk=tk),
        grid=(batch, G, nq),
        in_specs=[
            pl.BlockSpec((tq, W), lambda b, g, i: (b * nq + i, base + g)),
            pl.BlockSpec((S, W), lambda b, g, i: (b, base + G + g)),
            pl.BlockSpec((S, W), lambda b, g, i: (b, base + 2 * G + g)),
            pl.BlockSpec((None, nh, S, LANES), lambda b, g, i: (b, g, 0, 0)),
            pl.BlockSpec((None, SUBLANES, tq), lambda b, g, i: (b, 0, i)),
        ],
        out_specs=pl.BlockSpec((tq, nh * FOX_DH), lambda b, g, i: (b * nq + i, g)),
        out_shape=jax.ShapeDtypeStruct((T, FOX_HEADS * FOX_DH), BF16),
        scratch_shapes=([pltpu.VMEM((nh, S // tk, FOX_DH + ONES_ROWS, tk), BF16)]
                        + _flash_scratch(nh, tk, tq, FOX_DH)),
        compiler_params=_params("parallel", "parallel", "arbitrary"),
        name="fox_attn",
    )(qkv, qkv, qkv, cum_colb, cum_row)


def _xattn_kernel(h_ref, gq_ref, go_ref, wq_ref, k_ref, v_ref, wo_ref, o_ref, *, n_heads):
    x = h_ref[...]
    dh = x.shape[1] // n_heads
    xn = _rms(x, gq_ref[...]).astype(BF16)
    q = (_dot(xn, wq_ref[...]) * (LOG2E * dh ** -0.5)).astype(BF16)
    outs = []
    for hd in range(n_heads):
        sl = slice(hd * dh, (hd + 1) * dh)
        s = _dot_nt(q[:, sl], k_ref[:, sl])
        p = jnp.exp2(s - jnp.max(s, axis=-1, keepdims=True))
        l = jnp.sum(p, axis=-1, keepdims=True)
        outs.append((_dot(p.astype(BF16), v_ref[:, sl]) / l).astype(BF16))
    y = _dot(jnp.concatenate(outs, axis=1), wo_ref[...])
    o_ref[...] = x + _rms(y, go_ref[...])


def _xattn(h, g_q, g_o, w_q, kv, w_o, layer, batch, *, tm=256):
    T, D = h.shape
    M = kv.shape[0] // batch
    ns = T // batch // tm
    resident = pl.Buffered(1)
    return pl.pallas_call(
        functools.partial(_xattn_kernel, n_heads=XATTN_HEADS),
        grid=(batch, ns),
        in_specs=[
            pl.BlockSpec((tm, D), lambda b, i: (b * ns + i, 0)),
            pl.BlockSpec((1, D), lambda b, i: (0, 0)),
            pl.BlockSpec((1, D), lambda b, i: (0, 0)),
            pl.BlockSpec((None, D, D), lambda b, i: (layer, 0, 0), pipeline_mode=resident),
            pl.BlockSpec((M, D), lambda b, i: (b, 0)),
            pl.BlockSpec((M, D), lambda b, i: (b, 1)),
            pl.BlockSpec((None, D, D), lambda b, i: (layer, 0, 0), pipeline_mode=resident),
        ],
        out_specs=pl.BlockSpec((tm, D), lambda b, i: (b * ns + i, 0)),
        out_shape=jax.ShapeDtypeStruct((T, D), F32),
        compiler_params=_params("parallel", "arbitrary"),
        name="xattn",
    )(h, g_q, g_o, w_q, kv, kv, w_o)


def _proj_norm_res_kernel(*refs, n_in):
    x_refs, w_refs = refs[:n_in], refs[n_in:2 * n_in]
    g_ref, h_ref, o_ref = refs[2 * n_in:]
    y = _dot(x_refs[0][...], w_refs[0][...])
    for x_ref, w_ref in zip(x_refs[1:], w_refs[1:]):
        y = y + _dot(x_ref[...], w_ref[...])
    o_ref[...] = h_ref[...] + _rms(y, g_ref[...])


def _proj_norm_res(xs, w, layer, g, h, *, tm=512):
    T, N = h.shape
    n_in = len(xs)
    x_specs, w_specs, row0 = [], [], 0
    for x in xs:
        k = x.shape[1]
        x_specs.append(pl.BlockSpec((tm, k), lambda i: (i, 0)))
        w_specs.append(pl.BlockSpec((None, k, N), functools.partial(lambda i, r: (layer, r, 0), r=row0 // k)))
        row0 += k
    return pl.pallas_call(
        functools.partial(_proj_norm_res_kernel, n_in=n_in),
        grid=(T // tm,),
        in_specs=x_specs + w_specs + [pl.BlockSpec((1, N), lambda i: (0, 0)), pl.BlockSpec((tm, N), lambda i: (i, 0))],
        out_specs=pl.BlockSpec((tm, N), lambda i: (i, 0)),
        out_shape=jax.ShapeDtypeStruct((T, N), F32),
        compiler_params=_params("parallel"),
        name="proj_norm_res",
    )(*xs, *([w] * n_in), g, h)


def kernel(x, mem, norm_g, ffn1_w_in, ffn1_w_out, w_in, conv_w, conv_b, lru_gate_w, lru_gate_b, lru_lambda,
           diff_lambda, diff_subln_g, fox_f_bias, rel_bias, w_out, xattn_w_q, xattn_w_kv, xattn_w_o,
           ffn2_w_in, ffn2_w_out):
    B, S, D = x.shape
    depth = norm_g.shape[0]
    T = B * S
    lru_w = conv_w.shape[-1]
    n_axy = 2 * lru_w
    n_qkv = 3 * DIFF_HEADS * DIFF_VD + 3 * FOX_HEADS * FOX_DH
    diff_tq, diff_tk = 512, 512
    fox_tq, fox_tk = 512, 512

    h = x.astype(F32).reshape(T, D)
    mem2 = mem.astype(F32).reshape(B * mem.shape[1], D)
    ones_g = jnp.ones((1, D), F32)

    bf = lambda a: a.astype(BF16)
    ffn1_w_in, ffn1_w_out, ffn2_w_in, ffn2_w_out = bf(ffn1_w_in), bf(ffn1_w_out), bf(ffn2_w_in), bf(ffn2_w_out)
    w_in_b, w_out_b, gate_w_b = bf(w_in), bf(w_out), bf(lru_gate_w)
    w_q_b, w_kv_b, w_o_b = bf(xattn_w_q), bf(xattn_w_kv), bf(xattn_w_o)
    w_fl = bf(jnp.pad(w_in[:, :, n_axy + n_qkv:], ((0, 0), (0, 0), (0, LANES - FOX_HEADS))))
    f_bias = jnp.pad(fox_f_bias, ((0, 0), (0, LANES - FOX_HEADS)))
    n_dq, n_fq = DIFF_HEADS * 2 * DIFF_DH, FOX_HEADS * FOX_DH
    fq0 = 3 * DIFF_HEADS * DIFF_VD
    col_scale = jnp.ones((1, n_qkv), F32)
    col_scale = col_scale.at[:, :n_dq].set(LOG2E * DIFF_DH ** -0.5).at[:, fq0:fq0 + n_fq].set(LOG2E * FOX_DH ** -0.5)

    bias_table = _bias_table(rel_bias, diff_tq, diff_tk)

    for l in range(depth):
        g = norm_g[l]
        lam_init = 0.8 - 0.6 * math.exp(-0.3 * l)
        h = _ffn(h, g[0:1], g[1:2], ffn1_w_in, ffn1_w_out, l)

        axy, qkv, fl = _in_proj(h, g[2:3], w_in_b, w_fl, col_scale, l, n_axy, n_qkv)
        a_out = _rglru(axy, conv_w[l], conv_b[l][None], gate_w_b[l], lru_gate_b[l], lru_lambda[l][None], B)
        cum_colb, cum_row = _fox_prep(fl, f_bias[l][None], B)
        d_out = _diff_attn(qkv, bias_table, diff_lambda[l], diff_subln_g[l][:, None], B, lam_init,
                           tq=diff_tq, tk=diff_tk)
        f_out = _fox_attn(qkv, cum_colb, cum_row, B, tq=fox_tq, tk=fox_tk)
        h = _proj_norm_res([a_out, d_out, f_out], w_out_b, l, g[3:4], h)

        kv = _norm_matmul(mem2, ones_g, w_kv_b, l, 2 * D, BF16, normalize=False)
        h = _xattn(h, g[4:5], g[5:6], w_q_b, kv, w_o_b, l, B)

        h = _ffn(h, g[6:7], g[7:8], ffn2_w_in, ffn2_w_out, l)
    return h.reshape(B, S, D).astype(x.dtype)
```

```python
import functools
import math

import jax
import jax.numpy as jnp
import numpy as np
from jax import lax
from jax.experimental import pallas as pl
from jax.experimental.pallas import tpu as pltpu

F32 = jnp.float32
BF16 = jnp.bfloat16

CHUNK = 64
CONV_W = 4
LRU_BLOCK_W = 128
RGLRU_C = 8.0
DIFF_HEADS = 4
DIFF_DH = 64
DIFF_VD = 128
FOX_HEADS = 4
FOX_DH = 128
XATTN_HEADS = 4
REL_BUCKETS = 32
REL_MAX_DIST = 128
NORM_EPS = 1e-6
NEG_INF = -1e30
FFN_RESIDUAL_WEIGHT = 0.5
LOG2E = math.log2(math.e)

LANES = 128
SUBLANES = 8
V7X_VMEM_BYTES = 64 * 1024 * 1024
VMEM_LIMIT_BYTES = (V7X_VMEM_BYTES * 3) // 4
FFN_VMEM_LIMIT_BYTES = (V7X_VMEM_BYTES * 7) // 8


def _params(*semantics, vmem_limit_bytes=VMEM_LIMIT_BYTES):
    return pltpu.CompilerParams(dimension_semantics=semantics, vmem_limit_bytes=vmem_limit_bytes)


def _rms(x, g):
    return x * lax.rsqrt(jnp.mean(x * x, axis=-1, keepdims=True) + NORM_EPS) * g


def _dot(a, b):
    return jnp.dot(a, b, preferred_element_type=F32)


def _dot_nt(a, b):
    return lax.dot_general(a, b, (((1,), (1,)), ((), ())), preferred_element_type=F32)


def _ffn_kernel(x_ref, gpre_ref, gpost_ref, wg_ref, wu_ref, wo_ref, o_ref, xn_ref):
    j = pl.program_id(1)

    @pl.when(j == 0)
    def _():
        xn_ref[...] = _rms(x_ref[...], gpre_ref[...]).astype(BF16)
        o_ref[...] = jnp.zeros_like(o_ref)

    xn = xn_ref[...]
    gate = _dot(xn, wg_ref[...])
    up = _dot(xn, wu_ref[...])
    act = (gate * jax.nn.sigmoid(gate) * up).astype(BF16)
    o_ref[...] += _dot(act, wo_ref[...])

    @pl.when(j == pl.num_programs(1) - 1)
    def _():
        o_ref[...] = x_ref[...] + FFN_RESIDUAL_WEIGHT * _rms(o_ref[...], gpost_ref[...])


def _ffn(h, g_pre, g_post, w_in, w_out, layer, *, tm=1024, tf=256):
    T, D = h.shape
    F = w_out.shape[1]
    nf = F // tf
    return pl.pallas_call(
        _ffn_kernel,
        grid=(T // tm, nf),
        in_specs=[
            pl.BlockSpec((tm, D), lambda i, j: (i, 0)),
            pl.BlockSpec((1, D), lambda i, j: (0, 0)),
            pl.BlockSpec((1, D), lambda i, j: (0, 0)),
            pl.BlockSpec((None, D, tf), lambda i, j: (layer, 0, j)),
            pl.BlockSpec((None, D, tf), lambda i, j: (layer, 0, j + nf)),
            pl.BlockSpec((None, tf, D), lambda i, j: (layer, j, 0)),
        ],
        out_specs=pl.BlockSpec((tm, D), lambda i, j: (i, 0)),
        out_shape=jax.ShapeDtypeStruct((T, D), F32),
        scratch_shapes=[pltpu.VMEM((tm, D), BF16)],
        compiler_params=_params("parallel", "arbitrary", vmem_limit_bytes=FFN_VMEM_LIMIT_BYTES),
        name="ffn",
    )(h, g_pre, g_post, w_in, w_in, w_out)


def _norm_matmul_kernel(x_ref, g_ref, w_ref, o_ref, xn_ref, *, normalize):
    @pl.when(pl.program_id(1) == 0)
    def _():
        x = x_ref[...]
        if normalize:
            x = _rms(x, g_ref[...])
        xn_ref[...] = x.astype(BF16)

    o_ref[...] = _dot(xn_ref[...], w_ref[...]).astype(o_ref.dtype)


def _norm_matmul(x, g, w, layer, n_cols, out_dtype, *, normalize=True, tm=512, tn=1024):
    T, K = x.shape
    tn = min(tn, n_cols)
    return pl.pallas_call(
        functools.partial(_norm_matmul_kernel, normalize=normalize),
        grid=(T // tm, n_cols // tn),
        in_specs=[
            pl.BlockSpec((tm, K), lambda i, j: (i, 0)),
            pl.BlockSpec((1, K), lambda i, j: (0, 0)),
            pl.BlockSpec((None, K, tn), lambda i, j: (layer, 0, j)),
        ],
        out_specs=pl.BlockSpec((tm, tn), lambda i, j: (i, j)),
        out_shape=jax.ShapeDtypeStruct((T, n_cols), out_dtype),
        scratch_shapes=[pltpu.VMEM((tm, K), BF16)],
        compiler_params=_params("parallel", "arbitrary"),
        name="norm_matmul",
    )(x, g, w)


def _in_proj_kernel(x_ref, g_ref, w_ref, wfl_ref, cs_ref, axy_ref, qkv_ref, fl_ref, xn_ref, *, n_axy_tiles):
    j = pl.program_id(1)

    @pl.when(j == 0)
    def _():
        xn = _rms(x_ref[...], g_ref[...]).astype(BF16)
        xn_ref[...] = xn
        fl_ref[...] = _dot(xn, wfl_ref[...])

    @pl.when(j < n_axy_tiles)
    def _():
        axy_ref[...] = _dot(xn_ref[...], w_ref[...])

    @pl.when(j >= n_axy_tiles)
    def _():
        qkv_ref[...] = (_dot(xn_ref[...], w_ref[...]) * cs_ref[...]).astype(BF16)


def _in_proj(x, g, w, w_fl, col_scale, layer, n_axy, n_qkv, *, tm=1024, tn=1024):
    T, K = x.shape
    na, nq = n_axy // tn, n_qkv // tn
    return pl.pallas_call(
        functools.partial(_in_proj_kernel, n_axy_tiles=na),
        grid=(T // tm, na + nq),
        in_specs=[
            pl.BlockSpec((tm, K), lambda i, j: (i, 0)),
            pl.BlockSpec((1, K), lambda i, j: (0, 0)),
            pl.BlockSpec((None, K, tn), lambda i, j: (layer, 0, j)),
            pl.BlockSpec((None, K, LANES), lambda i, j: (layer, 0, 0)),
            pl.BlockSpec((1, tn), lambda i, j: (0, jnp.maximum(j - na, 0))),
        ],
        out_specs=[
            pl.BlockSpec((tm, tn), lambda i, j: (i, jnp.minimum(j, na - 1))),
            pl.BlockSpec((tm, tn), lambda i, j: (i, jnp.maximum(j - na, 0))),
            pl.BlockSpec((tm, LANES), lambda i, j: (i, 0)),
        ],
        out_shape=[
            jax.ShapeDtypeStruct((T, n_axy), F32),
            jax.ShapeDtypeStruct((T, n_qkv), BF16),
            jax.ShapeDtypeStruct((T, LANES), F32),
        ],
        scratch_shapes=[pltpu.VMEM((tm, K), BF16)],
        compiler_params=_params("parallel", "arbitrary", vmem_limit_bytes=FFN_VMEM_LIMIT_BYTES),
        name="in_proj",
    )(x, g, w, w_fl, col_scale)


def _softplus(x):
    return jnp.maximum(x, 0.0) + jnp.log1p(jnp.exp(-jnp.abs(x)))


def _rglru_kernel(ax_ref, ay_ref, cw_ref, cb_ref, gw_ref, gb_ref, lam_ref, o_ref,
                  xc_ref, tail_ref, hc_ref, *, ts):
    @pl.when(pl.program_id(1) == 0)
    def _():
        tail_ref[...] = jnp.zeros_like(tail_ref)
        hc_ref[...] = jnp.zeros_like(hc_ref)

    x = ax_ref[...]
    cw = cw_ref[...]
    cb = cb_ref[...]
    xc = cb + cw[CONV_W - 1:CONV_W] * x
    for d in range(1, CONV_W):
        xc = xc + cw[CONV_W - 1 - d:CONV_W - d] * pltpu.roll(x, d, axis=0)
    xc_ref[...] = xc
    x8 = x[0:SUBLANES]
    t8 = tail_ref[...]
    row8 = lax.broadcasted_iota(jnp.int32, x8.shape, 0)
    xc8 = cb + cw[CONV_W - 1:CONV_W] * x8
    for d in range(1, CONV_W):
        sh = jnp.where(row8 < d, pltpu.roll(t8, d, axis=0), pltpu.roll(x8, d, axis=0))
        xc8 = xc8 + cw[CONV_W - 1 - d:CONV_W - d] * sh
    xc_ref[0:SUBLANES, :] = xc8
    tail_ref[...] = x[ts - SUBLANES:ts]

    sp = _softplus(-lam_ref[...])
    row = lax.broadcasted_iota(jnp.int32, (ts // SUBLANES, SUBLANES, LRU_BLOCK_W), 1)
    for n in range(x.shape[1] // LRU_BLOCK_W):
        sl = slice(n * LRU_BLOCK_W, (n + 1) * LRU_BLOCK_W)
        xcn = xc_ref[:, sl]
        xb = xcn.astype(BF16)
        r = jax.nn.sigmoid(_dot(xb, gw_ref[0, n]) + gb_ref[0:1, sl])
        i = jax.nn.sigmoid(_dot(xb, gw_ref[1, n]) + gb_ref[1:2, sl])
        log_a = -RGLRU_C * r * sp[:, sl]
        a = jnp.exp(log_a)
        b = jnp.sqrt(-jnp.tanh(log_a) * (a * a + 1.0)) * (i * xcn)
        groups = ts // SUBLANES
        a = a.reshape(groups, SUBLANES, LRU_BLOCK_W)
        b = b.reshape(groups, SUBLANES, LRU_BLOCK_W)
        d = 1
        while d < SUBLANES:
            a_sh = jnp.where(row < d, 1.0, pltpu.roll(a, d, axis=1))
            b_sh = jnp.where(row < d, 0.0, pltpu.roll(b, d, axis=1))
            b = a * b_sh + b
            a = a * a_sh
            d *= 2
        carry = hc_ref[:, sl]
        hs = []
        for gi in range(groups):
            hs.append(b[gi] + a[gi] * carry)
            carry = hs[-1][SUBLANES - 1:SUBLANES]
        hc_ref[:, sl] = carry
        o_ref[:, sl] = (jnp.concatenate(hs, axis=0) * jax.nn.gelu(ay_ref[:, sl])).astype(o_ref.dtype)


def _rglru(axy, conv_w, conv_b, gate_w, gate_b, lam, batch, *, ts=256):
    T, W2 = axy.shape
    W = W2 // 2
    ns = T // batch // ts
    return pl.pallas_call(
        functools.partial(_rglru_kernel, ts=ts),
        grid=(batch, ns),
        in_specs=[
            pl.BlockSpec((ts, W), lambda b, s: (b * ns + s, 0)),
            pl.BlockSpec((ts, W), lambda b, s: (b * ns + s, 1)),
            pl.BlockSpec((CONV_W, W), lambda b, s: (0, 0)),
            pl.BlockSpec((1, W), lambda b, s: (0, 0)),
            pl.BlockSpec(gate_w.shape, lambda b, s: (0, 0, 0, 0)),
            pl.BlockSpec((2, W), lambda b, s: (0, 0)),
            pl.BlockSpec((1, W), lambda b, s: (0, 0)),
        ],
        out_specs=pl.BlockSpec((ts, W), lambda b, s: (b * ns + s, 0)),
        out_shape=jax.ShapeDtypeStruct((T, W), BF16),
        scratch_shapes=[pltpu.VMEM((ts, W), F32), pltpu.VMEM((SUBLANES, W), F32), pltpu.VMEM((1, W), F32)],
        compiler_params=_params("parallel", "arbitrary"),
        name="rglru",
    )(axy, axy, conv_w, conv_b, gate_w, gate_b, lam)


def _fox_prep_kernel(fl_ref, fb_ref, colb_ref, row_ref):
    x = fl_ref[...] + fb_ref[...]
    c = jnp.minimum(x, 0.0) - jnp.log1p(jnp.exp(-jnp.abs(x)))
    S = c.shape[0]
    row = lax.broadcasted_iota(jnp.int32, c.shape, 0)
    d = 1
    while d < S:
        c = c + jnp.where(row < d, 0.0, pltpu.roll(c, d, axis=0))
        d *= 2
    c = c * LOG2E
    for h in range(FOX_HEADS):
        colb_ref[h] = jnp.broadcast_to(c[:, h:h + 1], c.shape)
    for k in range(S // LANES):
        row_ref[:, k * LANES:(k + 1) * LANES] = c[k * LANES:(k + 1) * LANES, :].T[0:SUBLANES]


def _fox_prep(fl, f_bias, batch):
    T = fl.shape[0]
    S = T // batch
    return pl.pallas_call(
        _fox_prep_kernel,
        grid=(batch,),
        in_specs=[pl.BlockSpec((S, LANES), lambda b: (b, 0)), pl.BlockSpec((1, LANES), lambda b: (0, 0))],
        out_specs=[pl.BlockSpec((None, FOX_HEADS, S, LANES), lambda b: (b, 0, 0, 0)),
                   pl.BlockSpec((None, SUBLANES, S), lambda b: (b, 0, 0))],
        out_shape=[jax.ShapeDtypeStruct((batch, FOX_HEADS, S, LANES), F32),
                   jax.ShapeDtypeStruct((batch, SUBLANES, S), F32)],
        compiler_params=_params("parallel"),
        name="fox_prep",
    )(fl, f_bias)


def _t5_bucket(rel):
    nb = REL_BUCKETS // 2
    max_exact = nb // 2
    n = jnp.abs(rel)
    large = max_exact + (jnp.log(jnp.maximum(n, max_exact).astype(jnp.float32) / max_exact)
                         / math.log(REL_MAX_DIST / max_exact) * (nb - max_exact)).astype(jnp.int32)
    large = jnp.minimum(large, nb - 1)
    return jnp.where(rel > 0, nb, 0) + jnp.where(n < max_exact, n, large)


def _bias_tile_types(tq, tk):
    return -(-(tk - 1 + REL_MAX_DIST) // tq) + 1


def _bucket_tiles(tq, tk):
    nt = _bias_tile_types(tq, tk)
    t = jnp.arange(nt, dtype=jnp.int32)[:, None, None]
    c = jnp.arange(tk, dtype=jnp.int32)[None, :, None]
    r = jnp.arange(tq, dtype=jnp.int32)[None, None, :] + t * tq
    allowed = (c // CHUNK) <= (r // CHUNK)
    return jnp.where(allowed, _t5_bucket(c - r), -1)


def _bias_table_kernel(rb_ref, bucket_ref, o_ref):
    h = pl.program_id(0)
    bucket = bucket_ref[...]
    out = jnp.full(bucket.shape, NEG_INF, F32)
    for b in range(REL_BUCKETS):
        out = jnp.where(bucket == b, rb_ref[b, h] * LOG2E, out)
    o_ref[...] = out


def _bias_table(rel_bias, tq, tk):
    buckets = _bucket_tiles(tq, tk)
    nt = buckets.shape[0]
    H = rel_bias.shape[1]
    return pl.pallas_call(
        _bias_table_kernel,
        grid=(H, nt),
        in_specs=[
            pl.BlockSpec(memory_space=pltpu.SMEM),
            pl.BlockSpec((None, tk, tq), lambda h, t: (t, 0, 0)),
        ],
        out_specs=pl.BlockSpec((None, None, tk, tq), lambda h, t: (h, t, 0, 0)),
        out_shape=jax.ShapeDtypeStruct((H, nt, tk, tq), F32),
        compiler_params=_params("parallel", "parallel"),
        name="bias_table",
    )(rel_bias, buckets)


ONES_ROWS = 16
ATTN_HEADS_PER_STEP = 2


def _paired_flash_loop(pair, n_tiles, logits, vt1, sa_ref, sb_ref, m_ref, acc_ref):
    heads = range(m_ref.shape[1])
    n_steps = n_tiles + 1

    def step(w):
        if isinstance(w, int) and w < 2:
            return w, (pair, n_tiles - 1 - pair)[w], True
        sel = (w - 2 >= pair).astype(jnp.int32)
        return sel, w - 2 - sel * pair, False

    def fill(s_ref, w):
        sel, tile, diagonal = step(w)
        for h in heads:
            s_ref[h] = logits(h, sel, tile, diagonal)

    def update(s_ref, w):
        sel, tile, _ = step(w)
        for h in heads:
            s, m = s_ref[h], m_ref[sel, h]
            m_new = jnp.maximum(m, jnp.max(s, axis=0, keepdims=True))
            p = jnp.exp2(s - m_new).astype(BF16)
            acc_ref[sel, h] = jnp.exp2(m - m_new) * acc_ref[sel, h] + _dot(vt1(h, tile), p)
            m_ref[sel, h] = m_new

    m_ref[...] = jnp.full(m_ref.shape, NEG_INF, F32)
    acc_ref[...] = jnp.zeros(acc_ref.shape, F32)
    fill(sa_ref, 0)
    fill(sb_ref, 1)
    update(sa_ref, 0)
    fill(sa_ref, 2)
    update(sb_ref, 1)

    def two_steps(i, _):
        w = 3 + 2 * i
        fill(sb_ref, w)
        update(sa_ref, w - 1)
        fill(sa_ref, w + 1)
        update(sb_ref, w)
        return 0

    lax.fori_loop(0, (n_steps - 3) // 2, two_steps, 0)
    update(sa_ref, n_steps - 1)


def _flash_scratch(n_heads, tk, n, dv):
    return [pltpu.VMEM((n_heads, tk, n), F32), pltpu.VMEM((n_heads, tk, n), F32),
            pltpu.VMEM((2, n_heads, 1, n), F32), pltpu.VMEM((2, n_heads, dv + ONES_ROWS, n), F32)]


def _store_transposed(vt_ref, v_ref):
    n_heads, n_tiles, rows, tk = vt_ref.shape
    dv = rows - ONES_ROWS
    eye = (lax.broadcasted_iota(jnp.int32, (dv, dv), 0) == lax.broadcasted_iota(jnp.int32, (dv, dv), 1)).astype(BF16)
    for h in range(n_heads):
        for j in range(n_tiles):
            vt_ref[h, j, :dv, :] = _dot_nt(eye, v_ref[j * tk:(j + 1) * tk, h * dv:(h + 1) * dv]).astype(BF16)
            vt_ref[h, j, dv:, :] = jnp.ones((ONES_ROWS, tk), BF16)


def _diff_attn_kernel(q_ref, k_ref, v_ref, bias_ref, dl_ref, sg_ref, o_ref, vt_ref, qq_ref, sa_ref, sb_ref, m_ref,
                      acc_ref, *, tq, nt, lam_init):
    pair = pl.program_id(2)
    n_tiles = k_ref.shape[0] // tq

    @pl.when(pair == 0)
    def _():
        _store_transposed(vt_ref, v_ref)

    nh = ATTN_HEADS_PER_STEP
    q0s = (pl.multiple_of(pair * tq, tq), pl.multiple_of((n_tiles - 1 - pair) * tq, tq))
    lane = lax.broadcasted_iota(jnp.int32, (tq, LANES), 1)
    for sel in range(2):
        for h in range(nh):
            q = q_ref[pl.ds(q0s[sel], tq), h * LANES:(h + 1) * LANES]
            qq_ref[sel, h, :tq, :] = jnp.where(lane < DIFF_DH, q, 0)
            qq_ref[sel, h, tq:, :] = jnp.where(lane >= DIFF_DH, q, 0)

    def logits(h, sel, tile, diagonal):
        del diagonal
        k0 = pl.multiple_of(tile * tq, tq)
        q0 = q0s[0] + sel * (q0s[1] - q0s[0])
        s = _dot_nt(k_ref[pl.ds(k0, tq), h * LANES:(h + 1) * LANES], qq_ref[sel, h])
        bias = bias_ref[h, jnp.minimum((q0 - k0) // tq, nt - 1)]
        return s + jnp.concatenate([bias, bias], axis=1)

    _paired_flash_loop(pair, n_tiles, logits, lambda h, tile: vt_ref[h, tile], sa_ref, sb_ref, m_ref, acc_ref)
    dl = dl_ref[...]
    lam = (jnp.exp(jnp.sum(dl[0:1] * dl[1:2], axis=-1, keepdims=True))
           - jnp.exp(jnp.sum(dl[2:3] * dl[3:4], axis=-1, keepdims=True)) + lam_init)
    for sel in range(2):
        for h in range(nh):
            o = acc_ref[sel, h, :DIFF_VD, :] / acc_ref[sel, h, DIFF_VD:DIFF_VD + 1, :]
            d = o[:, :tq] - lam * o[:, tq:]
            y = d * lax.rsqrt(jnp.mean(d * d, axis=0, keepdims=True) + NORM_EPS) * sg_ref[...]
            o_ref[pl.ds(q0s[sel], tq), h * DIFF_VD:(h + 1) * DIFF_VD] = (y * (1.0 - lam_init)).T.astype(o_ref.dtype)


def _diff_attn(qkv, bias_table, diff_lambda, subln_g, batch, lam_init, *, tq):
    T = qkv.shape[0]
    S = T // batch
    nq = S // tq
    nt = bias_table.shape[1]
    nh = ATTN_HEADS_PER_STEP
    G = DIFF_HEADS // nh
    W = nh * LANES
    return pl.pallas_call(
        functools.partial(_diff_attn_kernel, tq=tq, nt=nt, lam_init=lam_init),
        grid=(batch, G, nq // 2),
        in_specs=[
            pl.BlockSpec((S, W), lambda b, g, i: (b, g)),
            pl.BlockSpec((S, W), lambda b, g, i: (b, G + g)),
            pl.BlockSpec((S, W), lambda b, g, i: (b, 2 * G + g)),
            pl.BlockSpec((nh, nt, tq, tq), lambda b, g, i: (g, 0, 0, 0)),
            pl.BlockSpec(diff_lambda.shape, lambda b, g, i: (0, 0)),
            pl.BlockSpec((DIFF_VD, 1), lambda b, g, i: (0, 0)),
        ],
        out_specs=pl.BlockSpec((S, nh * DIFF_VD), lambda b, g, i: (b, g)),
        out_shape=jax.ShapeDtypeStruct((T, DIFF_HEADS * DIFF_VD), BF16),
        scratch_shapes=([pltpu.VMEM((nh, nq, DIFF_VD + ONES_ROWS, tq), BF16), pltpu.VMEM((2, nh, 2 * tq, LANES), BF16)]
                        + _flash_scratch(nh, tq, 2 * tq, DIFF_VD)),
        compiler_params=_params("parallel", "parallel", "arbitrary"),
        name="diff_attn",
    )(qkv, qkv, qkv, bias_table, diff_lambda, subln_g)


def _fox_attn_kernel(q_ref, k_ref, v_ref, ck_ref, cq_ref, o_ref, vt_ref, sa_ref, sb_ref, m_ref, acc_ref, *, tq):
    pair = pl.program_id(2)
    n_tiles = k_ref.shape[0] // tq

    @pl.when(pair == 0)
    def _():
        _store_transposed(vt_ref, v_ref)

    nh = ATTN_HEADS_PER_STEP
    h0 = pl.program_id(1) * nh
    q_tiles = (pair, n_tiles - 1 - pair)
    q0s = tuple(pl.multiple_of(t * tq, tq) for t in q_tiles)
    cqs = [[cq_ref[q_tiles[sel], pl.ds(h0 + h, 1), :] for h in range(nh)] for sel in range(2)]

    def logits(h, sel, tile, diagonal):
        k0 = pl.multiple_of(tile * tq, tq)
        if diagonal:
            q0, cq = q0s[sel], cqs[sel][h]
        else:
            q0 = q0s[0] + sel * (q0s[1] - q0s[0])
            cq = jnp.where(sel == 1, cqs[1][h], cqs[0][h])
        q = q_ref[pl.ds(q0, tq), h * LANES:(h + 1) * LANES]
        ck = ck_ref[h, pl.ds(k0, tq), :]
        s = _dot_nt(k_ref[pl.ds(k0, tq), h * LANES:(h + 1) * LANES], q)
        s = s + (cq - jnp.concatenate([ck] * (tq // LANES), axis=1))
        if diagonal:
            s = jnp.where(lax.broadcasted_iota(jnp.int32, s.shape, 0) <= lax.broadcasted_iota(jnp.int32, s.shape, 1),
                          s, NEG_INF)
        return s

    _paired_flash_loop(pair, n_tiles, logits, lambda h, tile: vt_ref[h, tile], sa_ref, sb_ref, m_ref, acc_ref)
    for sel in range(2):
        for h in range(nh):
            o = acc_ref[sel, h, :FOX_DH, :] / acc_ref[sel, h, FOX_DH:FOX_DH + 1, :]
            o_ref[pl.ds(q0s[sel], tq), h * FOX_DH:(h + 1) * FOX_DH] = o.T.astype(o_ref.dtype)


def _fox_attn(qkv, cum_colb, cum_row, batch, *, tq):
    T = qkv.shape[0]
    S = T // batch
    nq = S // tq
    nh = ATTN_HEADS_PER_STEP
    G = FOX_HEADS // nh
    W = nh * LANES
    base = 3 * DIFF_HEADS // nh
    return pl.pallas_call(
        functools.partial(_fox_attn_kernel, tq=tq),
        grid=(batch, G, nq // 2),
        in_specs=[
            pl.BlockSpec((S, W), lambda b, g, i: (b, base + g)),
            pl.BlockSpec((S, W), lambda b, g, i: (b, base + G + g)),
            pl.BlockSpec((S, W), lambda b, g, i: (b, base + 2 * G + g)),
            pl.BlockSpec((None, nh, S, LANES), lambda b, g, i: (b, g, 0, 0)),
            pl.BlockSpec((None, nq, SUBLANES, tq), lambda b, g, i: (b, 0, 0, 0)),
        ],
        out_specs=pl.BlockSpec((S, nh * FOX_DH), lambda b, g, i: (b, g)),
        out_shape=jax.ShapeDtypeStruct((T, FOX_HEADS * FOX_DH), BF16),
        scratch_shapes=([pltpu.VMEM((nh, nq, FOX_DH + ONES_ROWS, tq), BF16)] + _flash_scratch(nh, tq, tq, FOX_DH)),
        compiler_params=_params("parallel", "parallel", "arbitrary"),
        name="fox_attn",
    )(qkv, qkv, qkv, cum_colb, cum_row)


def _xattn_kernel(h_ref, gq_ref, go_ref, wq_ref, k_ref, v_ref, wo_ref, o_ref, *, n_heads):
    x = h_ref[...]
    dh = x.shape[1] // n_heads
    xn = _rms(x, gq_ref[...]).astype(BF16)
    q = (_dot(xn, wq_ref[...]) * (LOG2E * dh ** -0.5)).astype(BF16)
    outs = []
    for hd in range(n_heads):
        sl = slice(hd * dh, (hd + 1) * dh)
        s = _dot_nt(q[:, sl], k_ref[:, sl])
        p = jnp.exp2(s - jnp.max(s, axis=-1, keepdims=True))
        l = jnp.sum(p, axis=-1, keepdims=True)
        outs.append((_dot(p.astype(BF16), v_ref[:, sl]) / l).astype(BF16))
    y = _dot(jnp.concatenate(outs, axis=1), wo_ref[...])
    o_ref[...] = x + _rms(y, go_ref[...])


def _xattn(h, g_q, g_o, w_q, kv, w_o, layer, batch, *, tm=512):
    T, D = h.shape
    M = kv.shape[0] // batch
    ns = T // batch // tm
    resident = pl.Buffered(1)
    return pl.pallas_call(
        functools.partial(_xattn_kernel, n_heads=XATTN_HEADS),
        grid=(batch, ns),
        in_specs=[
            pl.BlockSpec((tm, D), lambda b, i: (b * ns + i, 0)),
            pl.BlockSpec((1, D), lambda b, i: (0, 0)),
            pl.BlockSpec((1, D), lambda b, i: (0, 0)),
            pl.BlockSpec((None, D, D), lambda b, i: (layer, 0, 0), pipeline_mode=resident),
            pl.BlockSpec((M, D), lambda b, i: (b, 0)),
            pl.BlockSpec((M, D), lambda b, i: (b, 1)),
            pl.BlockSpec((None, D, D), lambda b, i: (layer, 0, 0), pipeline_mode=resident),
        ],
        out_specs=pl.BlockSpec((tm, D), lambda b, i: (b * ns + i, 0)),
        out_shape=jax.ShapeDtypeStruct((T, D), F32),
        compiler_params=_params("parallel", "arbitrary"),
        name="xattn",
    )(h, g_q, g_o, w_q, kv, kv, w_o)


def _proj_norm_res_kernel(*refs, n_in):
    x_refs, w_refs = refs[:n_in], refs[n_in:2 * n_in]
    g_ref, h_ref, o_ref = refs[2 * n_in:]
    y = _dot(x_refs[0][...], w_refs[0][...])
    for x_ref, w_ref in zip(x_refs[1:], w_refs[1:]):
        y = y + _dot(x_ref[...], w_ref[...])
    o_ref[...] = h_ref[...] + _rms(y, g_ref[...])


def _proj_norm_res(xs, w, layer, g, h, *, tm=512):
    T, N = h.shape
    n_in = len(xs)
    x_specs, w_specs, row0 = [], [], 0
    for x in xs:
        k = x.shape[1]
        x_specs.append(pl.BlockSpec((tm, k), lambda i: (i, 0)))
        w_specs.append(pl.BlockSpec((None, k, N), functools.partial(lambda i, r: (layer, r, 0), r=row0 // k)))
        row0 += k
    return pl.pallas_call(
        functools.partial(_proj_norm_res_kernel, n_in=n_in),
        grid=(T // tm,),
        in_specs=x_specs + w_specs + [pl.BlockSpec((1, N), lambda i: (0, 0)), pl.BlockSpec((tm, N), lambda i: (i, 0))],
        out_specs=pl.BlockSpec((tm, N), lambda i: (i, 0)),
        out_shape=jax.ShapeDtypeStruct((T, N), F32),
        compiler_params=_params("parallel"),
        name="proj_norm_res",
    )(*xs, *([w] * n_in), g, h)


def kernel(x, mem, norm_g, ffn1_w_in, ffn1_w_out, w_in, conv_w, conv_b, lru_gate_w, lru_gate_b, lru_lambda,
           diff_lambda, diff_subln_g, fox_f_bias, rel_bias, w_out, xattn_w_q, xattn_w_kv, xattn_w_o,
           ffn2_w_in, ffn2_w_out):
    B, S, D = x.shape
    depth = norm_g.shape[0]
    T = B * S
    lru_w = conv_w.shape[-1]
    n_axy = 2 * lru_w
    n_qkv = 3 * DIFF_HEADS * DIFF_VD + 3 * FOX_HEADS * FOX_DH
    attn_tq = 512

    h = x.astype(F32).reshape(T, D)
    mem2 = mem.astype(F32).reshape(B * mem.shape[1], D)
    ones_g = jnp.ones((1, D), F32)

    bf = lambda a: a.astype(BF16)
    ffn1_w_in, ffn1_w_out, ffn2_w_in, ffn2_w_out = bf(ffn1_w_in), bf(ffn1_w_out), bf(ffn2_w_in), bf(ffn2_w_out)
    w_in_b, w_out_b, gate_w_b = bf(w_in), bf(w_out), bf(lru_gate_w)
    w_q_b, w_kv_b, w_o_b = bf(xattn_w_q), bf(xattn_w_kv), bf(xattn_w_o)
    w_fl = bf(jnp.pad(w_in[:, :, n_axy + n_qkv:], ((0, 0), (0, 0), (0, LANES - FOX_HEADS))))
    f_bias = jnp.pad(fox_f_bias, ((0, 0), (0, LANES - FOX_HEADS)))
    n_dq, n_fq = DIFF_HEADS * 2 * DIFF_DH, FOX_HEADS * FOX_DH
    fq0 = 3 * DIFF_HEADS * DIFF_VD
    col_scale = jnp.ones((1, n_qkv), F32)
    col_scale = col_scale.at[:, :n_dq].set(LOG2E * DIFF_DH ** -0.5).at[:, fq0:fq0 + n_fq].set(LOG2E * FOX_DH ** -0.5)

    bias_table = _bias_table(rel_bias, attn_tq, attn_tq)

    for l in range(depth):
        g = norm_g[l]
        lam_init = 0.8 - 0.6 * math.exp(-0.3 * l)
        h = _ffn(h, g[0:1], g[1:2], ffn1_w_in, ffn1_w_out, l)

        axy, qkv, fl = _in_proj(h, g[2:3], w_in_b, w_fl, col_scale, l, n_axy, n_qkv)
        a_out = _rglru(axy, conv_w[l], conv_b[l][None], gate_w_b[l], lru_gate_b[l], lru_lambda[l][None], B)
        cum_colb, cum_row = _fox_prep(fl, f_bias[l][None], B)
        cum_row = cum_row.reshape(B, SUBLANES, S // attn_tq, attn_tq).transpose(0, 2, 1, 3)
        d_out = _diff_attn(qkv, bias_table, diff_lambda[l], diff_subln_g[l][:, None], B, lam_init, tq=attn_tq)
        f_out = _fox_attn(qkv, cum_colb, cum_row, B, tq=attn_tq)
        h = _proj_norm_res([a_out, d_out, f_out], w_out_b, l, g[3:4], h)

        kv = _norm_matmul(mem2, ones_g, w_kv_b, l, 2 * D, BF16, normalize=False)
        h = _xattn(h, g[4:5], g[5:6], w_q_b, kv, w_o_b, l, B)

        h = _ffn(h, g[6:7], g[7:8], ffn2_w_in, ffn2_w_out, l)
    return h.reshape(B, S, D).astype(x.dtype)
```

```python
import functools
import math

import jax
import jax.numpy as jnp
import numpy as np
from jax import lax
from jax.experimental import pallas as pl
from jax.experimental.pallas import tpu as pltpu

F32 = jnp.float32
BF16 = jnp.bfloat16

CHUNK = 64
CONV_W = 4
LRU_BLOCK_W = 128
RGLRU_C = 8.0
DIFF_HEADS = 4
DIFF_DH = 64
DIFF_VD = 128
FOX_HEADS = 4
FOX_DH = 128
XATTN_HEADS = 4
REL_BUCKETS = 32
REL_MAX_DIST = 128
NORM_EPS = 1e-6
NEG_INF = -1e30
FFN_RESIDUAL_WEIGHT = 0.5
LOG2E = math.log2(math.e)

LANES = 128
SUBLANES = 8
V7X_VMEM_BYTES = 64 * 1024 * 1024
VMEM_LIMIT_BYTES = (V7X_VMEM_BYTES * 3) // 4
FFN_VMEM_LIMIT_BYTES = (V7X_VMEM_BYTES * 7) // 8


def _params(*semantics, vmem_limit_bytes=VMEM_LIMIT_BYTES):
    return pltpu.CompilerParams(dimension_semantics=semantics, vmem_limit_bytes=vmem_limit_bytes)


def _rms(x, g):
    return x * lax.rsqrt(jnp.mean(x * x, axis=-1, keepdims=True) + NORM_EPS) * g


def _dot(a, b):
    return jnp.dot(a, b, preferred_element_type=F32)


def _dot_nt(a, b):
    return lax.dot_general(a, b, (((1,), (1,)), ((), ())), preferred_element_type=F32)


NORM_ROW_CHUNK = 16


def _inv_rms(x):
    inv = lax.rsqrt(jnp.mean(x * x, axis=-1, keepdims=True) + NORM_EPS)
    return jnp.broadcast_to(inv, (x.shape[0], LANES))


def _lane_tiled(v, width):
    return jnp.concatenate([v] * (width // LANES), axis=1)


def _for_row_chunks(n_rows, fn):
    def body(c, _):
        fn(pl.ds(pl.multiple_of(c * NORM_ROW_CHUNK, NORM_ROW_CHUNK), NORM_ROW_CHUNK))
        return 0

    lax.fori_loop(0, n_rows // NORM_ROW_CHUNK, body, 0, unroll=4)


def _ffn_kernel(x_ref, gpre_ref, gpost_ref, wg_ref, wu_ref, wo_ref, o_ref, xn_ref, inv_ref):
    j = pl.program_id(1)
    tm = x_ref.shape[0]

    @pl.when(j == 0)
    def _():
        inv_ref[...] = _inv_rms(x_ref[...])
        g = gpre_ref[...]

        def normalise(rows):
            xn_ref[rows, :] = (x_ref[rows, :] * _lane_tiled(inv_ref[rows, :], g.shape[1]) * g).astype(BF16)

        _for_row_chunks(tm, normalise)

    def hidden_step():
        xn = xn_ref[...]
        gate = _dot(xn, wg_ref[...])
        up = _dot(xn, wu_ref[...])
        act = (gate * jax.nn.sigmoid(gate) * up).astype(BF16)
        return _dot(act, wo_ref[...])

    @pl.when(j == 0)
    def _():
        o_ref[...] = hidden_step()

    @pl.when(j > 0)
    def _():
        o_ref[...] += hidden_step()

    @pl.when(j == pl.num_programs(1) - 1)
    def _():
        inv_ref[...] = _inv_rms(o_ref[...])
        g = FFN_RESIDUAL_WEIGHT * gpost_ref[...]

        def finish(rows):
            o_ref[rows, :] = x_ref[rows, :] + o_ref[rows, :] * _lane_tiled(inv_ref[rows, :], g.shape[1]) * g

        _for_row_chunks(tm, finish)


def _ffn(h, g_pre, g_post, w_in, w_out, layer, *, tm=1024, tf=256):
    T, D = h.shape
    F = w_out.shape[1]
    nf = F // tf
    return pl.pallas_call(
        _ffn_kernel,
        grid=(T // tm, nf),
        in_specs=[
            pl.BlockSpec((tm, D), lambda i, j: (i, 0)),
            pl.BlockSpec((1, D), lambda i, j: (0, 0)),
            pl.BlockSpec((1, D), lambda i, j: (0, 0)),
            pl.BlockSpec((None, D, tf), lambda i, j: (layer, 0, j)),
            pl.BlockSpec((None, D, tf), lambda i, j: (layer, 0, j + nf)),
            pl.BlockSpec((None, tf, D), lambda i, j: (layer, j, 0)),
        ],
        out_specs=pl.BlockSpec((tm, D), lambda i, j: (i, 0)),
        out_shape=jax.ShapeDtypeStruct((T, D), F32),
        scratch_shapes=[pltpu.VMEM((tm, D), BF16), pltpu.VMEM((tm, LANES), F32)],
        compiler_params=_params("parallel", "arbitrary", vmem_limit_bytes=FFN_VMEM_LIMIT_BYTES),
        name="ffn",
    )(h, g_pre, g_post, w_in, w_in, w_out)


def _norm_matmul_kernel(x_ref, g_ref, w_ref, o_ref, xn_ref, *, normalize):
    @pl.when(pl.program_id(1) == 0)
    def _():
        x = x_ref[...]
        if normalize:
            x = _rms(x, g_ref[...])
        xn_ref[...] = x.astype(BF16)

    o_ref[...] = _dot(xn_ref[...], w_ref[...]).astype(o_ref.dtype)


def _norm_matmul(x, g, w, layer, n_cols, out_dtype, *, normalize=True, tm=512, tn=1024):
    T, K = x.shape
    tn = min(tn, n_cols)
    return pl.pallas_call(
        functools.partial(_norm_matmul_kernel, normalize=normalize),
        grid=(T // tm, n_cols // tn),
        in_specs=[
            pl.BlockSpec((tm, K), lambda i, j: (i, 0)),
            pl.BlockSpec((1, K), lambda i, j: (0, 0)),
            pl.BlockSpec((None, K, tn), lambda i, j: (layer, 0, j)),
        ],
        out_specs=pl.BlockSpec((tm, tn), lambda i, j: (i, j)),
        out_shape=jax.ShapeDtypeStruct((T, n_cols), out_dtype),
        scratch_shapes=[pltpu.VMEM((tm, K), BF16)],
        compiler_params=_params("parallel", "arbitrary"),
        name="norm_matmul",
    )(x, g, w)


def _in_proj_kernel(x_ref, g_ref, w_ref, wfl_ref, cs_ref, axy_ref, qkv_ref, fl_ref, xn_ref, *, n_axy_tiles):
    j = pl.program_id(1)

    @pl.when(j == 0)
    def _():
        xn = _rms(x_ref[...], g_ref[...]).astype(BF16)
        xn_ref[...] = xn
        fl_ref[...] = _dot(xn, wfl_ref[...])

    @pl.when(j < n_axy_tiles)
    def _():
        axy_ref[...] = _dot(xn_ref[...], w_ref[...])

    @pl.when(j >= n_axy_tiles)
    def _():
        qkv_ref[...] = (_dot(xn_ref[...], w_ref[...]) * cs_ref[...]).astype(BF16)


def _in_proj(x, g, w, w_fl, col_scale, layer, n_axy, n_qkv, *, tm=1024, tn=1024):
    T, K = x.shape
    na, nq = n_axy // tn, n_qkv // tn
    return pl.pallas_call(
        functools.partial(_in_proj_kernel, n_axy_tiles=na),
        grid=(T // tm, na + nq),
        in_specs=[
            pl.BlockSpec((tm, K), lambda i, j: (i, 0)),
            pl.BlockSpec((1, K), lambda i, j: (0, 0)),
            pl.BlockSpec((None, K, tn), lambda i, j: (layer, 0, j)),
            pl.BlockSpec((None, K, LANES), lambda i, j: (layer, 0, 0)),
            pl.BlockSpec((1, tn), lambda i, j: (0, jnp.maximum(j - na, 0))),
        ],
        out_specs=[
            pl.BlockSpec((tm, tn), lambda i, j: (i, jnp.minimum(j, na - 1))),
            pl.BlockSpec((tm, tn), lambda i, j: (i, jnp.maximum(j - na, 0))),
            pl.BlockSpec((tm, LANES), lambda i, j: (i, 0)),
        ],
        out_shape=[
            jax.ShapeDtypeStruct((T, n_axy), F32),
            jax.ShapeDtypeStruct((T, n_qkv), BF16),
            jax.ShapeDtypeStruct((T, LANES), F32),
        ],
        scratch_shapes=[pltpu.VMEM((tm, K), BF16)],
        compiler_params=_params("parallel", "arbitrary", vmem_limit_bytes=FFN_VMEM_LIMIT_BYTES),
        name="in_proj",
    )(x, g, w, w_fl, col_scale)


def _softplus(x):
    return jnp.maximum(x, 0.0) + jnp.log1p(jnp.exp(-jnp.abs(x)))


def _rglru_kernel(ax_ref, ay_ref, cw_ref, cb_ref, gw_ref, gb_ref, lam_ref, o_ref,
                  xc_ref, tail_ref, hc_ref, *, ts):
    @pl.when(pl.program_id(1) == 0)
    def _():
        tail_ref[...] = jnp.zeros_like(tail_ref)
        hc_ref[...] = jnp.zeros_like(hc_ref)

    x = ax_ref[...]
    cw = cw_ref[...]
    cb = cb_ref[...]
    xc = cb + cw[CONV_W - 1:CONV_W] * x
    for d in range(1, CONV_W):
        xc = xc + cw[CONV_W - 1 - d:CONV_W - d] * pltpu.roll(x, d, axis=0)
    xc_ref[...] = xc
    x8 = x[0:SUBLANES]
    t8 = tail_ref[...]
    row8 = lax.broadcasted_iota(jnp.int32, x8.shape, 0)
    xc8 = cb + cw[CONV_W - 1:CONV_W] * x8
    for d in range(1, CONV_W):
        sh = jnp.where(row8 < d, pltpu.roll(t8, d, axis=0), pltpu.roll(x8, d, axis=0))
        xc8 = xc8 + cw[CONV_W - 1 - d:CONV_W - d] * sh
    xc_ref[0:SUBLANES, :] = xc8
    tail_ref[...] = x[ts - SUBLANES:ts]

    sp = _softplus(-lam_ref[...])
    row = lax.broadcasted_iota(jnp.int32, (ts // SUBLANES, SUBLANES, LRU_BLOCK_W), 1)
    for n in range(x.shape[1] // LRU_BLOCK_W):
        sl = slice(n * LRU_BLOCK_W, (n + 1) * LRU_BLOCK_W)
        xcn = xc_ref[:, sl]
        xb = xcn.astype(BF16)
        r = jax.nn.sigmoid(_dot(xb, gw_ref[0, n]) + gb_ref[0:1, sl])
        i = jax.nn.sigmoid(_dot(xb, gw_ref[1, n]) + gb_ref[1:2, sl])
        log_a = -RGLRU_C * r * sp[:, sl]
        a = jnp.exp(log_a)
        b = jnp.sqrt(-jnp.tanh(log_a) * (a * a + 1.0)) * (i * xcn)
        groups = ts // SUBLANES
        a = a.reshape(groups, SUBLANES, LRU_BLOCK_W)
        b = b.reshape(groups, SUBLANES, LRU_BLOCK_W)
        d = 1
        while d < SUBLANES:
            a_sh = jnp.where(row < d, 1.0, pltpu.roll(a, d, axis=1))
            b_sh = jnp.where(row < d, 0.0, pltpu.roll(b, d, axis=1))
            b = a * b_sh + b
            a = a * a_sh
            d *= 2
        carry = hc_ref[:, sl]
        hs = []
        for gi in range(groups):
            hs.append(b[gi] + a[gi] * carry)
            carry = hs[-1][SUBLANES - 1:SUBLANES]
        hc_ref[:, sl] = carry
        o_ref[:, sl] = (jnp.concatenate(hs, axis=0) * jax.nn.gelu(ay_ref[:, sl])).astype(o_ref.dtype)


def _rglru(axy, conv_w, conv_b, gate_w, gate_b, lam, batch, *, ts=256):
    T, W2 = axy.shape
    W = W2 // 2
    ns = T // batch // ts
    return pl.pallas_call(
        functools.partial(_rglru_kernel, ts=ts),
        grid=(batch, ns),
        in_specs=[
            pl.BlockSpec((ts, W), lambda b, s: (b * ns + s, 0)),
            pl.BlockSpec((ts, W), lambda b, s: (b * ns + s, 1)),
            pl.BlockSpec((CONV_W, W), lambda b, s: (0, 0)),
            pl.BlockSpec((1, W), lambda b, s: (0, 0)),
            pl.BlockSpec(gate_w.shape, lambda b, s: (0, 0, 0, 0)),
            pl.BlockSpec((2, W), lambda b, s: (0, 0)),
            pl.BlockSpec((1, W), lambda b, s: (0, 0)),
        ],
        out_specs=pl.BlockSpec((ts, W), lambda b, s: (b * ns + s, 0)),
        out_shape=jax.ShapeDtypeStruct((T, W), BF16),
        scratch_shapes=[pltpu.VMEM((ts, W), F32), pltpu.VMEM((SUBLANES, W), F32), pltpu.VMEM((1, W), F32)],
        compiler_params=_params("parallel", "arbitrary"),
        name="rglru",
    )(axy, axy, conv_w, conv_b, gate_w, gate_b, lam)


def _fox_prep_kernel(fl_ref, fb_ref, colb_ref, row_ref):
    x = fl_ref[...] + fb_ref[...]
    c = jnp.minimum(x, 0.0) - jnp.log1p(jnp.exp(-jnp.abs(x)))
    S = c.shape[0]
    row = lax.broadcasted_iota(jnp.int32, c.shape, 0)
    d = 1
    while d < S:
        c = c + jnp.where(row < d, 0.0, pltpu.roll(c, d, axis=0))
        d *= 2
    c = c * LOG2E
    for h in range(FOX_HEADS):
        colb_ref[h] = jnp.broadcast_to(c[:, h:h + 1], c.shape)
    for k in range(S // LANES):
        row_ref[:, k * LANES:(k + 1) * LANES] = c[k * LANES:(k + 1) * LANES, :].T[0:SUBLANES]


def _fox_prep(fl, f_bias, batch):
    T = fl.shape[0]
    S = T // batch
    return pl.pallas_call(
        _fox_prep_kernel,
        grid=(batch,),
        in_specs=[pl.BlockSpec((S, LANES), lambda b: (b, 0)), pl.BlockSpec((1, LANES), lambda b: (0, 0))],
        out_specs=[pl.BlockSpec((None, FOX_HEADS, S, LANES), lambda b: (b, 0, 0, 0)),
                   pl.BlockSpec((None, SUBLANES, S), lambda b: (b, 0, 0))],
        out_shape=[jax.ShapeDtypeStruct((batch, FOX_HEADS, S, LANES), F32),
                   jax.ShapeDtypeStruct((batch, SUBLANES, S), F32)],
        compiler_params=_params("parallel"),
        name="fox_prep",
    )(fl, f_bias)


def _t5_bucket(rel):
    nb = REL_BUCKETS // 2
    max_exact = nb // 2
    n = jnp.abs(rel)
    large = max_exact + (jnp.log(jnp.maximum(n, max_exact).astype(jnp.float32) / max_exact)
                         / math.log(REL_MAX_DIST / max_exact) * (nb - max_exact)).astype(jnp.int32)
    large = jnp.minimum(large, nb - 1)
    return jnp.where(rel > 0, nb, 0) + jnp.where(n < max_exact, n, large)


def _bias_tile_types(tq, tk):
    return -(-(tk - 1 + REL_MAX_DIST) // tq) + 1


def _bucket_tiles(tq, tk):
    nt = _bias_tile_types(tq, tk)
    t = jnp.arange(nt, dtype=jnp.int32)[:, None, None]
    c = jnp.arange(tk, dtype=jnp.int32)[None, :, None]
    r = jnp.arange(tq, dtype=jnp.int32)[None, None, :] + t * tq
    allowed = (c // CHUNK) <= (r // CHUNK)
    return jnp.where(allowed, _t5_bucket(c - r), -1)


def _bias_table_kernel(rb_ref, bucket_ref, o_ref):
    h = pl.program_id(0)
    bucket = bucket_ref[...]
    out = jnp.full(bucket.shape, NEG_INF, F32)
    for b in range(REL_BUCKETS):
        out = jnp.where(bucket == b, rb_ref[b, h] * LOG2E, out)
    o_ref[...] = out


def _bias_table(rel_bias, tq, tk):
    buckets = _bucket_tiles(tq, tk)
    nt = buckets.shape[0]
    H = rel_bias.shape[1]
    return pl.pallas_call(
        _bias_table_kernel,
        grid=(H, nt),
        in_specs=[
            pl.BlockSpec(memory_space=pltpu.SMEM),
            pl.BlockSpec((None, tk, tq), lambda h, t: (t, 0, 0)),
        ],
        out_specs=pl.BlockSpec((None, None, tk, tq), lambda h, t: (h, t, 0, 0)),
        out_shape=jax.ShapeDtypeStruct((H, nt, tk, tq), F32),
        compiler_params=_params("parallel", "parallel"),
        name="bias_table",
    )(rel_bias, buckets)


ONES_ROWS = 16
ATTN_HEADS_PER_STEP = 2


def _paired_flash_loop(pair, n_tiles, logits, vt1, sa_ref, sb_ref, m_ref, acc_ref):
    heads = range(m_ref.shape[1])
    n_steps = n_tiles + 1

    def step(w):
        if isinstance(w, int) and w < 2:
            return w, (pair, n_tiles - 1 - pair)[w], True
        sel = (w - 2 >= pair).astype(jnp.int32)
        return sel, w - 2 - sel * pair, False

    def fill(s_ref, w):
        sel, tile, diagonal = step(w)
        for h in heads:
            s_ref[h] = logits(h, sel, tile, diagonal)

    def update(s_ref, w):
        sel, tile, _ = step(w)
        for h in heads:
            s, m = s_ref[h], m_ref[sel, h]
            m_new = jnp.maximum(m, jnp.max(s, axis=0, keepdims=True))
            p = jnp.exp2(s - m_new).astype(BF16)
            acc_ref[sel, h] = jnp.exp2(m - m_new) * acc_ref[sel, h] + _dot(vt1(h, tile), p)
            m_ref[sel, h] = m_new

    m_ref[...] = jnp.full(m_ref.shape, NEG_INF, F32)
    acc_ref[...] = jnp.zeros(acc_ref.shape, F32)
    fill(sa_ref, 0)
    fill(sb_ref, 1)
    update(sa_ref, 0)
    fill(sa_ref, 2)
    update(sb_ref, 1)

    def two_steps(i, _):
        w = 3 + 2 * i
        fill(sb_ref, w)
        update(sa_ref, w - 1)
        fill(sa_ref, w + 1)
        update(sb_ref, w)
        return 0

    lax.fori_loop(0, (n_steps - 3) // 2, two_steps, 0)
    update(sa_ref, n_steps - 1)


def _flash_scratch(n_heads, tk, n, dv):
    return [pltpu.VMEM((n_heads, tk, n), F32), pltpu.VMEM((n_heads, tk, n), F32),
            pltpu.VMEM((2, n_heads, 1, n), F32), pltpu.VMEM((2, n_heads, dv + ONES_ROWS, n), F32)]


def _store_transposed(vt_ref, v_ref):
    n_heads, n_tiles, rows, tk = vt_ref.shape
    dv = rows - ONES_ROWS
    eye = (lax.broadcasted_iota(jnp.int32, (dv, dv), 0) == lax.broadcasted_iota(jnp.int32, (dv, dv), 1)).astype(BF16)
    for h in range(n_heads):
        for j in range(n_tiles):
            vt_ref[h, j, :dv, :] = _dot_nt(eye, v_ref[j * tk:(j + 1) * tk, h * dv:(h + 1) * dv]).astype(BF16)
            vt_ref[h, j, dv:, :] = jnp.ones((ONES_ROWS, tk), BF16)


def _diff_attn_kernel(q_ref, k_ref, v_ref, bias_ref, dl_ref, sg_ref, o_ref, vt_ref, qq_ref, sa_ref, sb_ref, m_ref,
                      acc_ref, *, tq, nt, lam_init):
    pair = pl.program_id(2)
    n_tiles = k_ref.shape[0] // tq

    @pl.when(pair == 0)
    def _():
        _store_transposed(vt_ref, v_ref)

    nh = ATTN_HEADS_PER_STEP
    q0s = (pl.multiple_of(pair * tq, tq), pl.multiple_of((n_tiles - 1 - pair) * tq, tq))
    lane = lax.broadcasted_iota(jnp.int32, (tq, LANES), 1)
    for sel in range(2):
        for h in range(nh):
            q = q_ref[pl.ds(q0s[sel], tq), h * LANES:(h + 1) * LANES]
            qq_ref[sel, h, :tq, :] = jnp.where(lane < DIFF_DH, q, 0)
            qq_ref[sel, h, tq:, :] = jnp.where(lane >= DIFF_DH, q, 0)

    def logits(h, sel, tile, diagonal):
        del diagonal
        k0 = pl.multiple_of(tile * tq, tq)
        q0 = q0s[0] + sel * (q0s[1] - q0s[0])
        s = _dot_nt(k_ref[pl.ds(k0, tq), h * LANES:(h + 1) * LANES], qq_ref[sel, h])
        bias = bias_ref[h, jnp.minimum((q0 - k0) // tq, nt - 1)]
        return s + jnp.concatenate([bias, bias], axis=1)

    _paired_flash_loop(pair, n_tiles, logits, lambda h, tile: vt_ref[h, tile], sa_ref, sb_ref, m_ref, acc_ref)
    dl = dl_ref[...]
    lam = (jnp.exp(jnp.sum(dl[0:1] * dl[1:2], axis=-1, keepdims=True))
           - jnp.exp(jnp.sum(dl[2:3] * dl[3:4], axis=-1, keepdims=True)) + lam_init)
    for sel in range(2):
        for h in range(nh):
            o = acc_ref[sel, h, :DIFF_VD, :] / acc_ref[sel, h, DIFF_VD:DIFF_VD + 1, :]
            d = o[:, :tq] - lam * o[:, tq:]
            y = d * lax.rsqrt(jnp.mean(d * d, axis=0, keepdims=True) + NORM_EPS) * sg_ref[...]
            o_ref[pl.ds(q0s[sel], tq), h * DIFF_VD:(h + 1) * DIFF_VD] = (y * (1.0 - lam_init)).T.astype(o_ref.dtype)


def _diff_attn(qkv, bias_table, diff_lambda, subln_g, batch, lam_init, *, tq):
    T = qkv.shape[0]
    S = T // batch
    nq = S // tq
    nt = bias_table.shape[1]
    nh = ATTN_HEADS_PER_STEP
    G = DIFF_HEADS // nh
    W = nh * LANES
    return pl.pallas_call(
        functools.partial(_diff_attn_kernel, tq=tq, nt=nt, lam_init=lam_init),
        grid=(batch, G, nq // 2),
        in_specs=[
            pl.BlockSpec((S, W), lambda b, g, i: (b, g)),
            pl.BlockSpec((S, W), lambda b, g, i: (b, G + g)),
            pl.BlockSpec((S, W), lambda b, g, i: (b, 2 * G + g)),
            pl.BlockSpec((nh, nt, tq, tq), lambda b, g, i: (g, 0, 0, 0)),
            pl.BlockSpec(diff_lambda.shape, lambda b, g, i: (0, 0)),
            pl.BlockSpec((DIFF_VD, 1), lambda b, g, i: (0, 0)),
        ],
        out_specs=pl.BlockSpec((S, nh * DIFF_VD), lambda b, g, i: (b, g)),
        out_shape=jax.ShapeDtypeStruct((T, DIFF_HEADS * DIFF_VD), BF16),
        scratch_shapes=([pltpu.VMEM((nh, nq, DIFF_VD + ONES_ROWS, tq), BF16), pltpu.VMEM((2, nh, 2 * tq, LANES), BF16)]
                        + _flash_scratch(nh, tq, 2 * tq, DIFF_VD)),
        compiler_params=_params("parallel", "parallel", "arbitrary"),
        name="diff_attn",
    )(qkv, qkv, qkv, bias_table, diff_lambda, subln_g)


def _fox_attn_kernel(q_ref, k_ref, v_ref, ck_ref, cq_ref, o_ref, vt_ref, sa_ref, sb_ref, m_ref, acc_ref, *, tq):
    pair = pl.program_id(2)
    n_tiles = k_ref.shape[0] // tq

    @pl.when(pair == 0)
    def _():
        _store_transposed(vt_ref, v_ref)

    nh = ATTN_HEADS_PER_STEP
    h0 = pl.program_id(1) * nh
    q_tiles = (pair, n_tiles - 1 - pair)
    q0s = tuple(pl.multiple_of(t * tq, tq) for t in q_tiles)
    cqs = [[cq_ref[q_tiles[sel], pl.ds(h0 + h, 1), :] for h in range(nh)] for sel in range(2)]

    def logits(h, sel, tile, diagonal):
        k0 = pl.multiple_of(tile * tq, tq)
        if diagonal:
            q0, cq = q0s[sel], cqs[sel][h]
        else:
            q0 = q0s[0] + sel * (q0s[1] - q0s[0])
            cq = jnp.where(sel == 1, cqs[1][h], cqs[0][h])
        q = q_ref[pl.ds(q0, tq), h * LANES:(h + 1) * LANES]
        ck = ck_ref[h, pl.ds(k0, tq), :]
        s = _dot_nt(k_ref[pl.ds(k0, tq), h * LANES:(h + 1) * LANES], q)
        s = s + (cq - jnp.concatenate([ck] * (tq // LANES), axis=1))
        if diagonal:
            s = jnp.where(lax.broadcasted_iota(jnp.int32, s.shape, 0) <= lax.broadcasted_iota(jnp.int32, s.shape, 1),
                          s, NEG_INF)
        return s

    _paired_flash_loop(pair, n_tiles, logits, lambda h, tile: vt_ref[h, tile], sa_ref, sb_ref, m_ref, acc_ref)
    for sel in range(2):
        for h in range(nh):
            o = acc_ref[sel, h, :FOX_DH, :] / acc_ref[sel, h, FOX_DH:FOX_DH + 1, :]
            o_ref[pl.ds(q0s[sel], tq), h * FOX_DH:(h + 1) * FOX_DH] = o.T.astype(o_ref.dtype)


def _fox_attn(qkv, cum_colb, cum_row, batch, *, tq):
    T = qkv.shape[0]
    S = T // batch
    nq = S // tq
    nh = ATTN_HEADS_PER_STEP
    G = FOX_HEADS // nh
    W = nh * LANES
    base = 3 * DIFF_HEADS // nh
    return pl.pallas_call(
        functools.partial(_fox_attn_kernel, tq=tq),
        grid=(batch, G, nq // 2),
        in_specs=[
            pl.BlockSpec((S, W), lambda b, g, i: (b, base + g)),
            pl.BlockSpec((S, W), lambda b, g, i: (b, base + G + g)),
            pl.BlockSpec((S, W), lambda b, g, i: (b, base + 2 * G + g)),
            pl.BlockSpec((None, nh, S, LANES), lambda b, g, i: (b, g, 0, 0)),
            pl.BlockSpec((None, nq, SUBLANES, tq), lambda b, g, i: (b, 0, 0, 0)),
        ],
        out_specs=pl.BlockSpec((S, nh * FOX_DH), lambda b, g, i: (b, g)),
        out_shape=jax.ShapeDtypeStruct((T, FOX_HEADS * FOX_DH), BF16),
        scratch_shapes=([pltpu.VMEM((nh, nq, FOX_DH + ONES_ROWS, tq), BF16)] + _flash_scratch(nh, tq, tq, FOX_DH)),
        compiler_params=_params("parallel", "parallel", "arbitrary"),
        name="fox_attn",
    )(qkv, qkv, qkv, cum_colb, cum_row)


def _xattn_kernel(h_ref, gq_ref, go_ref, wq_ref, k_ref, v_ref, wo_ref, o_ref, *, n_heads):
    x = h_ref[...]
    dh = x.shape[1] // n_heads
    xn = _rms(x, gq_ref[...]).astype(BF16)
    q = (_dot(xn, wq_ref[...]) * (LOG2E * dh ** -0.5)).astype(BF16)
    outs = []
    for hd in range(n_heads):
        sl = slice(hd * dh, (hd + 1) * dh)
        s = _dot_nt(q[:, sl], k_ref[:, sl])
        p = jnp.exp2(s - jnp.max(s, axis=-1, keepdims=True))
        l = jnp.sum(p, axis=-1, keepdims=True)
        outs.append((_dot(p.astype(BF16), v_ref[:, sl]) / l).astype(BF16))
    y = _dot(jnp.concatenate(outs, axis=1), wo_ref[...])
    o_ref[...] = x + _rms(y, go_ref[...])


def _xattn(h, g_q, g_o, w_q, kv, w_o, layer, batch, *, tm=512):
    T, D = h.shape
    M = kv.shape[0] // batch
    ns = T // batch // tm
    resident = pl.Buffered(1)
    return pl.pallas_call(
        functools.partial(_xattn_kernel, n_heads=XATTN_HEADS),
        grid=(batch, ns),
        in_specs=[
            pl.BlockSpec((tm, D), lambda b, i: (b * ns + i, 0)),
            pl.BlockSpec((1, D), lambda b, i: (0, 0)),
            pl.BlockSpec((1, D), lambda b, i: (0, 0)),
            pl.BlockSpec((None, D, D), lambda b, i: (layer, 0, 0), pipeline_mode=resident),
            pl.BlockSpec((M, D), lambda b, i: (b, 0)),
            pl.BlockSpec((M, D), lambda b, i: (b, 1)),
            pl.BlockSpec((None, D, D), lambda b, i: (layer, 0, 0), pipeline_mode=resident),
        ],
        out_specs=pl.BlockSpec((tm, D), lambda b, i: (b * ns + i, 0)),
        out_shape=jax.ShapeDtypeStruct((T, D), F32),
        compiler_params=_params("parallel", "arbitrary"),
        name="xattn",
    )(h, g_q, g_o, w_q, kv, kv, w_o)


def _proj_norm_res_kernel(*refs, n_in):
    x_refs, w_refs = refs[:n_in], refs[n_in:2 * n_in]
    g_ref, h_ref, o_ref = refs[2 * n_in:]
    y = _dot(x_refs[0][...], w_refs[0][...])
    for x_ref, w_ref in zip(x_refs[1:], w_refs[1:]):
        y = y + _dot(x_ref[...], w_ref[...])
    o_ref[...] = h_ref[...] + _rms(y, g_ref[...])


def _proj_norm_res(xs, w, layer, g, h, *, tm=512):
    T, N = h.shape
    n_in = len(xs)
    x_specs, w_specs, row0 = [], [], 0
    for x in xs:
        k = x.shape[1]
        x_specs.append(pl.BlockSpec((tm, k), lambda i: (i, 0)))
        w_specs.append(pl.BlockSpec((None, k, N), functools.partial(lambda i, r: (layer, r, 0), r=row0 // k)))
        row0 += k
    return pl.pallas_call(
        functools.partial(_proj_norm_res_kernel, n_in=n_in),
        grid=(T // tm,),
        in_specs=x_specs + w_specs + [pl.BlockSpec((1, N), lambda i: (0, 0)), pl.BlockSpec((tm, N), lambda i: (i, 0))],
        out_specs=pl.BlockSpec((tm, N), lambda i: (i, 0)),
        out_shape=jax.ShapeDtypeStruct((T, N), F32),
        compiler_params=_params("parallel"),
        name="proj_norm_res",
    )(*xs, *([w] * n_in), g, h)


def kernel(x, mem, norm_g, ffn1_w_in, ffn1_w_out, w_in, conv_w, conv_b, lru_gate_w, lru_gate_b, lru_lambda,
           diff_lambda, diff_subln_g, fox_f_bias, rel_bias, w_out, xattn_w_q, xattn_w_kv, xattn_w_o,
           ffn2_w_in, ffn2_w_out):
    B, S, D = x.shape
    depth = norm_g.shape[0]
    T = B * S
    lru_w = conv_w.shape[-1]
    n_axy = 2 * lru_w
    n_qkv = 3 * DIFF_HEADS * DIFF_VD + 3 * FOX_HEADS * FOX_DH
    attn_tq = 512

    h = x.astype(F32).reshape(T, D)
    mem2 = mem.astype(F32).reshape(B * mem.shape[1], D)
    ones_g = jnp.ones((1, D), F32)

    bf = lambda a: a.astype(BF16)
    ffn1_w_in, ffn1_w_out, ffn2_w_in, ffn2_w_out = bf(ffn1_w_in), bf(ffn1_w_out), bf(ffn2_w_in), bf(ffn2_w_out)
    w_in_b, w_out_b, gate_w_b = bf(w_in[:, :, :n_axy + n_qkv]), bf(w_out), bf(lru_gate_w)
    w_q_b, w_kv_b, w_o_b = bf(xattn_w_q), bf(xattn_w_kv), bf(xattn_w_o)
    w_fl = bf(jnp.pad(w_in[:, :, n_axy + n_qkv:], ((0, 0), (0, 0), (0, LANES - FOX_HEADS))))
    f_bias = jnp.pad(fox_f_bias, ((0, 0), (0, LANES - FOX_HEADS)))
    n_dq, n_fq = DIFF_HEADS * 2 * DIFF_DH, FOX_HEADS * FOX_DH
    fq0 = 3 * DIFF_HEADS * DIFF_VD
    col_scale = jnp.ones((1, n_qkv), F32)
    col_scale = col_scale.at[:, :n_dq].set(LOG2E * DIFF_DH ** -0.5).at[:, fq0:fq0 + n_fq].set(LOG2E * FOX_DH ** -0.5)

    bias_table = _bias_table(rel_bias, attn_tq, attn_tq)

    for l in range(depth):
        g = norm_g[l]
        lam_init = 0.8 - 0.6 * math.exp(-0.3 * l)
        h = _ffn(h, g[0:1], g[1:2], ffn1_w_in, ffn1_w_out, l)

        axy, qkv, fl = _in_proj(h, g[2:3], w_in_b, w_fl, col_scale, l, n_axy, n_qkv)
        a_out = _rglru(axy, conv_w[l], conv_b[l][None], gate_w_b[l], lru_gate_b[l], lru_lambda[l][None], B)
        cum_colb, cum_row = _fox_prep(fl, f_bias[l][None], B)
        cum_row = cum_row.reshape(B, SUBLANES, S // attn_tq, attn_tq).transpose(0, 2, 1, 3)
        d_out = _diff_attn(qkv, bias_table, diff_lambda[l], diff_subln_g[l][:, None], B, lam_init, tq=attn_tq)
        f_out = _fox_attn(qkv, cum_colb, cum_row, B, tq=attn_tq)
        h = _proj_norm_res([a_out, d_out, f_out], w_out_b, l, g[3:4], h)

        kv = _norm_matmul(mem2, ones_g, w_kv_b, l, 2 * D, BF16, normalize=False)
        h = _xattn(h, g[4:5], g[5:6], w_q_b, kv, w_o_b, l, B)

        h = _ffn(h, g[6:7], g[7:8], ffn2_w_in, ffn2_w_out, l)
    return h.reshape(B, S, D).astype(x.dtype)
```

```python
import functools
import math

import jax
import jax.numpy as jnp
import numpy as np
from jax import lax
from jax.experimental import pallas as pl
from jax.experimental.pallas import tpu as pltpu

F32 = jnp.float32
BF16 = jnp.bfloat16

CHUNK = 64
CONV_W = 4
LRU_BLOCK_W = 128
RGLRU_C = 8.0
DIFF_HEADS = 4
DIFF_DH = 64
DIFF_VD = 128
FOX_HEADS = 4
FOX_DH = 128
XATTN_HEADS = 4
REL_BUCKETS = 32
REL_MAX_DIST = 128
NORM_EPS = 1e-6
NEG_INF = -1e30
FFN_RESIDUAL_WEIGHT = 0.5
LOG2E = math.log2(math.e)

LANES = 128
SUBLANES = 8
V7X_VMEM_BYTES = 64 * 1024 * 1024
VMEM_LIMIT_BYTES = (V7X_VMEM_BYTES * 3) // 4
FFN_VMEM_LIMIT_BYTES = (V7X_VMEM_BYTES * 7) // 8


def _params(*semantics, vmem_limit_bytes=VMEM_LIMIT_BYTES):
    return pltpu.CompilerParams(dimension_semantics=semantics, vmem_limit_bytes=vmem_limit_bytes)


def _rms(x, g):
    return x * lax.rsqrt(jnp.mean(x * x, axis=-1, keepdims=True) + NORM_EPS) * g


def _dot(a, b):
    return jnp.dot(a, b, preferred_element_type=F32)


def _dot_nt(a, b):
    return lax.dot_general(a, b, (((1,), (1,)), ((), ())), preferred_element_type=F32)


NORM_ROW_CHUNK = 16


def _inv_rms(x):
    inv = lax.rsqrt(jnp.mean(x * x, axis=-1, keepdims=True) + NORM_EPS)
    return jnp.broadcast_to(inv, (x.shape[0], LANES))


def _lane_tiled(v, width):
    return jnp.concatenate([v] * (width // LANES), axis=1)


def _for_row_chunks(n_rows, fn):
    def body(c, _):
        fn(pl.ds(pl.multiple_of(c * NORM_ROW_CHUNK, NORM_ROW_CHUNK), NORM_ROW_CHUNK))
        return 0

    lax.fori_loop(0, n_rows // NORM_ROW_CHUNK, body, 0, unroll=4)


def _store_normed(xn_ref, x_ref, g, inv_ref):
    inv_ref[...] = _inv_rms(x_ref[...])

    def normalise(rows):
        xn_ref[rows, :] = (x_ref[rows, :] * _lane_tiled(inv_ref[rows, :], g.shape[1]) * g).astype(xn_ref.dtype)

    _for_row_chunks(x_ref.shape[0], normalise)


FFN_EDGE_ROWS = 256


def _ffn_kernel(x_ref, gpre_ref, gpost_ref, wg_ref, wu_ref, wo_ref, o_ref, xn_ref):
    j = pl.program_id(1)
    last = pl.num_programs(1) - 1
    edge_chunks = [slice(r, r + FFN_EDGE_ROWS) for r in range(0, x_ref.shape[0], FFN_EDGE_ROWS)]

    def hidden(xn):
        gate = _dot(xn, wg_ref[...])
        up = _dot(xn, wu_ref[...])
        act = (gate * jax.nn.sigmoid(gate) * up).astype(BF16)
        return _dot(act, wo_ref[...])

    @pl.when(j == 0)
    def _():
        g = gpre_ref[...]
        for rows in edge_chunks:
            xn = _rms(x_ref[rows, :], g).astype(BF16)
            xn_ref[rows, :] = xn
            o_ref[rows, :] = hidden(xn)

    @pl.when(jnp.logical_and(j > 0, j < last))
    def _():
        o_ref[...] += hidden(xn_ref[...])

    @pl.when(j == last)
    def _():
        g = FFN_RESIDUAL_WEIGHT * gpost_ref[...]
        for rows in edge_chunks:
            y = o_ref[rows, :] + hidden(xn_ref[rows, :])
            o_ref[rows, :] = x_ref[rows, :] + _rms(y, g)


def _ffn(h, g_pre, g_post, w_in, w_out, layer, *, tm=1024, tf=256):
    T, D = h.shape
    F = w_out.shape[1]
    nf = F // tf
    return pl.pallas_call(
        _ffn_kernel,
        grid=(T // tm, nf),
        in_specs=[
            pl.BlockSpec((tm, D), lambda i, j: (i, 0)),
            pl.BlockSpec((1, D), lambda i, j: (0, 0)),
            pl.BlockSpec((1, D), lambda i, j: (0, 0)),
            pl.BlockSpec((None, D, tf), lambda i, j: (layer, 0, j)),
            pl.BlockSpec((None, D, tf), lambda i, j: (layer, 0, j + nf)),
            pl.BlockSpec((None, tf, D), lambda i, j: (layer, j, 0)),
        ],
        out_specs=pl.BlockSpec((tm, D), lambda i, j: (i, 0)),
        out_shape=jax.ShapeDtypeStruct((T, D), F32),
        scratch_shapes=[pltpu.VMEM((tm, D), BF16)],
        compiler_params=_params("parallel", "arbitrary", vmem_limit_bytes=FFN_VMEM_LIMIT_BYTES),
        name="ffn",
    )(h, g_pre, g_post, w_in, w_in, w_out)


def _norm_matmul_kernel(x_ref, g_ref, w_ref, o_ref, xn_ref, *, normalize):
    @pl.when(pl.program_id(1) == 0)
    def _():
        x = x_ref[...]
        if normalize:
            x = _rms(x, g_ref[...])
        xn_ref[...] = x.astype(BF16)

    o_ref[...] = _dot(xn_ref[...], w_ref[...]).astype(o_ref.dtype)


def _norm_matmul(x, g, w, layer, n_cols, out_dtype, *, normalize=True, tm=512, tn=1024):
    T, K = x.shape
    tn = min(tn, n_cols)
    return pl.pallas_call(
        functools.partial(_norm_matmul_kernel, normalize=normalize),
        grid=(T // tm, n_cols // tn),
        in_specs=[
            pl.BlockSpec((tm, K), lambda i, j: (i, 0)),
            pl.BlockSpec((1, K), lambda i, j: (0, 0)),
            pl.BlockSpec((None, K, tn), lambda i, j: (layer, 0, j)),
        ],
        out_specs=pl.BlockSpec((tm, tn), lambda i, j: (i, j)),
        out_shape=jax.ShapeDtypeStruct((T, n_cols), out_dtype),
        scratch_shapes=[pltpu.VMEM((tm, K), BF16)],
        compiler_params=_params("parallel", "arbitrary"),
        name="norm_matmul",
    )(x, g, w)


def _in_proj_kernel(x_ref, g_ref, w_ref, wfl_ref, cs_ref, axy_ref, qkv_ref, fl_ref, xn_ref, inv_ref, *, n_axy_tiles):
    j = pl.program_id(1)

    @pl.when(j == 0)
    def _():
        _store_normed(xn_ref, x_ref, g_ref[...], inv_ref)
        fl_ref[...] = _dot(xn_ref[...], wfl_ref[...])

    @pl.when(j < n_axy_tiles)
    def _():
        axy_ref[...] = _dot(xn_ref[...], w_ref[...])

    @pl.when(j >= n_axy_tiles)
    def _():
        qkv_ref[...] = (_dot(xn_ref[...], w_ref[...]) * cs_ref[...]).astype(BF16)


def _in_proj(x, g, w, w_fl, col_scale, layer, n_axy, n_qkv, *, tm=1024, tn=1024):
    T, K = x.shape
    na, nq = n_axy // tn, n_qkv // tn
    return pl.pallas_call(
        functools.partial(_in_proj_kernel, n_axy_tiles=na),
        grid=(T // tm, na + nq),
        in_specs=[
            pl.BlockSpec((tm, K), lambda i, j: (i, 0)),
            pl.BlockSpec((1, K), lambda i, j: (0, 0)),
            pl.BlockSpec((None, K, tn), lambda i, j: (layer, 0, j)),
            pl.BlockSpec((None, K, LANES), lambda i, j: (layer, 0, 0)),
            pl.BlockSpec((1, tn), lambda i, j: (0, jnp.maximum(j - na, 0))),
        ],
        out_specs=[
            pl.BlockSpec((tm, tn), lambda i, j: (i, jnp.minimum(j, na - 1))),
            pl.BlockSpec((tm, tn), lambda i, j: (i, jnp.maximum(j - na, 0))),
            pl.BlockSpec((tm, LANES), lambda i, j: (i, 0)),
        ],
        out_shape=[
            jax.ShapeDtypeStruct((T, n_axy), F32),
            jax.ShapeDtypeStruct((T, n_qkv), BF16),
            jax.ShapeDtypeStruct((T, LANES), F32),
        ],
        scratch_shapes=[pltpu.VMEM((tm, K), BF16), pltpu.VMEM((tm, LANES), F32)],
        compiler_params=_params("parallel", "arbitrary", vmem_limit_bytes=FFN_VMEM_LIMIT_BYTES),
        name="in_proj",
    )(x, g, w, w_fl, col_scale)


def _softplus(x):
    return jnp.maximum(x, 0.0) + jnp.log1p(jnp.exp(-jnp.abs(x)))


def _rglru_kernel(ax_ref, ay_ref, cw_ref, cb_ref, gw_ref, gb_ref, lam_ref, o_ref,
                  xc_ref, tail_ref, hc_ref, *, ts):
    @pl.when(pl.program_id(1) == 0)
    def _():
        tail_ref[...] = jnp.zeros_like(tail_ref)
        hc_ref[...] = jnp.zeros_like(hc_ref)

    x = ax_ref[...]
    cw = cw_ref[...]
    cb = cb_ref[...]
    xc = cb + cw[CONV_W - 1:CONV_W] * x
    for d in range(1, CONV_W):
        xc = xc + cw[CONV_W - 1 - d:CONV_W - d] * pltpu.roll(x, d, axis=0)
    xc_ref[...] = xc
    x8 = x[0:SUBLANES]
    t8 = tail_ref[...]
    row8 = lax.broadcasted_iota(jnp.int32, x8.shape, 0)
    xc8 = cb + cw[CONV_W - 1:CONV_W] * x8
    for d in range(1, CONV_W):
        sh = jnp.where(row8 < d, pltpu.roll(t8, d, axis=0), pltpu.roll(x8, d, axis=0))
        xc8 = xc8 + cw[CONV_W - 1 - d:CONV_W - d] * sh
    xc_ref[0:SUBLANES, :] = xc8
    tail_ref[...] = x[ts - SUBLANES:ts]

    sp = _softplus(-lam_ref[...])
    row = lax.broadcasted_iota(jnp.int32, (ts // SUBLANES, SUBLANES, LRU_BLOCK_W), 1)
    for n in range(x.shape[1] // LRU_BLOCK_W):
        sl = slice(n * LRU_BLOCK_W, (n + 1) * LRU_BLOCK_W)
        xcn = xc_ref[:, sl]
        xb = xcn.astype(BF16)
        r = jax.nn.sigmoid(_dot(xb, gw_ref[0, n]) + gb_ref[0:1, sl])
        i = jax.nn.sigmoid(_dot(xb, gw_ref[1, n]) + gb_ref[1:2, sl])
        log_a = -RGLRU_C * r * sp[:, sl]
        a = jnp.exp(log_a)
        b = jnp.sqrt(-jnp.tanh(log_a) * (a * a + 1.0)) * (i * xcn)
        groups = ts // SUBLANES
        a = a.reshape(groups, SUBLANES, LRU_BLOCK_W)
        b = b.reshape(groups, SUBLANES, LRU_BLOCK_W)
        d = 1
        while d < SUBLANES:
            a_sh = jnp.where(row < d, 1.0, pltpu.roll(a, d, axis=1))
            b_sh = jnp.where(row < d, 0.0, pltpu.roll(b, d, axis=1))
            b = a * b_sh + b
            a = a * a_sh
            d *= 2
        carry = hc_ref[:, sl]
        hs = []
        for gi in range(groups):
            hs.append(b[gi] + a[gi] * carry)
            carry = hs[-1][SUBLANES - 1:SUBLANES]
        hc_ref[:, sl] = carry
        o_ref[:, sl] = (jnp.concatenate(hs, axis=0) * jax.nn.gelu(ay_ref[:, sl])).astype(o_ref.dtype)


def _rglru(axy, conv_w, conv_b, gate_w, gate_b, lam, batch, *, ts=256):
    T, W2 = axy.shape
    W = W2 // 2
    ns = T // batch // ts
    return pl.pallas_call(
        functools.partial(_rglru_kernel, ts=ts),
        grid=(batch, ns),
        in_specs=[
            pl.BlockSpec((ts, W), lambda b, s: (b * ns + s, 0)),
            pl.BlockSpec((ts, W), lambda b, s: (b * ns + s, 1)),
            pl.BlockSpec((CONV_W, W), lambda b, s: (0, 0)),
            pl.BlockSpec((1, W), lambda b, s: (0, 0)),
            pl.BlockSpec(gate_w.shape, lambda b, s: (0, 0, 0, 0)),
            pl.BlockSpec((2, W), lambda b, s: (0, 0)),
            pl.BlockSpec((1, W), lambda b, s: (0, 0)),
        ],
        out_specs=pl.BlockSpec((ts, W), lambda b, s: (b * ns + s, 0)),
        out_shape=jax.ShapeDtypeStruct((T, W), BF16),
        scratch_shapes=[pltpu.VMEM((ts, W), F32), pltpu.VMEM((SUBLANES, W), F32), pltpu.VMEM((1, W), F32)],
        compiler_params=_params("parallel", "arbitrary"),
        name="rglru",
    )(axy, axy, conv_w, conv_b, gate_w, gate_b, lam)


def _fox_prep_kernel(fl_ref, fb_ref, colb_ref, row_ref):
    x = fl_ref[...] + fb_ref[...]
    c = jnp.minimum(x, 0.0) - jnp.log1p(jnp.exp(-jnp.abs(x)))
    S = c.shape[0]
    row = lax.broadcasted_iota(jnp.int32, c.shape, 0)
    d = 1
    while d < S:
        c = c + jnp.where(row < d, 0.0, pltpu.roll(c, d, axis=0))
        d *= 2
    c = c * LOG2E
    for h in range(FOX_HEADS):
        colb_ref[h] = jnp.broadcast_to(c[:, h:h + 1], c.shape)
    for k in range(S // LANES):
        row_ref[:, k * LANES:(k + 1) * LANES] = c[k * LANES:(k + 1) * LANES, :].T[0:SUBLANES]


def _fox_prep(fl, f_bias, batch):
    T = fl.shape[0]
    S = T // batch
    return pl.pallas_call(
        _fox_prep_kernel,
        grid=(batch,),
        in_specs=[pl.BlockSpec((S, LANES), lambda b: (b, 0)), pl.BlockSpec((1, LANES), lambda b: (0, 0))],
        out_specs=[pl.BlockSpec((None, FOX_HEADS, S, LANES), lambda b: (b, 0, 0, 0)),
                   pl.BlockSpec((None, SUBLANES, S), lambda b: (b, 0, 0))],
        out_shape=[jax.ShapeDtypeStruct((batch, FOX_HEADS, S, LANES), F32),
                   jax.ShapeDtypeStruct((batch, SUBLANES, S), F32)],
        compiler_params=_params("parallel"),
        name="fox_prep",
    )(fl, f_bias)


def _t5_bucket(rel):
    nb = REL_BUCKETS // 2
    max_exact = nb // 2
    n = jnp.abs(rel)
    large = max_exact + (jnp.log(jnp.maximum(n, max_exact).astype(jnp.float32) / max_exact)
                         / math.log(REL_MAX_DIST / max_exact) * (nb - max_exact)).astype(jnp.int32)
    large = jnp.minimum(large, nb - 1)
    return jnp.where(rel > 0, nb, 0) + jnp.where(n < max_exact, n, large)


def _bias_tile_types(tq, tk):
    return -(-(tk - 1 + REL_MAX_DIST) // tq) + 1


def _bucket_tiles(tq, tk):
    nt = _bias_tile_types(tq, tk)
    t = jnp.arange(nt, dtype=jnp.int32)[:, None, None]
    c = jnp.arange(tk, dtype=jnp.int32)[None, :, None]
    r = jnp.arange(tq, dtype=jnp.int32)[None, None, :] + t * tq
    allowed = (c // CHUNK) <= (r // CHUNK)
    return jnp.where(allowed, _t5_bucket(c - r), -1)


def _bias_table_kernel(rb_ref, bucket_ref, o_ref):
    h = pl.program_id(0)
    bucket = bucket_ref[...]
    out = jnp.full(bucket.shape, NEG_INF, F32)
    for b in range(REL_BUCKETS):
        out = jnp.where(bucket == b, rb_ref[b, h] * LOG2E, out)
    o_ref[...] = out


def _bias_table(rel_bias, tq, tk):
    buckets = _bucket_tiles(tq, tk)
    nt = buckets.shape[0]
    H = rel_bias.shape[1]
    return pl.pallas_call(
        _bias_table_kernel,
        grid=(H, nt),
        in_specs=[
            pl.BlockSpec(memory_space=pltpu.SMEM),
            pl.BlockSpec((None, tk, tq), lambda h, t: (t, 0, 0)),
        ],
        out_specs=pl.BlockSpec((None, None, tk, tq), lambda h, t: (h, t, 0, 0)),
        out_shape=jax.ShapeDtypeStruct((H, nt, tk, tq), F32),
        compiler_params=_params("parallel", "parallel"),
        name="bias_table",
    )(rel_bias, buckets)


ONES_ROWS = 16
ATTN_HEADS_PER_STEP = 2


def _paired_flash_loop(pair, n_tiles, logits, vt1, sa_ref, sb_ref, m_ref, acc_ref):
    heads = range(m_ref.shape[1])
    n_steps = n_tiles + 1

    def step(w):
        if isinstance(w, int) and w < 2:
            return w, (pair, n_tiles - 1 - pair)[w], True
        sel = (w - 2 >= pair).astype(jnp.int32)
        return sel, w - 2 - sel * pair, False

    def fill(s_ref, w):
        sel, tile, diagonal = step(w)
        for h in heads:
            s_ref[h] = logits(h, sel, tile, diagonal)

    def update(s_ref, w):
        sel, tile, _ = step(w)
        for h in heads:
            s, m = s_ref[h], m_ref[sel, h]
            m_new = jnp.maximum(m, jnp.max(s, axis=0, keepdims=True))
            p = jnp.exp2(s - m_new).astype(BF16)
            acc_ref[sel, h] = jnp.exp2(m - m_new) * acc_ref[sel, h] + _dot(vt1(h, tile), p)
            m_ref[sel, h] = m_new

    m_ref[...] = jnp.full(m_ref.shape, NEG_INF, F32)
    acc_ref[...] = jnp.zeros(acc_ref.shape, F32)
    fill(sa_ref, 0)
    fill(sb_ref, 1)
    update(sa_ref, 0)
    fill(sa_ref, 2)
    update(sb_ref, 1)

    def two_steps(i, _):
        w = 3 + 2 * i
        fill(sb_ref, w)
        update(sa_ref, w - 1)
        fill(sa_ref, w + 1)
        update(sb_ref, w)
        return 0

    lax.fori_loop(0, (n_steps - 3) // 2, two_steps, 0)
    update(sa_ref, n_steps - 1)


def _flash_scratch(n_heads, tk, n, dv):
    return [pltpu.VMEM((n_heads, tk, n), F32), pltpu.VMEM((n_heads, tk, n), F32),
            pltpu.VMEM((2, n_heads, 1, n), F32), pltpu.VMEM((2, n_heads, dv + ONES_ROWS, n), F32)]


def _store_transposed(vt_ref, v_ref):
    n_heads, n_tiles, rows, tk = vt_ref.shape
    dv = rows - ONES_ROWS
    eye = (lax.broadcasted_iota(jnp.int32, (dv, dv), 0) == lax.broadcasted_iota(jnp.int32, (dv, dv), 1)).astype(BF16)
    for h in range(n_heads):
        for j in range(n_tiles):
            vt_ref[h, j, :dv, :] = _dot_nt(eye, v_ref[j * tk:(j + 1) * tk, h * dv:(h + 1) * dv]).astype(BF16)
            vt_ref[h, j, dv:, :] = jnp.ones((ONES_ROWS, tk), BF16)


def _diff_attn_kernel(q_ref, k_ref, v_ref, bias_ref, dl_ref, sg_ref, o_ref, vt_ref, qq_ref, sa_ref, sb_ref, m_ref,
                      acc_ref, *, tq, nt, lam_init):
    pair = pl.program_id(2)
    n_tiles = k_ref.shape[0] // tq

    @pl.when(pair == 0)
    def _():
        _store_transposed(vt_ref, v_ref)

    nh = ATTN_HEADS_PER_STEP
    q0s = (pl.multiple_of(pair * tq, tq), pl.multiple_of((n_tiles - 1 - pair) * tq, tq))
    lane = lax.broadcasted_iota(jnp.int32, (tq, LANES), 1)
    for sel in range(2):
        for h in range(nh):
            q = q_ref[pl.ds(q0s[sel], tq), h * LANES:(h + 1) * LANES]
            qq_ref[sel, h, :tq, :] = jnp.where(lane < DIFF_DH, q, 0)
            qq_ref[sel, h, tq:, :] = jnp.where(lane >= DIFF_DH, q, 0)

    def logits(h, sel, tile, diagonal):
        del diagonal
        k0 = pl.multiple_of(tile * tq, tq)
        q0 = q0s[0] + sel * (q0s[1] - q0s[0])
        s = _dot_nt(k_ref[pl.ds(k0, tq), h * LANES:(h + 1) * LANES], qq_ref[sel, h])
        bias = bias_ref[h, jnp.minimum((q0 - k0) // tq, nt - 1)]
        return s + jnp.concatenate([bias, bias], axis=1)

    _paired_flash_loop(pair, n_tiles, logits, lambda h, tile: vt_ref[h, tile], sa_ref, sb_ref, m_ref, acc_ref)
    dl = dl_ref[...]
    lam = (jnp.exp(jnp.sum(dl[0:1] * dl[1:2], axis=-1, keepdims=True))
           - jnp.exp(jnp.sum(dl[2:3] * dl[3:4], axis=-1, keepdims=True)) + lam_init)
    for sel in range(2):
        for h in range(nh):
            o = acc_ref[sel, h, :DIFF_VD, :] / acc_ref[sel, h, DIFF_VD:DIFF_VD + 1, :]
            d = o[:, :tq] - lam * o[:, tq:]
            y = d * lax.rsqrt(jnp.mean(d * d, axis=0, keepdims=True) + NORM_EPS) * sg_ref[...]
            o_ref[pl.ds(q0s[sel], tq), h * DIFF_VD:(h + 1) * DIFF_VD] = (y * (1.0 - lam_init)).T.astype(o_ref.dtype)


def _diff_attn(qkv, bias_table, diff_lambda, subln_g, batch, lam_init, *, tq):
    T = qkv.shape[0]
    S = T // batch
    nq = S // tq
    nt = bias_table.shape[1]
    nh = ATTN_HEADS_PER_STEP
    G = DIFF_HEADS // nh
    W = nh * LANES
    return pl.pallas_call(
        functools.partial(_diff_attn_kernel, tq=tq, nt=nt, lam_init=lam_init),
        grid=(batch, G, nq // 2),
        in_specs=[
            pl.BlockSpec((S, W), lambda b, g, i: (b, g)),
            pl.BlockSpec((S, W), lambda b, g, i: (b, G + g)),
            pl.BlockSpec((S, W), lambda b, g, i: (b, 2 * G + g)),
            pl.BlockSpec((nh, nt, tq, tq), lambda b, g, i: (g, 0, 0, 0)),
            pl.BlockSpec(diff_lambda.shape, lambda b, g, i: (0, 0)),
            pl.BlockSpec((DIFF_VD, 1), lambda b, g, i: (0, 0)),
        ],
        out_specs=pl.BlockSpec((S, nh * DIFF_VD), lambda b, g, i: (b, g)),
        out_shape=jax.ShapeDtypeStruct((T, DIFF_HEADS * DIFF_VD), BF16),
        scratch_shapes=([pltpu.VMEM((nh, nq, DIFF_VD + ONES_ROWS, tq), BF16), pltpu.VMEM((2, nh, 2 * tq, LANES), BF16)]
                        + _flash_scratch(nh, tq, 2 * tq, DIFF_VD)),
        compiler_params=_params("parallel", "parallel", "arbitrary"),
        name="diff_attn",
    )(qkv, qkv, qkv, bias_table, diff_lambda, subln_g)


def _fox_attn_kernel(q_ref, k_ref, v_ref, ck_ref, cq_ref, o_ref, vt_ref, sa_ref, sb_ref, m_ref, acc_ref, *, tq):
    pair = pl.program_id(2)
    n_tiles = k_ref.shape[0] // tq

    @pl.when(pair == 0)
    def _():
        _store_transposed(vt_ref, v_ref)

    nh = ATTN_HEADS_PER_STEP
    h0 = pl.program_id(1) * nh
    q_tiles = (pair, n_tiles - 1 - pair)
    q0s = tuple(pl.multiple_of(t * tq, tq) for t in q_tiles)
    cqs = [[cq_ref[q_tiles[sel], pl.ds(h0 + h, 1), :] for h in range(nh)] for sel in range(2)]

    def logits(h, sel, tile, diagonal):
        k0 = pl.multiple_of(tile * tq, tq)
        if diagonal:
            q0, cq = q0s[sel], cqs[sel][h]
        else:
            q0 = q0s[0] + sel * (q0s[1] - q0s[0])
            cq = jnp.where(sel == 1, cqs[1][h], cqs[0][h])
        q = q_ref[pl.ds(q0, tq), h * LANES:(h + 1) * LANES]
        ck = ck_ref[h, pl.ds(k0, tq), :]
        s = _dot_nt(k_ref[pl.ds(k0, tq), h * LANES:(h + 1) * LANES], q)
        s = s + (cq - jnp.concatenate([ck] * (tq // LANES), axis=1))
        if diagonal:
            s = jnp.where(lax.broadcasted_iota(jnp.int32, s.shape, 0) <= lax.broadcasted_iota(jnp.int32, s.shape, 1),
                          s, NEG_INF)
        return s

    _paired_flash_loop(pair, n_tiles, logits, lambda h, tile: vt_ref[h, tile], sa_ref, sb_ref, m_ref, acc_ref)
    for sel in range(2):
        for h in range(nh):
            o = acc_ref[sel, h, :FOX_DH, :] / acc_ref[sel, h, FOX_DH:FOX_DH + 1, :]
            o_ref[pl.ds(q0s[sel], tq), h * FOX_DH:(h + 1) * FOX_DH] = o.T.astype(o_ref.dtype)


def _fox_attn(qkv, cum_colb, cum_row, batch, *, tq):
    T = qkv.shape[0]
    S = T // batch
    nq = S // tq
    nh = ATTN_HEADS_PER_STEP
    G = FOX_HEADS // nh
    W = nh * LANES
    base = 3 * DIFF_HEADS // nh
    return pl.pallas_call(
        functools.partial(_fox_attn_kernel, tq=tq),
        grid=(batch, G, nq // 2),
        in_specs=[
            pl.BlockSpec((S, W), lambda b, g, i: (b, base + g)),
            pl.BlockSpec((S, W), lambda b, g, i: (b, base + G + g)),
            pl.BlockSpec((S, W), lambda b, g, i: (b, base + 2 * G + g)),
            pl.BlockSpec((None, nh, S, LANES), lambda b, g, i: (b, g, 0, 0)),
            pl.BlockSpec((None, nq, SUBLANES, tq), lambda b, g, i: (b, 0, 0, 0)),
        ],
        out_specs=pl.BlockSpec((S, nh * FOX_DH), lambda b, g, i: (b, g)),
        out_shape=jax.ShapeDtypeStruct((T, FOX_HEADS * FOX_DH), BF16),
        scratch_shapes=([pltpu.VMEM((nh, nq, FOX_DH + ONES_ROWS, tq), BF16)] + _flash_scratch(nh, tq, tq, FOX_DH)),
        compiler_params=_params("parallel", "parallel", "arbitrary"),
        name="fox_attn",
    )(qkv, qkv, qkv, cum_colb, cum_row)


def _xattn_kernel(h_ref, gq_ref, go_ref, wq_ref, k_ref, v_ref, wo_ref, o_ref, *, n_heads):
    x = h_ref[...]
    dh = x.shape[1] // n_heads
    xn = _rms(x, gq_ref[...]).astype(BF16)
    q = (_dot(xn, wq_ref[...]) * (LOG2E * dh ** -0.5)).astype(BF16)
    outs = []
    for hd in range(n_heads):
        sl = slice(hd * dh, (hd + 1) * dh)
        s = _dot_nt(q[:, sl], k_ref[:, sl])
        p = jnp.exp2(s - jnp.max(s, axis=-1, keepdims=True))
        l = jnp.sum(p, axis=-1, keepdims=True)
        outs.append((_dot(p.astype(BF16), v_ref[:, sl]) / l).astype(BF16))
    y = _dot(jnp.concatenate(outs, axis=1), wo_ref[...])
    o_ref[...] = x + _rms(y, go_ref[...])


def _xattn(h, g_q, g_o, w_q, kv, w_o, layer, batch, *, tm=512):
    T, D = h.shape
    M = kv.shape[0] // batch
    ns = T // batch // tm
    resident = pl.Buffered(1)
    return pl.pallas_call(
        functools.partial(_xattn_kernel, n_heads=XATTN_HEADS),
        grid=(batch, ns),
        in_specs=[
            pl.BlockSpec((tm, D), lambda b, i: (b * ns + i, 0)),
            pl.BlockSpec((1, D), lambda b, i: (0, 0)),
            pl.BlockSpec((1, D), lambda b, i: (0, 0)),
            pl.BlockSpec((None, D, D), lambda b, i: (layer, 0, 0), pipeline_mode=resident),
            pl.BlockSpec((M, D), lambda b, i: (b, 0)),
            pl.BlockSpec((M, D), lambda b, i: (b, 1)),
            pl.BlockSpec((None, D, D), lambda b, i: (layer, 0, 0), pipeline_mode=resident),
        ],
        out_specs=pl.BlockSpec((tm, D), lambda b, i: (b * ns + i, 0)),
        out_shape=jax.ShapeDtypeStruct((T, D), F32),
        compiler_params=_params("parallel", "arbitrary"),
        name="xattn",
    )(h, g_q, g_o, w_q, kv, kv, w_o)


def _proj_norm_res_kernel(*refs, n_in):
    x_refs, w_refs = refs[:n_in], refs[n_in:2 * n_in]
    g_ref, h_ref, o_ref = refs[2 * n_in:]
    y = _dot(x_refs[0][...], w_refs[0][...])
    for x_ref, w_ref in zip(x_refs[1:], w_refs[1:]):
        y = y + _dot(x_ref[...], w_ref[...])
    o_ref[...] = h_ref[...] + _rms(y, g_ref[...])


def _proj_norm_res(xs, w, layer, g, h, *, tm=512):
    T, N = h.shape
    n_in = len(xs)
    x_specs, w_specs, row0 = [], [], 0
    for x in xs:
        k = x.shape[1]
        x_specs.append(pl.BlockSpec((tm, k), lambda i: (i, 0)))
        w_specs.append(pl.BlockSpec((None, k, N), functools.partial(lambda i, r: (layer, r, 0), r=row0 // k)))
        row0 += k
    return pl.pallas_call(
        functools.partial(_proj_norm_res_kernel, n_in=n_in),
        grid=(T // tm,),
        in_specs=x_specs + w_specs + [pl.BlockSpec((1, N), lambda i: (0, 0)), pl.BlockSpec((tm, N), lambda i: (i, 0))],
        out_specs=pl.BlockSpec((tm, N), lambda i: (i, 0)),
        out_shape=jax.ShapeDtypeStruct((T, N), F32),
        compiler_params=_params("parallel"),
        name="proj_norm_res",
    )(*xs, *([w] * n_in), g, h)


def kernel(x, mem, norm_g, ffn1_w_in, ffn1_w_out, w_in, conv_w, conv_b, lru_gate_w, lru_gate_b, lru_lambda,
           diff_lambda, diff_subln_g, fox_f_bias, rel_bias, w_out, xattn_w_q, xattn_w_kv, xattn_w_o,
           ffn2_w_in, ffn2_w_out):
    B, S, D = x.shape
    depth = norm_g.shape[0]
    T = B * S
    lru_w = conv_w.shape[-1]
    n_axy = 2 * lru_w
    n_qkv = 3 * DIFF_HEADS * DIFF_VD + 3 * FOX_HEADS * FOX_DH
    attn_tq = 512

    h = x.astype(F32).reshape(T, D)
    mem2 = mem.astype(F32).reshape(B * mem.shape[1], D)
    ones_g = jnp.ones((1, D), F32)

    bf = lambda a: a.astype(BF16)
    ffn1_w_in, ffn1_w_out, ffn2_w_in, ffn2_w_out = bf(ffn1_w_in), bf(ffn1_w_out), bf(ffn2_w_in), bf(ffn2_w_out)
    w_in_b, w_out_b, gate_w_b = bf(w_in), bf(w_out), bf(lru_gate_w)
    w_q_b, w_kv_b, w_o_b = bf(xattn_w_q), bf(xattn_w_kv), bf(xattn_w_o)
    w_fl = bf(jnp.pad(w_in[:, :, n_axy + n_qkv:], ((0, 0), (0, 0), (0, LANES - FOX_HEADS))))
    f_bias = jnp.pad(fox_f_bias, ((0, 0), (0, LANES - FOX_HEADS)))
    n_dq, n_fq = DIFF_HEADS * 2 * DIFF_DH, FOX_HEADS * FOX_DH
    fq0 = 3 * DIFF_HEADS * DIFF_VD
    col_scale = jnp.ones((1, n_qkv), F32)
    col_scale = col_scale.at[:, :n_dq].set(LOG2E * DIFF_DH ** -0.5).at[:, fq0:fq0 + n_fq].set(LOG2E * FOX_DH ** -0.5)

    bias_table = _bias_table(rel_bias, attn_tq, attn_tq)

    for l in range(depth):
        g = norm_g[l]
        lam_init = 0.8 - 0.6 * math.exp(-0.3 * l)
        h = _ffn(h, g[0:1], g[1:2], ffn1_w_in, ffn1_w_out, l)

        axy, qkv, fl = _in_proj(h, g[2:3], w_in_b, w_fl, col_scale, l, n_axy, n_qkv)
        a_out = _rglru(axy, conv_w[l], conv_b[l][None], gate_w_b[l], lru_gate_b[l], lru_lambda[l][None], B)
        cum_colb, cum_row = _fox_prep(fl, f_bias[l][None], B)
        cum_row = cum_row.reshape(B, SUBLANES, S // attn_tq, attn_tq).transpose(0, 2, 1, 3)
        d_out = _diff_attn(qkv, bias_table, diff_lambda[l], diff_subln_g[l][:, None], B, lam_init, tq=attn_tq)
        f_out = _fox_attn(qkv, cum_colb, cum_row, B, tq=attn_tq)
        h = _proj_norm_res([a_out, d_out, f_out], w_out_b, l, g[3:4], h)

        kv = _norm_matmul(mem2, ones_g, w_kv_b, l, 2 * D, BF16, normalize=False)
        h = _xattn(h, g[4:5], g[5:6], w_q_b, kv, w_o_b, l, B)

        h = _ffn(h, g[6:7], g[7:8], ffn2_w_in, ffn2_w_out, l)
    return h.reshape(B, S, D).astype(x.dtype)
```

```python
import functools
import math

import jax
import jax.numpy as jnp
import numpy as np
from jax import lax
from jax.experimental import pallas as pl
from jax.experimental.pallas import tpu as pltpu

F32 = jnp.float32
BF16 = jnp.bfloat16

CHUNK = 64
CONV_W = 4
LRU_BLOCK_W = 128
RGLRU_C = 8.0
DIFF_HEADS = 4
DIFF_DH = 64
DIFF_VD = 128
FOX_HEADS = 4
FOX_DH = 128
XATTN_HEADS = 4
REL_BUCKETS = 32
REL_MAX_DIST = 128
NORM_EPS = 1e-6
NEG_INF = -1e30
FFN_RESIDUAL_WEIGHT = 0.5
LOG2E = math.log2(math.e)

LANES = 128
SUBLANES = 8
V7X_VMEM_BYTES = 64 * 1024 * 1024
VMEM_LIMIT_BYTES = (V7X_VMEM_BYTES * 3) // 4
FFN_VMEM_LIMIT_BYTES = (V7X_VMEM_BYTES * 7) // 8


def _params(*semantics, vmem_limit_bytes=VMEM_LIMIT_BYTES):
    return pltpu.CompilerParams(dimension_semantics=semantics, vmem_limit_bytes=vmem_limit_bytes)


def _rms(x, g):
    return x * lax.rsqrt(jnp.mean(x * x, axis=-1, keepdims=True) + NORM_EPS) * g


def _dot(a, b):
    return jnp.dot(a, b, preferred_element_type=F32)


def _dot_nt(a, b):
    return lax.dot_general(a, b, (((1,), (1,)), ((), ())), preferred_element_type=F32)


FFN_EDGE_ROWS = 256


def _ffn_kernel(x_ref, gpre_ref, gpost_ref, wg_ref, wu_ref, wo_ref, o_ref, xn_ref):
    j = pl.program_id(1)
    last = pl.num_programs(1) - 1
    edge_chunks = [slice(r, r + FFN_EDGE_ROWS) for r in range(0, x_ref.shape[0], FFN_EDGE_ROWS)]

    def hidden(xn):
        gate = _dot(xn, wg_ref[...])
        up = _dot(xn, wu_ref[...])
        act = (gate * jax.nn.sigmoid(gate) * up).astype(BF16)
        return _dot(act, wo_ref[...])

    @pl.when(j == 0)
    def _():
        g = gpre_ref[...]
        for rows in edge_chunks:
            xn = _rms(x_ref[rows, :], g).astype(BF16)
            xn_ref[rows, :] = xn
            o_ref[rows, :] = hidden(xn)

    @pl.when(jnp.logical_and(j > 0, j < last))
    def _():
        o_ref[...] += hidden(xn_ref[...])

    @pl.when(j == last)
    def _():
        g = FFN_RESIDUAL_WEIGHT * gpost_ref[...]
        for rows in edge_chunks:
            y = o_ref[rows, :] + hidden(xn_ref[rows, :])
            o_ref[rows, :] = x_ref[rows, :] + _rms(y, g)


def _ffn(h, g_pre, g_post, w_in, w_out, layer, *, tm=1024, tf=256):
    T, D = h.shape
    F = w_out.shape[1]
    nf = F // tf
    return pl.pallas_call(
        _ffn_kernel,
        grid=(T // tm, nf),
        in_specs=[
            pl.BlockSpec((tm, D), lambda i, j: (i, 0)),
            pl.BlockSpec((1, D), lambda i, j: (0, 0)),
            pl.BlockSpec((1, D), lambda i, j: (0, 0)),
            pl.BlockSpec((None, D, tf), lambda i, j: (layer, 0, j)),
            pl.BlockSpec((None, D, tf), lambda i, j: (layer, 0, j + nf)),
            pl.BlockSpec((None, tf, D), lambda i, j: (layer, j, 0)),
        ],
        out_specs=pl.BlockSpec((tm, D), lambda i, j: (i, 0)),
        out_shape=jax.ShapeDtypeStruct((T, D), F32),
        scratch_shapes=[pltpu.VMEM((tm, D), BF16)],
        compiler_params=_params("parallel", "arbitrary", vmem_limit_bytes=FFN_VMEM_LIMIT_BYTES),
        name="ffn",
    )(h, g_pre, g_post, w_in, w_in, w_out)


def _norm_matmul_kernel(x_ref, g_ref, w_ref, o_ref, xn_ref, *, normalize):
    @pl.when(pl.program_id(1) == 0)
    def _():
        x = x_ref[...]
        if normalize:
            x = _rms(x, g_ref[...])
        xn_ref[...] = x.astype(BF16)

    o_ref[...] = _dot(xn_ref[...], w_ref[...]).astype(o_ref.dtype)


def _norm_matmul(x, g, w, layer, n_cols, out_dtype, *, normalize=True, tm=512, tn=1024):
    T, K = x.shape
    tn = min(tn, n_cols)
    return pl.pallas_call(
        functools.partial(_norm_matmul_kernel, normalize=normalize),
        grid=(T // tm, n_cols // tn),
        in_specs=[
            pl.BlockSpec((tm, K), lambda i, j: (i, 0)),
            pl.BlockSpec((1, K), lambda i, j: (0, 0)),
            pl.BlockSpec((None, K, tn), lambda i, j: (layer, 0, j)),
        ],
        out_specs=pl.BlockSpec((tm, tn), lambda i, j: (i, j)),
        out_shape=jax.ShapeDtypeStruct((T, n_cols), out_dtype),
        scratch_shapes=[pltpu.VMEM((tm, K), BF16)],
        compiler_params=_params("parallel", "arbitrary"),
        name="norm_matmul",
    )(x, g, w)


def _in_proj_kernel(x_ref, g_ref, w_ref, wfl_ref, cs_ref, cw_ref, cb_ref, gw_ref, gb_ref, lam_ref,
                    qkv_ref, fl_ref, aout_ref, xn_ref, axy_ref, xc_ref, tail_ref, hc_ref,
                    *, n_axy_tiles, chunk_starts, tiles_per_seq):
    i, j = pl.program_id(0), pl.program_id(1)

    @pl.when(j == 0)
    def _():
        xn = _rms(x_ref[...], g_ref[...]).astype(BF16)
        xn_ref[...] = xn
        fl_ref[...] = _dot(xn, wfl_ref[...])

    @pl.when(jnp.logical_and(j == 0, i % tiles_per_seq == 0))
    def _():
        tail_ref[...] = jnp.zeros_like(tail_ref)
        hc_ref[...] = jnp.zeros_like(hc_ref)

    @pl.when(j < n_axy_tiles)
    def _():
        axy_ref[j] = _dot(xn_ref[...], w_ref[...])

    for c, (r0, r1) in enumerate(zip(chunk_starts[:-1], chunk_starts[1:])):
        @pl.when(j == n_axy_tiles + c)
        def _(r0=r0, r1=r1):
            n_blocks = aout_ref.shape[1] // LRU_BLOCK_W
            half, cols = xn_ref.shape[0] // 2, w_ref.shape[1] // (n_blocks // 2)

            def project(n):
                rows = slice(n % 2 * half, (n % 2 + 1) * half)
                sl = slice(n // 2 * cols, (n // 2 + 1) * cols)
                qkv_ref[rows, sl] = (_dot(xn_ref[rows, :], w_ref[:, sl]) * cs_ref[:, sl]).astype(BF16)

            def store(sl, v):
                aout_ref[r0:r1, sl] = v.astype(aout_ref.dtype)

            _rglru_rows(axy_ref[0, r0:r1, :], lambda sl: axy_ref[1, r0:r1, sl], store, project, cw_ref[...],
                        cb_ref[...], gw_ref, gb_ref, lam_ref[...], xc_ref, tail_ref, hc_ref)


def _in_proj(x, g, w, w_fl, col_scale, conv_w, conv_b, gate_w, gate_b, lam, layer, batch, n_axy, n_qkv,
             *, tm=1024, tn=1024):
    T, K = x.shape
    na, nq = n_axy // tn, n_qkv // tn
    assert na == 2, "one column tile per RG-LRU branch half"
    lru_w = n_axy // 2
    align = 2 * SUBLANES
    rows = (tm // nq) // align * align
    chunk_starts = (0,) + tuple(tm - rows * (nq - 1 - c) for c in range(nq))
    return pl.pallas_call(
        functools.partial(_in_proj_kernel, n_axy_tiles=na, chunk_starts=chunk_starts, tiles_per_seq=T // batch // tm),
        grid=(T // tm, na + nq),
        in_specs=[
            pl.BlockSpec((tm, K), lambda i, j: (i, 0)),
            pl.BlockSpec((1, K), lambda i, j: (0, 0)),
            pl.BlockSpec((None, K, tn), lambda i, j: (layer, 0, j)),
            pl.BlockSpec((None, K, LANES), lambda i, j: (layer, 0, 0)),
            pl.BlockSpec((1, tn), lambda i, j: (0, jnp.maximum(j - na, 0))),
            pl.BlockSpec((CONV_W, lru_w), lambda i, j: (0, 0)),
            pl.BlockSpec((1, lru_w), lambda i, j: (0, 0)),
            pl.BlockSpec(gate_w.shape, lambda i, j: (0, 0, 0, 0)),
            pl.BlockSpec((2, lru_w), lambda i, j: (0, 0)),
            pl.BlockSpec((1, lru_w), lambda i, j: (0, 0)),
        ],
        out_specs=[
            pl.BlockSpec((tm, tn), lambda i, j: (i, jnp.maximum(j - na, 0))),
            pl.BlockSpec((tm, LANES), lambda i, j: (i, 0)),
            pl.BlockSpec((tm, lru_w), lambda i, j: (i, 0)),
        ],
        out_shape=[
            jax.ShapeDtypeStruct((T, n_qkv), BF16),
            jax.ShapeDtypeStruct((T, LANES), F32),
            jax.ShapeDtypeStruct((T, lru_w), BF16),
        ],
        scratch_shapes=[
            pltpu.VMEM((tm, K), BF16),
            pltpu.VMEM((na, tm, tn), F32),
            pltpu.VMEM((chunk_starts[1], lru_w), F32),
            pltpu.VMEM((SUBLANES, lru_w), F32),
            pltpu.VMEM((1, lru_w), F32),
        ],
        compiler_params=_params("arbitrary", "arbitrary", vmem_limit_bytes=FFN_VMEM_LIMIT_BYTES),
        name="in_proj",
    )(x, g, w, w_fl, col_scale, conv_w, conv_b, gate_w, gate_b, lam)


def _softplus(x):
    return jnp.maximum(x, 0.0) + jnp.log1p(jnp.exp(-jnp.abs(x)))


def _rglru_rows(x, ay_block, store_block, before_block, cw, cb, gw_ref, gb_ref, lam, xc_ref, tail_ref, hc_ref):
    ts = x.shape[0]
    xc = cb + cw[CONV_W - 1:CONV_W] * x
    for d in range(1, CONV_W):
        xc = xc + cw[CONV_W - 1 - d:CONV_W - d] * pltpu.roll(x, d, axis=0)
    xc_ref[0:ts, :] = xc
    x8 = x[0:SUBLANES]
    t8 = tail_ref[...]
    row8 = lax.broadcasted_iota(jnp.int32, x8.shape, 0)
    xc8 = cb + cw[CONV_W - 1:CONV_W] * x8
    for d in range(1, CONV_W):
        sh = jnp.where(row8 < d, pltpu.roll(t8, d, axis=0), pltpu.roll(x8, d, axis=0))
        xc8 = xc8 + cw[CONV_W - 1 - d:CONV_W - d] * sh
    xc_ref[0:SUBLANES, :] = xc8
    tail_ref[...] = x[ts - SUBLANES:ts]

    sp = _softplus(-lam)
    groups = ts // SUBLANES
    row = lax.broadcasted_iota(jnp.int32, (groups, SUBLANES, LRU_BLOCK_W), 1)
    for n in range(x.shape[1] // LRU_BLOCK_W):
        before_block(n)
        sl = slice(n * LRU_BLOCK_W, (n + 1) * LRU_BLOCK_W)
        xcn = xc_ref[0:ts, sl]
        xb = xcn.astype(BF16)
        r = jax.nn.sigmoid(_dot(xb, gw_ref[0, n]) + gb_ref[0:1, sl])
        i = jax.nn.sigmoid(_dot(xb, gw_ref[1, n]) + gb_ref[1:2, sl])
        log_a = -RGLRU_C * r * sp[:, sl]
        a = jnp.exp(log_a)
        b = jnp.sqrt(-jnp.tanh(log_a) * (a * a + 1.0)) * (i * xcn)
        a = a.reshape(groups, SUBLANES, LRU_BLOCK_W)
        b = b.reshape(groups, SUBLANES, LRU_BLOCK_W)
        d = 1
        while d < SUBLANES:
            a_sh = jnp.where(row < d, 1.0, pltpu.roll(a, d, axis=1))
            b_sh = jnp.where(row < d, 0.0, pltpu.roll(b, d, axis=1))
            b = a * b_sh + b
            a = a * a_sh
            d *= 2
        carry = hc_ref[:, sl]
        hs = []
        for gi in range(groups):
            hs.append(b[gi] + a[gi] * carry)
            carry = hs[-1][SUBLANES - 1:SUBLANES]
        hc_ref[:, sl] = carry
        store_block(sl, jnp.concatenate(hs, axis=0) * jax.nn.gelu(ay_block(sl)))


def _fox_prep_kernel(fl_ref, fb_ref, colb_ref, row_ref):
    x = fl_ref[...] + fb_ref[...]
    c = jnp.minimum(x, 0.0) - jnp.log1p(jnp.exp(-jnp.abs(x)))
    S = c.shape[0]
    row = lax.broadcasted_iota(jnp.int32, c.shape, 0)
    d = 1
    while d < S:
        c = c + jnp.where(row < d, 0.0, pltpu.roll(c, d, axis=0))
        d *= 2
    c = c * LOG2E
    for h in range(FOX_HEADS):
        colb_ref[h] = jnp.broadcast_to(c[:, h:h + 1], c.shape)
    for k in range(S // LANES):
        row_ref[:, k * LANES:(k + 1) * LANES] = c[k * LANES:(k + 1) * LANES, :].T[0:SUBLANES]


def _fox_prep(fl, f_bias, batch):
    T = fl.shape[0]
    S = T // batch
    return pl.pallas_call(
        _fox_prep_kernel,
        grid=(batch,),
        in_specs=[pl.BlockSpec((S, LANES), lambda b: (b, 0)), pl.BlockSpec((1, LANES), lambda b: (0, 0))],
        out_specs=[pl.BlockSpec((None, FOX_HEADS, S, LANES), lambda b: (b, 0, 0, 0)),
                   pl.BlockSpec((None, SUBLANES, S), lambda b: (b, 0, 0))],
        out_shape=[jax.ShapeDtypeStruct((batch, FOX_HEADS, S, LANES), F32),
                   jax.ShapeDtypeStruct((batch, SUBLANES, S), F32)],
        compiler_params=_params("parallel"),
        name="fox_prep",
    )(fl, f_bias)


def _t5_bucket(rel):
    nb = REL_BUCKETS // 2
    max_exact = nb // 2
    n = jnp.abs(rel)
    large = max_exact + (jnp.log(jnp.maximum(n, max_exact).astype(jnp.float32) / max_exact)
                         / math.log(REL_MAX_DIST / max_exact) * (nb - max_exact)).astype(jnp.int32)
    large = jnp.minimum(large, nb - 1)
    return jnp.where(rel > 0, nb, 0) + jnp.where(n < max_exact, n, large)


def _bias_tile_types(tq, tk):
    return -(-(tk - 1 + REL_MAX_DIST) // tq) + 1


def _bucket_tiles(tq, tk):
    nt = _bias_tile_types(tq, tk)
    t = jnp.arange(nt, dtype=jnp.int32)[:, None, None]
    c = jnp.arange(tk, dtype=jnp.int32)[None, :, None]
    r = jnp.arange(tq, dtype=jnp.int32)[None, None, :] + t * tq
    allowed = (c // CHUNK) <= (r // CHUNK)
    return jnp.where(allowed, _t5_bucket(c - r), -1)


def _bias_table_kernel(rb_ref, bucket_ref, o_ref):
    h = pl.program_id(0)
    bucket = bucket_ref[...]
    out = jnp.full(bucket.shape, NEG_INF, F32)
    for b in range(REL_BUCKETS):
        out = jnp.where(bucket == b, rb_ref[b, h] * LOG2E, out)
    o_ref[...] = out


def _bias_table(rel_bias, tq, tk):
    buckets = _bucket_tiles(tq, tk)
    nt = buckets.shape[0]
    H = rel_bias.shape[1]
    return pl.pallas_call(
        _bias_table_kernel,
        grid=(H, nt),
        in_specs=[
            pl.BlockSpec(memory_space=pltpu.SMEM),
            pl.BlockSpec((None, tk, tq), lambda h, t: (t, 0, 0)),
        ],
        out_specs=pl.BlockSpec((None, None, tk, tq), lambda h, t: (h, t, 0, 0)),
        out_shape=jax.ShapeDtypeStruct((H, nt, tk, tq), F32),
        compiler_params=_params("parallel", "parallel"),
        name="bias_table",
    )(rel_bias, buckets)


ONES_ROWS = 16
ATTN_HEADS_PER_STEP = 2


def _paired_flash_loop(pair, n_tiles, logits, vt1, sa_ref, sb_ref, m_ref, acc_ref):
    heads = range(m_ref.shape[1])
    n_steps = n_tiles + 1

    def step(w):
        if isinstance(w, int) and w < 2:
            return w, (pair, n_tiles - 1 - pair)[w], True
        sel = (w - 2 >= pair).astype(jnp.int32)
        return sel, w - 2 - sel * pair, False

    def fill(s_ref, w):
        sel, tile, diagonal = step(w)
        for h in heads:
            s_ref[h] = logits(h, sel, tile, diagonal)

    def update(s_ref, w):
        sel, tile, _ = step(w)
        for h in heads:
            s, m = s_ref[h], m_ref[sel, h]
            m_new = jnp.maximum(m, jnp.max(s, axis=0, keepdims=True))
            p = jnp.exp2(s - m_new).astype(BF16)
            acc_ref[sel, h] = jnp.exp2(m - m_new) * acc_ref[sel, h] + _dot(vt1(h, tile), p)
            m_ref[sel, h] = m_new

    m_ref[...] = jnp.full(m_ref.shape, NEG_INF, F32)
    acc_ref[...] = jnp.zeros(acc_ref.shape, F32)
    fill(sa_ref, 0)
    fill(sb_ref, 1)
    update(sa_ref, 0)
    fill(sa_ref, 2)
    update(sb_ref, 1)

    def two_steps(i, _):
        w = 3 + 2 * i
        fill(sb_ref, w)
        update(sa_ref, w - 1)
        fill(sa_ref, w + 1)
        update(sb_ref, w)
        return 0

    lax.fori_loop(0, (n_steps - 3) // 2, two_steps, 0)
    update(sa_ref, n_steps - 1)


def _flash_scratch(n_heads, tk, n, dv):
    return [pltpu.VMEM((n_heads, tk, n), F32), pltpu.VMEM((n_heads, tk, n), F32),
            pltpu.VMEM((2, n_heads, 1, n), F32), pltpu.VMEM((2, n_heads, dv + ONES_ROWS, n), F32)]


def _store_transposed(vt_ref, v_ref):
    n_heads, n_tiles, rows, tk = vt_ref.shape
    dv = rows - ONES_ROWS
    eye = (lax.broadcasted_iota(jnp.int32, (dv, dv), 0) == lax.broadcasted_iota(jnp.int32, (dv, dv), 1)).astype(BF16)
    for h in range(n_heads):
        for j in range(n_tiles):
            vt_ref[h, j, :dv, :] = _dot_nt(eye, v_ref[j * tk:(j + 1) * tk, h * dv:(h + 1) * dv]).astype(BF16)
            vt_ref[h, j, dv:, :] = jnp.ones((ONES_ROWS, tk), BF16)


def _diff_attn_kernel(q_ref, k_ref, v_ref, bias_ref, dl_ref, sg_ref, o_ref, vt_ref, qq_ref, sa_ref, sb_ref, m_ref,
                      acc_ref, *, tq, nt, lam_init):
    pair = pl.program_id(2)
    n_tiles = k_ref.shape[0] // tq

    @pl.when(pair == 0)
    def _():
        _store_transposed(vt_ref, v_ref)

    nh = ATTN_HEADS_PER_STEP
    q0s = (pl.multiple_of(pair * tq, tq), pl.multiple_of((n_tiles - 1 - pair) * tq, tq))
    lane = lax.broadcasted_iota(jnp.int32, (tq, LANES), 1)
    for sel in range(2):
        for h in range(nh):
            q = q_ref[pl.ds(q0s[sel], tq), h * LANES:(h + 1) * LANES]
            qq_ref[sel, h, :tq, :] = jnp.where(lane < DIFF_DH, q, 0)
            qq_ref[sel, h, tq:, :] = jnp.where(lane >= DIFF_DH, q, 0)

    def logits(h, sel, tile, diagonal):
        del diagonal
        k0 = pl.multiple_of(tile * tq, tq)
        q0 = q0s[0] + sel * (q0s[1] - q0s[0])
        s = _dot_nt(k_ref[pl.ds(k0, tq), h * LANES:(h + 1) * LANES], qq_ref[sel, h])
        bias = bias_ref[h, jnp.minimum((q0 - k0) // tq, nt - 1)]
        return s + jnp.concatenate([bias, bias], axis=1)

    _paired_flash_loop(pair, n_tiles, logits, lambda h, tile: vt_ref[h, tile], sa_ref, sb_ref, m_ref, acc_ref)
    dl = dl_ref[...]
    lam = (jnp.exp(jnp.sum(dl[0:1] * dl[1:2], axis=-1, keepdims=True))
           - jnp.exp(jnp.sum(dl[2:3] * dl[3:4], axis=-1, keepdims=True)) + lam_init)
    for sel in range(2):
        for h in range(nh):
            o = acc_ref[sel, h, :DIFF_VD, :] / acc_ref[sel, h, DIFF_VD:DIFF_VD + 1, :]
            d = o[:, :tq] - lam * o[:, tq:]
            y = d * lax.rsqrt(jnp.mean(d * d, axis=0, keepdims=True) + NORM_EPS) * sg_ref[...]
            o_ref[pl.ds(q0s[sel], tq), h * DIFF_VD:(h + 1) * DIFF_VD] = (y * (1.0 - lam_init)).T.astype(o_ref.dtype)


def _diff_attn(qkv, bias_table, diff_lambda, subln_g, batch, lam_init, *, tq):
    T = qkv.shape[0]
    S = T // batch
    nq = S // tq
    nt = bias_table.shape[1]
    nh = ATTN_HEADS_PER_STEP
    G = DIFF_HEADS // nh
    W = nh * LANES
    return pl.pallas_call(
        functools.partial(_diff_attn_kernel, tq=tq, nt=nt, lam_init=lam_init),
        grid=(batch, G, nq // 2),
        in_specs=[
            pl.BlockSpec((S, W), lambda b, g, i: (b, g)),
            pl.BlockSpec((S, W), lambda b, g, i: (b, G + g)),
            pl.BlockSpec((S, W), lambda b, g, i: (b, 2 * G + g)),
            pl.BlockSpec((nh, nt, tq, tq), lambda b, g, i: (g, 0, 0, 0)),
            pl.BlockSpec(diff_lambda.shape, lambda b, g, i: (0, 0)),
            pl.BlockSpec((DIFF_VD, 1), lambda b, g, i: (0, 0)),
        ],
        out_specs=pl.BlockSpec((S, nh * DIFF_VD), lambda b, g, i: (b, g)),
        out_shape=jax.ShapeDtypeStruct((T, DIFF_HEADS * DIFF_VD), BF16),
        scratch_shapes=([pltpu.VMEM((nh, nq, DIFF_VD + ONES_ROWS, tq), BF16), pltpu.VMEM((2, nh, 2 * tq, LANES), BF16)]
                        + _flash_scratch(nh, tq, 2 * tq, DIFF_VD)),
        compiler_params=_params("parallel", "parallel", "arbitrary"),
        name="diff_attn",
    )(qkv, qkv, qkv, bias_table, diff_lambda, subln_g)


def _fox_attn_kernel(q_ref, k_ref, v_ref, ck_ref, cq_ref, o_ref, vt_ref, sa_ref, sb_ref, m_ref, acc_ref, *, tq):
    pair = pl.program_id(2)
    n_tiles = k_ref.shape[0] // tq

    @pl.when(pair == 0)
    def _():
        _store_transposed(vt_ref, v_ref)

    nh = ATTN_HEADS_PER_STEP
    h0 = pl.program_id(1) * nh
    q_tiles = (pair, n_tiles - 1 - pair)
    q0s = tuple(pl.multiple_of(t * tq, tq) for t in q_tiles)
    cqs = [[cq_ref[q_tiles[sel], pl.ds(h0 + h, 1), :] for h in range(nh)] for sel in range(2)]

    def logits(h, sel, tile, diagonal):
        k0 = pl.multiple_of(tile * tq, tq)
        if diagonal:
            q0, cq = q0s[sel], cqs[sel][h]
        else:
            q0 = q0s[0] + sel * (q0s[1] - q0s[0])
            cq = jnp.where(sel == 1, cqs[1][h], cqs[0][h])
        q = q_ref[pl.ds(q0, tq), h * LANES:(h + 1) * LANES]
        ck = ck_ref[h, pl.ds(k0, tq), :]
        s = _dot_nt(k_ref[pl.ds(k0, tq), h * LANES:(h + 1) * LANES], q)
        s = s + (cq - jnp.concatenate([ck] * (tq // LANES), axis=1))
        if diagonal:
            s = jnp.where(lax.broadcasted_iota(jnp.int32, s.shape, 0) <= lax.broadcasted_iota(jnp.int32, s.shape, 1),
                          s, NEG_INF)
        return s

    _paired_flash_loop(pair, n_tiles, logits, lambda h, tile: vt_ref[h, tile], sa_ref, sb_ref, m_ref, acc_ref)
    for sel in range(2):
        for h in range(nh):
            o = acc_ref[sel, h, :FOX_DH, :] / acc_ref[sel, h, FOX_DH:FOX_DH + 1, :]
            o_ref[pl.ds(q0s[sel], tq), h * FOX_DH:(h + 1) * FOX_DH] = o.T.astype(o_ref.dtype)


def _fox_attn(qkv, cum_colb, cum_row, batch, *, tq):
    T = qkv.shape[0]
    S = T // batch
    nq = S // tq
    nh = ATTN_HEADS_PER_STEP
    G = FOX_HEADS // nh
    W = nh * LANES
    base = 3 * DIFF_HEADS // nh
    return pl.pallas_call(
        functools.partial(_fox_attn_kernel, tq=tq),
        grid=(batch, G, nq // 2),
        in_specs=[
            pl.BlockSpec((S, W), lambda b, g, i: (b, base + g)),
            pl.BlockSpec((S, W), lambda b, g, i: (b, base + G + g)),
            pl.BlockSpec((S, W), lambda b, g, i: (b, base + 2 * G + g)),
            pl.BlockSpec((None, nh, S, LANES), lambda b, g, i: (b, g, 0, 0)),
            pl.BlockSpec((None, nq, SUBLANES, tq), lambda b, g, i: (b, 0, 0, 0)),
        ],
        out_specs=pl.BlockSpec((S, nh * FOX_DH), lambda b, g, i: (b, g)),
        out_shape=jax.ShapeDtypeStruct((T, FOX_HEADS * FOX_DH), BF16),
        scratch_shapes=([pltpu.VMEM((nh, nq, FOX_DH + ONES_ROWS, tq), BF16)] + _flash_scratch(nh, tq, tq, FOX_DH)),
        compiler_params=_params("parallel", "parallel", "arbitrary"),
        name="fox_attn",
    )(qkv, qkv, qkv, cum_colb, cum_row)


def _xattn_kernel(h_ref, gq_ref, go_ref, wq_ref, k_ref, v_ref, wo_ref, o_ref, *, n_heads):
    x = h_ref[...]
    dh = x.shape[1] // n_heads
    xn = _rms(x, gq_ref[...]).astype(BF16)
    q = (_dot(xn, wq_ref[...]) * (LOG2E * dh ** -0.5)).astype(BF16)
    outs = []
    for hd in range(n_heads):
        sl = slice(hd * dh, (hd + 1) * dh)
        s = _dot_nt(q[:, sl], k_ref[:, sl])
        p = jnp.exp2(s - jnp.max(s, axis=-1, keepdims=True))
        l = jnp.sum(p, axis=-1, keepdims=True)
        outs.append((_dot(p.astype(BF16), v_ref[:, sl]) / l).astype(BF16))
    y = _dot(jnp.concatenate(outs, axis=1), wo_ref[...])
    o_ref[...] = x + _rms(y, go_ref[...])


def _xattn(h, g_q, g_o, w_q, kv, w_o, layer, batch, *, tm=512):
    T, D = h.shape
    M = kv.shape[0] // batch
    ns = T // batch // tm
    resident = pl.Buffered(1)
    return pl.pallas_call(
        functools.partial(_xattn_kernel, n_heads=XATTN_HEADS),
        grid=(batch, ns),
        in_specs=[
            pl.BlockSpec((tm, D), lambda b, i: (b * ns + i, 0)),
            pl.BlockSpec((1, D), lambda b, i: (0, 0)),
            pl.BlockSpec((1, D), lambda b, i: (0, 0)),
            pl.BlockSpec((None, D, D), lambda b, i: (layer, 0, 0), pipeline_mode=resident),
            pl.BlockSpec((M, D), lambda b, i: (b, 0)),
            pl.BlockSpec((M, D), lambda b, i: (b, 1)),
            pl.BlockSpec((None, D, D), lambda b, i: (layer, 0, 0), pipeline_mode=resident),
        ],
        out_specs=pl.BlockSpec((tm, D), lambda b, i: (b * ns + i, 0)),
        out_shape=jax.ShapeDtypeStruct((T, D), F32),
        compiler_params=_params("parallel", "arbitrary"),
        name="xattn",
    )(h, g_q, g_o, w_q, kv, kv, w_o)


def _proj_norm_res_kernel(*refs, n_in):
    x_refs, w_refs = refs[:n_in], refs[n_in:2 * n_in]
    g_ref, h_ref, o_ref = refs[2 * n_in:]
    y = _dot(x_refs[0][...], w_refs[0][...])
    for x_ref, w_ref in zip(x_refs[1:], w_refs[1:]):
        y = y + _dot(x_ref[...], w_ref[...])
    o_ref[...] = h_ref[...] + _rms(y, g_ref[...])


def _proj_norm_res(xs, w, layer, g, h, *, tm=512):
    T, N = h.shape
    n_in = len(xs)
    x_specs, w_specs, row0 = [], [], 0
    for x in xs:
        k = x.shape[1]
        x_specs.append(pl.BlockSpec((tm, k), lambda i: (i, 0)))
        w_specs.append(pl.BlockSpec((None, k, N), functools.partial(lambda i, r: (layer, r, 0), r=row0 // k)))
        row0 += k
    return pl.pallas_call(
        functools.partial(_proj_norm_res_kernel, n_in=n_in),
        grid=(T // tm,),
        in_specs=x_specs + w_specs + [pl.BlockSpec((1, N), lambda i: (0, 0)), pl.BlockSpec((tm, N), lambda i: (i, 0))],
        out_specs=pl.BlockSpec((tm, N), lambda i: (i, 0)),
        out_shape=jax.ShapeDtypeStruct((T, N), F32),
        compiler_params=_params("parallel"),
        name="proj_norm_res",
    )(*xs, *([w] * n_in), g, h)


def kernel(x, mem, norm_g, ffn1_w_in, ffn1_w_out, w_in, conv_w, conv_b, lru_gate_w, lru_gate_b, lru_lambda,
           diff_lambda, diff_subln_g, fox_f_bias, rel_bias, w_out, xattn_w_q, xattn_w_kv, xattn_w_o,
           ffn2_w_in, ffn2_w_out):
    B, S, D = x.shape
    depth = norm_g.shape[0]
    T = B * S
    lru_w = conv_w.shape[-1]
    n_axy = 2 * lru_w
    n_qkv = 3 * DIFF_HEADS * DIFF_VD + 3 * FOX_HEADS * FOX_DH
    attn_tq = 512

    h = x.astype(F32).reshape(T, D)
    mem2 = mem.astype(F32).reshape(B * mem.shape[1], D)
    ones_g = jnp.ones((1, D), F32)

    bf = lambda a: a.astype(BF16)
    ffn1_w_in, ffn1_w_out, ffn2_w_in, ffn2_w_out = bf(ffn1_w_in), bf(ffn1_w_out), bf(ffn2_w_in), bf(ffn2_w_out)
    w_in_b, w_out_b, gate_w_b = bf(w_in), bf(w_out), bf(lru_gate_w)
    w_q_b, w_kv_b, w_o_b = bf(xattn_w_q), bf(xattn_w_kv), bf(xattn_w_o)
    w_fl = bf(jnp.pad(w_in[:, :, n_axy + n_qkv:], ((0, 0), (0, 0), (0, LANES - FOX_HEADS))))
    f_bias = jnp.pad(fox_f_bias, ((0, 0), (0, LANES - FOX_HEADS)))
    n_dq, n_fq = DIFF_HEADS * 2 * DIFF_DH, FOX_HEADS * FOX_DH
    fq0 = 3 * DIFF_HEADS * DIFF_VD
    col_scale = jnp.ones((1, n_qkv), F32)
    col_scale = col_scale.at[:, :n_dq].set(LOG2E * DIFF_DH ** -0.5).at[:, fq0:fq0 + n_fq].set(LOG2E * FOX_DH ** -0.5)

    bias_table = _bias_table(rel_bias, attn_tq, attn_tq)

    for l in range(depth):
        g = norm_g[l]
        lam_init = 0.8 - 0.6 * math.exp(-0.3 * l)
        h = _ffn(h, g[0:1], g[1:2], ffn1_w_in, ffn1_w_out, l)

        qkv, fl, a_out = _in_proj(h, g[2:3], w_in_b, w_fl, col_scale, conv_w[l], conv_b[l][None], gate_w_b[l],
                                  lru_gate_b[l], lru_lambda[l][None], l, B, n_axy, n_qkv)
        cum_colb, cum_row = _fox_prep(fl, f_bias[l][None], B)
        cum_row = cum_row.reshape(B, SUBLANES, S // attn_tq, attn_tq).transpose(0, 2, 1, 3)
        d_out = _diff_attn(qkv, bias_table, diff_lambda[l], diff_subln_g[l][:, None], B, lam_init, tq=attn_tq)
        f_out = _fox_attn(qkv, cum_colb, cum_row, B, tq=attn_tq)
        h = _proj_norm_res([a_out, d_out, f_out], w_out_b, l, g[3:4], h)

        kv = _norm_matmul(mem2, ones_g, w_kv_b, l, 2 * D, BF16, normalize=False)
        h = _xattn(h, g[4:5], g[5:6], w_q_b, kv, w_o_b, l, B)

        h = _ffn(h, g[6:7], g[7:8], ffn2_w_in, ffn2_w_out, l)
    return h.reshape(B, S, D).astype(x.dtype)
```

```python
import functools
import math

import jax
import jax.numpy as jnp
import numpy as np
from jax import lax
from jax.experimental import pallas as pl
from jax.experimental.pallas import tpu as pltpu

F32 = jnp.float32
BF16 = jnp.bfloat16

CHUNK = 64
CONV_W = 4
LRU_BLOCK_W = 128
RGLRU_C = 8.0
DIFF_HEADS = 4
DIFF_DH = 64
DIFF_VD = 128
FOX_HEADS = 4
FOX_DH = 128
XATTN_HEADS = 4
REL_BUCKETS = 32
REL_MAX_DIST = 128
NORM_EPS = 1e-6
NEG_INF = -1e30
FFN_RESIDUAL_WEIGHT = 0.5
LOG2E = math.log2(math.e)

LANES = 128
SUBLANES = 8
V7X_VMEM_BYTES = 64 * 1024 * 1024
VMEM_LIMIT_BYTES = (V7X_VMEM_BYTES * 3) // 4
FFN_VMEM_LIMIT_BYTES = (V7X_VMEM_BYTES * 7) // 8


def _params(*semantics, vmem_limit_bytes=VMEM_LIMIT_BYTES):
    return pltpu.CompilerParams(dimension_semantics=semantics, vmem_limit_bytes=vmem_limit_bytes)


def _rms(x, g):
    return x * lax.rsqrt(jnp.mean(x * x, axis=-1, keepdims=True) + NORM_EPS) * g


def _dot(a, b):
    return jnp.dot(a, b, preferred_element_type=F32)


def _dot_nt(a, b):
    return lax.dot_general(a, b, (((1,), (1,)), ((), ())), preferred_element_type=F32)


FFN_EDGE_ROWS = 512


def _ffn_kernel(x_ref, gpre_ref, gpost_ref, wg_ref, wu_ref, wo_ref, o_ref, xn_ref):
    j = pl.program_id(1)
    last = pl.num_programs(1) - 1
    edge_chunks = [slice(r, r + FFN_EDGE_ROWS) for r in range(0, x_ref.shape[0], FFN_EDGE_ROWS)]

    def hidden(xn):
        gate = _dot(xn, wg_ref[...])
        up = _dot(xn, wu_ref[...])
        act = (gate * jax.nn.sigmoid(gate) * up).astype(BF16)
        return _dot(act, wo_ref[...])

    @pl.when(j == 0)
    def _():
        g = gpre_ref[...]
        for rows in edge_chunks:
            xn = _rms(x_ref[rows, :], g).astype(BF16)
            xn_ref[rows, :] = xn
            o_ref[rows, :] = hidden(xn)

    @pl.when(jnp.logical_and(j > 0, j < last))
    def _():
        o_ref[...] += hidden(xn_ref[...])

    @pl.when(j == last)
    def _():
        g = FFN_RESIDUAL_WEIGHT * gpost_ref[...]
        for rows in edge_chunks:
            y = o_ref[rows, :] + hidden(xn_ref[rows, :])
            o_ref[rows, :] = x_ref[rows, :] + _rms(y, g)


def _ffn(h, g_pre, g_post, w_in, w_out, layer, *, tm=1024, tf=256):
    T, D = h.shape
    F = w_out.shape[1]
    nf = F // tf
    return pl.pallas_call(
        _ffn_kernel,
        grid=(T // tm, nf),
        in_specs=[
            pl.BlockSpec((tm, D), lambda i, j: (i, 0)),
            pl.BlockSpec((1, D), lambda i, j: (0, 0)),
            pl.BlockSpec((1, D), lambda i, j: (0, 0)),
            pl.BlockSpec((None, D, tf), lambda i, j: (layer, 0, j)),
            pl.BlockSpec((None, D, tf), lambda i, j: (layer, 0, j + nf)),
            pl.BlockSpec((None, tf, D), lambda i, j: (layer, j, 0)),
        ],
        out_specs=pl.BlockSpec((tm, D), lambda i, j: (i, 0)),
        out_shape=jax.ShapeDtypeStruct((T, D), F32),
        scratch_shapes=[pltpu.VMEM((tm, D), BF16)],
        compiler_params=_params("parallel", "arbitrary", vmem_limit_bytes=FFN_VMEM_LIMIT_BYTES),
        name="ffn",
    )(h, g_pre, g_post, w_in, w_in, w_out)


def _norm_matmul_kernel(x_ref, g_ref, w_ref, o_ref, xn_ref, *, normalize):
    @pl.when(pl.program_id(1) == 0)
    def _():
        x = x_ref[...]
        if normalize:
            x = _rms(x, g_ref[...])
        xn_ref[...] = x.astype(BF16)

    o_ref[...] = _dot(xn_ref[...], w_ref[...]).astype(o_ref.dtype)


def _norm_matmul(x, g, w, layer, n_cols, out_dtype, *, normalize=True, tm=512, tn=1024):
    T, K = x.shape
    tn = min(tn, n_cols)
    return pl.pallas_call(
        functools.partial(_norm_matmul_kernel, normalize=normalize),
        grid=(T // tm, n_cols // tn),
        in_specs=[
            pl.BlockSpec((tm, K), lambda i, j: (i, 0)),
            pl.BlockSpec((1, K), lambda i, j: (0, 0)),
            pl.BlockSpec((None, K, tn), lambda i, j: (layer, 0, j)),
        ],
        out_specs=pl.BlockSpec((tm, tn), lambda i, j: (i, j)),
        out_shape=jax.ShapeDtypeStruct((T, n_cols), out_dtype),
        scratch_shapes=[pltpu.VMEM((tm, K), BF16)],
        compiler_params=_params("parallel", "arbitrary"),
        name="norm_matmul",
    )(x, g, w)


def _in_proj_kernel(x_ref, g_ref, w_ref, wfl_ref, cs_ref, cw_ref, cb_ref, gw_ref, gb_ref, lam_ref,
                    qkv_ref, fl_ref, aout_ref, xn_ref, axy_ref, xc_ref, tail_ref, hc_ref,
                    *, n_axy_tiles, chunk_starts, tiles_per_seq):
    i, j = pl.program_id(0), pl.program_id(1)

    @pl.when(j == 0)
    def _():
        g = g_ref[...]
        half = x_ref.shape[0] // 2
        for rows in (slice(0, half), slice(half, 2 * half)):
            xn = _rms(x_ref[rows, :], g).astype(BF16)
            xn_ref[rows, :] = xn
            fl_ref[rows, :] = _dot(xn, wfl_ref[...])
            axy_ref[0, rows, :] = _dot(xn, w_ref[...])

    @pl.when(jnp.logical_and(j == 0, i % tiles_per_seq == 0))
    def _():
        tail_ref[...] = jnp.zeros_like(tail_ref)
        hc_ref[...] = jnp.zeros_like(hc_ref)

    @pl.when(jnp.logical_and(j > 0, j < n_axy_tiles))
    def _():
        axy_ref[j] = _dot(xn_ref[...], w_ref[...])

    for c, (r0, r1) in enumerate(zip(chunk_starts[:-1], chunk_starts[1:])):
        @pl.when(j == n_axy_tiles + c)
        def _(r0=r0, r1=r1):
            n_blocks = aout_ref.shape[1] // LRU_BLOCK_W
            half, cols = xn_ref.shape[0] // 2, w_ref.shape[1] // (n_blocks // 2)

            def project(n):
                rows = slice(n % 2 * half, (n % 2 + 1) * half)
                sl = slice(n // 2 * cols, (n // 2 + 1) * cols)
                qkv_ref[rows, sl] = (_dot(xn_ref[rows, :], w_ref[:, sl]) * cs_ref[:, sl]).astype(BF16)

            def store(sl, v):
                aout_ref[r0:r1, sl] = v.astype(aout_ref.dtype)

            _rglru_rows(axy_ref[0, r0:r1, :], lambda sl: axy_ref[1, r0:r1, sl], store, project, cw_ref[...],
                        cb_ref[...], gw_ref, gb_ref, lam_ref[...], xc_ref, tail_ref, hc_ref)


def _in_proj(x, g, w, w_fl, col_scale, conv_w, conv_b, gate_w, gate_b, lam, layer, batch, n_axy, n_qkv,
             *, tm=1024, tn=1024):
    T, K = x.shape
    na, nq = n_axy // tn, n_qkv // tn
    assert na == 2, "one column tile per RG-LRU branch half"
    lru_w = n_axy // 2
    align = 2 * SUBLANES
    rows = (tm // nq) // align * align
    chunk_starts = (0,) + tuple(tm - rows * (nq - 1 - c) for c in range(nq))
    return pl.pallas_call(
        functools.partial(_in_proj_kernel, n_axy_tiles=na, chunk_starts=chunk_starts, tiles_per_seq=T // batch // tm),
        grid=(T // tm, na + nq),
        in_specs=[
            pl.BlockSpec((tm, K), lambda i, j: (i, 0)),
            pl.BlockSpec((1, K), lambda i, j: (0, 0)),
            pl.BlockSpec((None, K, tn), lambda i, j: (layer, 0, j)),
            pl.BlockSpec((None, K, LANES), lambda i, j: (layer, 0, 0)),
            pl.BlockSpec((1, tn), lambda i, j: (0, jnp.maximum(j - na, 0))),
            pl.BlockSpec((CONV_W, lru_w), lambda i, j: (0, 0)),
            pl.BlockSpec((1, lru_w), lambda i, j: (0, 0)),
            pl.BlockSpec(gate_w.shape, lambda i, j: (0, 0, 0, 0)),
            pl.BlockSpec((2, lru_w), lambda i, j: (0, 0)),
            pl.BlockSpec((1, lru_w), lambda i, j: (0, 0)),
        ],
        out_specs=[
            pl.BlockSpec((tm, tn), lambda i, j: (i, jnp.maximum(j - na, 0))),
            pl.BlockSpec((tm, LANES), lambda i, j: (i, 0)),
            pl.BlockSpec((tm, lru_w), lambda i, j: (i, 0)),
        ],
        out_shape=[
            jax.ShapeDtypeStruct((T, n_qkv), BF16),
            jax.ShapeDtypeStruct((T, LANES), F32),
            jax.ShapeDtypeStruct((T, lru_w), BF16),
        ],
        scratch_shapes=[
            pltpu.VMEM((tm, K), BF16),
            pltpu.VMEM((na, tm, tn), F32),
            pltpu.VMEM((chunk_starts[1], lru_w), F32),
            pltpu.VMEM((SUBLANES, lru_w), F32),
            pltpu.VMEM((1, lru_w), F32),
        ],
        compiler_params=_params("arbitrary", "arbitrary", vmem_limit_bytes=FFN_VMEM_LIMIT_BYTES),
        name="in_proj",
    )(x, g, w, w_fl, col_scale, conv_w, conv_b, gate_w, gate_b, lam)


def _softplus(x):
    return jnp.maximum(x, 0.0) + jnp.log1p(jnp.exp(-jnp.abs(x)))


def _rglru_rows(x, ay_block, store_block, before_block, cw, cb, gw_ref, gb_ref, lam, xc_ref, tail_ref, hc_ref):
    ts = x.shape[0]
    xc = cb + cw[CONV_W - 1:CONV_W] * x
    for d in range(1, CONV_W):
        xc = xc + cw[CONV_W - 1 - d:CONV_W - d] * pltpu.roll(x, d, axis=0)
    xc_ref[0:ts, :] = xc
    x8 = x[0:SUBLANES]
    t8 = tail_ref[...]
    row8 = lax.broadcasted_iota(jnp.int32, x8.shape, 0)
    xc8 = cb + cw[CONV_W - 1:CONV_W] * x8
    for d in range(1, CONV_W):
        sh = jnp.where(row8 < d, pltpu.roll(t8, d, axis=0), pltpu.roll(x8, d, axis=0))
        xc8 = xc8 + cw[CONV_W - 1 - d:CONV_W - d] * sh
    xc_ref[0:SUBLANES, :] = xc8
    tail_ref[...] = x[ts - SUBLANES:ts]

    sp = _softplus(-lam)
    groups = ts // SUBLANES
    row = lax.broadcasted_iota(jnp.int32, (groups, SUBLANES, LRU_BLOCK_W), 1)
    for n in range(x.shape[1] // LRU_BLOCK_W):
        before_block(n)
        sl = slice(n * LRU_BLOCK_W, (n + 1) * LRU_BLOCK_W)
        xcn = xc_ref[0:ts, sl]
        xb = xcn.astype(BF16)
        r = jax.nn.sigmoid(_dot(xb, gw_ref[0, n]) + gb_ref[0:1, sl])
        i = jax.nn.sigmoid(_dot(xb, gw_ref[1, n]) + gb_ref[1:2, sl])
        log_a = -RGLRU_C * r * sp[:, sl]
        a = jnp.exp(log_a)
        b = jnp.sqrt(-jnp.tanh(log_a) * (a * a + 1.0)) * (i * xcn)
        a = a.reshape(groups, SUBLANES, LRU_BLOCK_W)
        b = b.reshape(groups, SUBLANES, LRU_BLOCK_W)
        d = 1
        while d < SUBLANES:
            a_sh = jnp.where(row < d, 1.0, pltpu.roll(a, d, axis=1))
            b_sh = jnp.where(row < d, 0.0, pltpu.roll(b, d, axis=1))
            b = a * b_sh + b
            a = a * a_sh
            d *= 2
        carry = hc_ref[:, sl]
        hs = []
        for gi in range(groups):
            hs.append(b[gi] + a[gi] * carry)
            carry = hs[-1][SUBLANES - 1:SUBLANES]
        hc_ref[:, sl] = carry
        store_block(sl, jnp.concatenate(hs, axis=0) * jax.nn.gelu(ay_block(sl)))


def _fox_prep_kernel(fl_ref, fb_ref, colb_ref, row_ref):
    x = fl_ref[...] + fb_ref[...]
    c = jnp.minimum(x, 0.0) - jnp.log1p(jnp.exp(-jnp.abs(x)))
    S = c.shape[0]
    row = lax.broadcasted_iota(jnp.int32, c.shape, 0)
    d = 1
    while d < S:
        c = c + jnp.where(row < d, 0.0, pltpu.roll(c, d, axis=0))
        d *= 2
    c = c * LOG2E
    for h in range(FOX_HEADS):
        colb_ref[h] = jnp.broadcast_to(c[:, h:h + 1], c.shape)
    for k in range(S // LANES):
        row_ref[:, k * LANES:(k + 1) * LANES] = c[k * LANES:(k + 1) * LANES, :].T[0:SUBLANES]


def _fox_prep(fl, f_bias, batch):
    T = fl.shape[0]
    S = T // batch
    return pl.pallas_call(
        _fox_prep_kernel,
        grid=(batch,),
        in_specs=[pl.BlockSpec((S, LANES), lambda b: (b, 0)), pl.BlockSpec((1, LANES), lambda b: (0, 0))],
        out_specs=[pl.BlockSpec((None, FOX_HEADS, S, LANES), lambda b: (b, 0, 0, 0)),
                   pl.BlockSpec((None, SUBLANES, S), lambda b: (b, 0, 0))],
        out_shape=[jax.ShapeDtypeStruct((batch, FOX_HEADS, S, LANES), F32),
                   jax.ShapeDtypeStruct((batch, SUBLANES, S), F32)],
        compiler_params=_params("parallel"),
        name="fox_prep",
    )(fl, f_bias)


def _t5_bucket(rel):
    nb = REL_BUCKETS // 2
    max_exact = nb // 2
    n = jnp.abs(rel)
    large = max_exact + (jnp.log(jnp.maximum(n, max_exact).astype(jnp.float32) / max_exact)
                         / math.log(REL_MAX_DIST / max_exact) * (nb - max_exact)).astype(jnp.int32)
    large = jnp.minimum(large, nb - 1)
    return jnp.where(rel > 0, nb, 0) + jnp.where(n < max_exact, n, large)


def _bias_tile_types(tq, tk):
    return -(-(tk - 1 + REL_MAX_DIST) // tq) + 1


def _bucket_tiles(tq, tk):
    nt = _bias_tile_types(tq, tk)
    t = jnp.arange(nt, dtype=jnp.int32)[:, None, None]
    c = jnp.arange(tk, dtype=jnp.int32)[None, :, None]
    r = jnp.arange(tq, dtype=jnp.int32)[None, None, :] + t * tq
    allowed = (c // CHUNK) <= (r // CHUNK)
    return jnp.where(allowed, _t5_bucket(c - r), -1)


def _bias_table_kernel(rb_ref, bucket_ref, o_ref):
    h = pl.program_id(0)
    bucket = bucket_ref[...]
    out = jnp.full(bucket.shape, NEG_INF, F32)
    for b in range(REL_BUCKETS):
        out = jnp.where(bucket == b, rb_ref[b, h] * LOG2E, out)
    o_ref[...] = out


def _bias_table(rel_bias, tq, tk):
    buckets = _bucket_tiles(tq, tk)
    nt = buckets.shape[0]
    H = rel_bias.shape[1]
    return pl.pallas_call(
        _bias_table_kernel,
        grid=(H, nt),
        in_specs=[
            pl.BlockSpec(memory_space=pltpu.SMEM),
            pl.BlockSpec((None, tk, tq), lambda h, t: (t, 0, 0)),
        ],
        out_specs=pl.BlockSpec((None, None, tk, tq), lambda h, t: (h, t, 0, 0)),
        out_shape=jax.ShapeDtypeStruct((H, nt, tk, tq), F32),
        compiler_params=_params("parallel", "parallel"),
        name="bias_table",
    )(rel_bias, buckets)


ONES_ROWS = 16
ATTN_HEADS_PER_STEP = 2


def _paired_flash_loop(pair, n_tiles, logits, vt1, sa_ref, sb_ref, m_ref, acc_ref):
    heads = range(m_ref.shape[1])
    n_steps = n_tiles + 1

    def step(w):
        if isinstance(w, int) and w < 2:
            return w, (pair, n_tiles - 1 - pair)[w], True
        sel = (w - 2 >= pair).astype(jnp.int32)
        return sel, w - 2 - sel * pair, False

    def fill(s_ref, w):
        sel, tile, diagonal = step(w)
        for h in heads:
            s_ref[h] = logits(h, sel, tile, diagonal)

    def update(s_ref, w):
        sel, tile, _ = step(w)
        for h in heads:
            s, m = s_ref[h], m_ref[sel, h]
            m_new = jnp.maximum(m, jnp.max(s, axis=0, keepdims=True))
            p = jnp.exp2(s - m_new).astype(BF16)
            acc_ref[sel, h] = jnp.exp2(m - m_new) * acc_ref[sel, h] + _dot(vt1(h, tile), p)
            m_ref[sel, h] = m_new

    m_ref[...] = jnp.full(m_ref.shape, NEG_INF, F32)
    acc_ref[...] = jnp.zeros(acc_ref.shape, F32)
    fill(sa_ref, 0)
    fill(sb_ref, 1)
    update(sa_ref, 0)
    fill(sa_ref, 2)
    update(sb_ref, 1)

    def two_steps(i, _):
        w = 3 + 2 * i
        fill(sb_ref, w)
        update(sa_ref, w - 1)
        fill(sa_ref, w + 1)
        update(sb_ref, w)
        return 0

    lax.fori_loop(0, (n_steps - 3) // 2, two_steps, 0)
    update(sa_ref, n_steps - 1)


def _flash_scratch(n_heads, tk, n, dv):
    return [pltpu.VMEM((n_heads, tk, n), F32), pltpu.VMEM((n_heads, tk, n), F32),
            pltpu.VMEM((2, n_heads, 1, n), F32), pltpu.VMEM((2, n_heads, dv + ONES_ROWS, n), F32)]


def _store_transposed(vt_ref, v_ref):
    n_heads, n_tiles, rows, tk = vt_ref.shape
    dv = rows - ONES_ROWS
    eye = (lax.broadcasted_iota(jnp.int32, (dv, dv), 0) == lax.broadcasted_iota(jnp.int32, (dv, dv), 1)).astype(BF16)
    for h in range(n_heads):
        for j in range(n_tiles):
            vt_ref[h, j, :dv, :] = _dot_nt(eye, v_ref[j * tk:(j + 1) * tk, h * dv:(h + 1) * dv]).astype(BF16)
            vt_ref[h, j, dv:, :] = jnp.ones((ONES_ROWS, tk), BF16)


def _diff_attn_kernel(q_ref, k_ref, v_ref, bias_ref, dl_ref, sg_ref, o_ref, vt_ref, qq_ref, sa_ref, sb_ref, m_ref,
                      acc_ref, *, tq, nt, lam_init):
    pair = pl.program_id(2)
    n_tiles = k_ref.shape[0] // tq

    @pl.when(pair == 0)
    def _():
        _store_transposed(vt_ref, v_ref)

    nh = ATTN_HEADS_PER_STEP
    q0s = (pl.multiple_of(pair * tq, tq), pl.multiple_of((n_tiles - 1 - pair) * tq, tq))
    lane = lax.broadcasted_iota(jnp.int32, (tq, LANES), 1)
    for sel in range(2):
        for h in range(nh):
            q = q_ref[pl.ds(q0s[sel], tq), h * LANES:(h + 1) * LANES]
            qq_ref[sel, h, :tq, :] = jnp.where(lane < DIFF_DH, q, 0)
            qq_ref[sel, h, tq:, :] = jnp.where(lane >= DIFF_DH, q, 0)

    def logits(h, sel, tile, diagonal):
        del diagonal
        k0 = pl.multiple_of(tile * tq, tq)
        q0 = q0s[0] + sel * (q0s[1] - q0s[0])
        s = _dot_nt(k_ref[pl.ds(k0, tq), h * LANES:(h + 1) * LANES], qq_ref[sel, h])
        bias = bias_ref[h, jnp.minimum((q0 - k0) // tq, nt - 1)]
        return s + jnp.concatenate([bias, bias], axis=1)

    _paired_flash_loop(pair, n_tiles, logits, lambda h, tile: vt_ref[h, tile], sa_ref, sb_ref, m_ref, acc_ref)
    dl = dl_ref[...]
    lam = (jnp.exp(jnp.sum(dl[0:1] * dl[1:2], axis=-1, keepdims=True))
           - jnp.exp(jnp.sum(dl[2:3] * dl[3:4], axis=-1, keepdims=True)) + lam_init)
    for sel in range(2):
        for h in range(nh):
            o = acc_ref[sel, h, :DIFF_VD, :] / acc_ref[sel, h, DIFF_VD:DIFF_VD + 1, :]
            d = o[:, :tq] - lam * o[:, tq:]
            y = d * lax.rsqrt(jnp.mean(d * d, axis=0, keepdims=True) + NORM_EPS) * sg_ref[...]
            o_ref[pl.ds(q0s[sel], tq), h * DIFF_VD:(h + 1) * DIFF_VD] = (y * (1.0 - lam_init)).T.astype(o_ref.dtype)


def _diff_attn(qkv, bias_table, diff_lambda, subln_g, batch, lam_init, *, tq):
    T = qkv.shape[0]
    S = T // batch
    nq = S // tq
    nt = bias_table.shape[1]
    nh = ATTN_HEADS_PER_STEP
    G = DIFF_HEADS // nh
    W = nh * LANES
    return pl.pallas_call(
        functools.partial(_diff_attn_kernel, tq=tq, nt=nt, lam_init=lam_init),
        grid=(batch, G, nq // 2),
        in_specs=[
            pl.BlockSpec((S, W), lambda b, g, i: (b, g)),
            pl.BlockSpec((S, W), lambda b, g, i: (b, G + g)),
            pl.BlockSpec((S, W), lambda b, g, i: (b, 2 * G + g)),
            pl.BlockSpec((nh, nt, tq, tq), lambda b, g, i: (g, 0, 0, 0)),
            pl.BlockSpec(diff_lambda.shape, lambda b, g, i: (0, 0)),
            pl.BlockSpec((DIFF_VD, 1), lambda b, g, i: (0, 0)),
        ],
        out_specs=pl.BlockSpec((S, nh * DIFF_VD), lambda b, g, i: (b, g)),
        out_shape=jax.ShapeDtypeStruct((T, DIFF_HEADS * DIFF_VD), BF16),
        scratch_shapes=([pltpu.VMEM((nh, nq, DIFF_VD + ONES_ROWS, tq), BF16), pltpu.VMEM((2, nh, 2 * tq, LANES), BF16)]
                        + _flash_scratch(nh, tq, 2 * tq, DIFF_VD)),
        compiler_params=_params("parallel", "parallel", "arbitrary"),
        name="diff_attn",
    )(qkv, qkv, qkv, bias_table, diff_lambda, subln_g)


def _fox_attn_kernel(q_ref, k_ref, v_ref, ck_ref, cq_ref, o_ref, vt_ref, sa_ref, sb_ref, m_ref, acc_ref, *, tq):
    pair = pl.program_id(2)
    n_tiles = k_ref.shape[0] // tq

    @pl.when(pair == 0)
    def _():
        _store_transposed(vt_ref, v_ref)

    nh = ATTN_HEADS_PER_STEP
    h0 = pl.program_id(1) * nh
    q_tiles = (pair, n_tiles - 1 - pair)
    q0s = tuple(pl.multiple_of(t * tq, tq) for t in q_tiles)
    cqs = [[cq_ref[q_tiles[sel], pl.ds(h0 + h, 1), :] for h in range(nh)] for sel in range(2)]

    def logits(h, sel, tile, diagonal):
        k0 = pl.multiple_of(tile * tq, tq)
        if diagonal:
            q0, cq = q0s[sel], cqs[sel][h]
        else:
            q0 = q0s[0] + sel * (q0s[1] - q0s[0])
            cq = jnp.where(sel == 1, cqs[1][h], cqs[0][h])
        q = q_ref[pl.ds(q0, tq), h * LANES:(h + 1) * LANES]
        ck = ck_ref[h, pl.ds(k0, tq), :]
        s = _dot_nt(k_ref[pl.ds(k0, tq), h * LANES:(h + 1) * LANES], q)
        s = s + (cq - jnp.concatenate([ck] * (tq // LANES), axis=1))
        if diagonal:
            s = jnp.where(lax.broadcasted_iota(jnp.int32, s.shape, 0) <= lax.broadcasted_iota(jnp.int32, s.shape, 1),
                          s, NEG_INF)
        return s

    _paired_flash_loop(pair, n_tiles, logits, lambda h, tile: vt_ref[h, tile], sa_ref, sb_ref, m_ref, acc_ref)
    for sel in range(2):
        for h in range(nh):
            o = acc_ref[sel, h, :FOX_DH, :] / acc_ref[sel, h, FOX_DH:FOX_DH + 1, :]
            o_ref[pl.ds(q0s[sel], tq), h * FOX_DH:(h + 1) * FOX_DH] = o.T.astype(o_ref.dtype)


def _fox_attn(qkv, cum_colb, cum_row, batch, *, tq):
    T = qkv.shape[0]
    S = T // batch
    nq = S // tq
    nh = ATTN_HEADS_PER_STEP
    G = FOX_HEADS // nh
    W = nh * LANES
    base = 3 * DIFF_HEADS // nh
    return pl.pallas_call(
        functools.partial(_fox_attn_kernel, tq=tq),
        grid=(batch, G, nq // 2),
        in_specs=[
            pl.BlockSpec((S, W), lambda b, g, i: (b, base + g)),
            pl.BlockSpec((S, W), lambda b, g, i: (b, base + G + g)),
            pl.BlockSpec((S, W), lambda b, g, i: (b, base + 2 * G + g)),
            pl.BlockSpec((None, nh, S, LANES), lambda b, g, i: (b, g, 0, 0)),
            pl.BlockSpec((None, nq, SUBLANES, tq), lambda b, g, i: (b, 0, 0, 0)),
        ],
        out_specs=pl.BlockSpec((S, nh * FOX_DH), lambda b, g, i: (b, g)),
        out_shape=jax.ShapeDtypeStruct((T, FOX_HEADS * FOX_DH), BF16),
        scratch_shapes=([pltpu.VMEM((nh, nq, FOX_DH + ONES_ROWS, tq), BF16)] + _flash_scratch(nh, tq, tq, FOX_DH)),
        compiler_params=_params("parallel", "parallel", "arbitrary"),
        name="fox_attn",
    )(qkv, qkv, qkv, cum_colb, cum_row)


def _xattn_kernel(h_ref, gq_ref, go_ref, wq_ref, k_ref, v_ref, wo_ref, o_ref, *, n_heads):
    x = h_ref[...]
    dh = x.shape[1] // n_heads
    xn = _rms(x, gq_ref[...]).astype(BF16)
    q = (_dot(xn, wq_ref[...]) * (LOG2E * dh ** -0.5)).astype(BF16)
    outs = []
    for hd in range(n_heads):
        sl = slice(hd * dh, (hd + 1) * dh)
        s = _dot_nt(q[:, sl], k_ref[:, sl])
        p = jnp.exp2(s - jnp.max(s, axis=-1, keepdims=True))
        l = jnp.sum(p, axis=-1, keepdims=True)
        outs.append((_dot(p.astype(BF16), v_ref[:, sl]) / l).astype(BF16))
    y = _dot(jnp.concatenate(outs, axis=1), wo_ref[...])
    o_ref[...] = x + _rms(y, go_ref[...])


def _xattn(h, g_q, g_o, w_q, kv, w_o, layer, batch, *, tm=512):
    T, D = h.shape
    M = kv.shape[0] // batch
    ns = T // batch // tm
    resident = pl.Buffered(1)
    return pl.pallas_call(
        functools.partial(_xattn_kernel, n_heads=XATTN_HEADS),
        grid=(batch, ns),
        in_specs=[
            pl.BlockSpec((tm, D), lambda b, i: (b * ns + i, 0)),
            pl.BlockSpec((1, D), lambda b, i: (0, 0)),
            pl.BlockSpec((1, D), lambda b, i: (0, 0)),
            pl.BlockSpec((None, D, D), lambda b, i: (layer, 0, 0), pipeline_mode=resident),
            pl.BlockSpec((M, D), lambda b, i: (b, 0)),
            pl.BlockSpec((M, D), lambda b, i: (b, 1)),
            pl.BlockSpec((None, D, D), lambda b, i: (layer, 0, 0), pipeline_mode=resident),
        ],
        out_specs=pl.BlockSpec((tm, D), lambda b, i: (b * ns + i, 0)),
        out_shape=jax.ShapeDtypeStruct((T, D), F32),
        compiler_params=_params("parallel", "arbitrary"),
        name="xattn",
    )(h, g_q, g_o, w_q, kv, kv, w_o)


def _proj_norm_res_kernel(*refs, n_in):
    x_refs, w_refs = refs[:n_in], refs[n_in:2 * n_in]
    g_ref, h_ref, o_ref = refs[2 * n_in:]
    y = _dot(x_refs[0][...], w_refs[0][...])
    for x_ref, w_ref in zip(x_refs[1:], w_refs[1:]):
        y = y + _dot(x_ref[...], w_ref[...])
    o_ref[...] = h_ref[...] + _rms(y, g_ref[...])


def _proj_norm_res(xs, w, layer, g, h, *, tm=512):
    T, N = h.shape
    n_in = len(xs)
    x_specs, w_specs, row0 = [], [], 0
    for x in xs:
        k = x.shape[1]
        x_specs.append(pl.BlockSpec((tm, k), lambda i: (i, 0)))
        w_specs.append(pl.BlockSpec((None, k, N), functools.partial(lambda i, r: (layer, r, 0), r=row0 // k)))
        row0 += k
    return pl.pallas_call(
        functools.partial(_proj_norm_res_kernel, n_in=n_in),
        grid=(T // tm,),
        in_specs=x_specs + w_specs + [pl.BlockSpec((1, N), lambda i: (0, 0)), pl.BlockSpec((tm, N), lambda i: (i, 0))],
        out_specs=pl.BlockSpec((tm, N), lambda i: (i, 0)),
        out_shape=jax.ShapeDtypeStruct((T, N), F32),
        compiler_params=_params("parallel"),
        name="proj_norm_res",
    )(*xs, *([w] * n_in), g, h)


def kernel(x, mem, norm_g, ffn1_w_in, ffn1_w_out, w_in, conv_w, conv_b, lru_gate_w, lru_gate_b, lru_lambda,
           diff_lambda, diff_subln_g, fox_f_bias, rel_bias, w_out, xattn_w_q, xattn_w_kv, xattn_w_o,
           ffn2_w_in, ffn2_w_out):
    B, S, D = x.shape
    depth = norm_g.shape[0]
    T = B * S
    lru_w = conv_w.shape[-1]
    n_axy = 2 * lru_w
    n_qkv = 3 * DIFF_HEADS * DIFF_VD + 3 * FOX_HEADS * FOX_DH
    attn_tq = 512

    h = x.astype(F32).reshape(T, D)
    mem2 = mem.astype(F32).reshape(B * mem.shape[1], D)
    ones_g = jnp.ones((1, D), F32)

    bf = lambda a: a.astype(BF16)
    ffn1_w_in, ffn1_w_out, ffn2_w_in, ffn2_w_out = bf(ffn1_w_in), bf(ffn1_w_out), bf(ffn2_w_in), bf(ffn2_w_out)
    w_in_b, w_out_b, gate_w_b = bf(w_in), bf(w_out), bf(lru_gate_w)
    w_q_b, w_kv_b, w_o_b = bf(xattn_w_q), bf(xattn_w_kv), bf(xattn_w_o)
    w_fl = jnp.pad(w_in_b[:, :, n_axy + n_qkv:], ((0, 0), (0, 0), (0, LANES - FOX_HEADS)))
    f_bias = jnp.pad(fox_f_bias, ((0, 0), (0, LANES - FOX_HEADS)))
    n_dq, n_fq = DIFF_HEADS * 2 * DIFF_DH, FOX_HEADS * FOX_DH
    fq0 = 3 * DIFF_HEADS * DIFF_VD
    col_scale = jnp.ones((1, n_qkv), F32)
    col_scale = col_scale.at[:, :n_dq].set(LOG2E * DIFF_DH ** -0.5).at[:, fq0:fq0 + n_fq].set(LOG2E * FOX_DH ** -0.5)

    bias_table = _bias_table(rel_bias, attn_tq, attn_tq)

    for l in range(depth):
        g = norm_g[l]
        lam_init = 0.8 - 0.6 * math.exp(-0.3 * l)
        h = _ffn(h, g[0:1], g[1:2], ffn1_w_in, ffn1_w_out, l)

        qkv, fl, a_out = _in_proj(h, g[2:3], w_in_b, w_fl, col_scale, conv_w[l], conv_b[l][None], gate_w_b[l],
                                  lru_gate_b[l], lru_lambda[l][None], l, B, n_axy, n_qkv)
        cum_colb, cum_row = _fox_prep(fl, f_bias[l][None], B)
        cum_row = cum_row.reshape(B, SUBLANES, S // attn_tq, attn_tq).transpose(0, 2, 1, 3)
        d_out = _diff_attn(qkv, bias_table, diff_lambda[l], diff_subln_g[l][:, None], B, lam_init, tq=attn_tq)
        f_out = _fox_attn(qkv, cum_colb, cum_row, B, tq=attn_tq)
        h = _proj_norm_res([a_out, d_out, f_out], w_out_b, l, g[3:4], h)

        kv = _norm_matmul(mem2, ones_g, w_kv_b, l, 2 * D, BF16, normalize=False)
        h = _xattn(h, g[4:5], g[5:6], w_q_b, kv, w_o_b, l, B)

        h = _ffn(h, g[6:7], g[7:8], ffn2_w_in, ffn2_w_out, l)
    return h.reshape(B, S, D).astype(x.dtype)
```

```python
import functools
import math

import jax
import jax.numpy as jnp
import numpy as np
from jax import lax
from jax.experimental import pallas as pl
from jax.experimental.pallas import tpu as pltpu

F32 = jnp.float32
BF16 = jnp.bfloat16

CHUNK = 64
CONV_W = 4
LRU_BLOCK_W = 128
RGLRU_C = 8.0
DIFF_HEADS = 4
DIFF_DH = 64
DIFF_VD = 128
FOX_HEADS = 4
FOX_DH = 128
XATTN_HEADS = 4
REL_BUCKETS = 32
REL_MAX_DIST = 128
NORM_EPS = 1e-6
NEG_INF = -1e30
FFN_RESIDUAL_WEIGHT = 0.5
LOG2E = math.log2(math.e)

LANES = 128
SUBLANES = 8
V7X_VMEM_BYTES = 64 * 1024 * 1024
VMEM_LIMIT_BYTES = (V7X_VMEM_BYTES * 3) // 4
FFN_VMEM_LIMIT_BYTES = (V7X_VMEM_BYTES * 7) // 8


def _params(*semantics, vmem_limit_bytes=VMEM_LIMIT_BYTES):
    return pltpu.CompilerParams(dimension_semantics=semantics, vmem_limit_bytes=vmem_limit_bytes)


def _rms(x, g):
    return x * lax.rsqrt(jnp.mean(x * x, axis=-1, keepdims=True) + NORM_EPS) * g


def _dot(a, b):
    return jnp.dot(a, b, preferred_element_type=F32)


def _dot_nt(a, b):
    return lax.dot_general(a, b, (((1,), (1,)), ((), ())), preferred_element_type=F32)


FFN_EDGE_ROWS = 1024


def _ffn_kernel(x_ref, gpre_ref, gpost_ref, wg_ref, wu_ref, wo_ref, o_ref, xn_ref):
    j = pl.program_id(1)
    last = pl.num_programs(1) - 1
    edge_chunks = [slice(r, r + FFN_EDGE_ROWS) for r in range(0, x_ref.shape[0], FFN_EDGE_ROWS)]

    def hidden(xn):
        gate = _dot(xn, wg_ref[...])
        up = _dot(xn, wu_ref[...])
        act = (gate * jax.nn.sigmoid(gate) * up).astype(BF16)
        return _dot(act, wo_ref[...])

    @pl.when(j == 0)
    def _():
        g = gpre_ref[...]
        for rows in edge_chunks:
            xn = _rms(x_ref[rows, :], g).astype(BF16)
            xn_ref[rows, :] = xn
            o_ref[rows, :] = hidden(xn)

    @pl.when(jnp.logical_and(j > 0, j < last))
    def _():
        o_ref[...] += hidden(xn_ref[...])

    @pl.when(j == last)
    def _():
        g = FFN_RESIDUAL_WEIGHT * gpost_ref[...]
        for rows in edge_chunks:
            y = o_ref[rows, :] + hidden(xn_ref[rows, :])
            o_ref[rows, :] = x_ref[rows, :] + _rms(y, g)


def _ffn(h, g_pre, g_post, w_in, w_out, layer, *, tm=1024, tf=256):
    T, D = h.shape
    F = w_out.shape[1]
    nf = F // tf
    return pl.pallas_call(
        _ffn_kernel,
        grid=(T // tm, nf),
        in_specs=[
            pl.BlockSpec((tm, D), lambda i, j: (i, 0)),
            pl.BlockSpec((1, D), lambda i, j: (0, 0)),
            pl.BlockSpec((1, D), lambda i, j: (0, 0)),
            pl.BlockSpec((None, D, tf), lambda i, j: (layer, 0, j)),
            pl.BlockSpec((None, D, tf), lambda i, j: (layer, 0, j + nf)),
            pl.BlockSpec((None, tf, D), lambda i, j: (layer, j, 0)),
        ],
        out_specs=pl.BlockSpec((tm, D), lambda i, j: (i, 0)),
        out_shape=jax.ShapeDtypeStruct((T, D), F32),
        scratch_shapes=[pltpu.VMEM((tm, D), BF16)],
        compiler_params=_params("parallel", "arbitrary", vmem_limit_bytes=FFN_VMEM_LIMIT_BYTES),
        name="ffn",
    )(h, g_pre, g_post, w_in, w_in, w_out)


def _norm_matmul_kernel(x_ref, g_ref, w_ref, o_ref, xn_ref, *, normalize):
    @pl.when(pl.program_id(1) == 0)
    def _():
        x = x_ref[...]
        if normalize:
            x = _rms(x, g_ref[...])
        xn_ref[...] = x.astype(BF16)

    o_ref[...] = _dot(xn_ref[...], w_ref[...]).astype(o_ref.dtype)


def _norm_matmul(x, g, w, layer, n_cols, out_dtype, *, normalize=True, tm=512, tn=1024):
    T, K = x.shape
    tn = min(tn, n_cols)
    return pl.pallas_call(
        functools.partial(_norm_matmul_kernel, normalize=normalize),
        grid=(T // tm, n_cols // tn),
        in_specs=[
            pl.BlockSpec((tm, K), lambda i, j: (i, 0)),
            pl.BlockSpec((1, K), lambda i, j: (0, 0)),
            pl.BlockSpec((None, K, tn), lambda i, j: (layer, 0, j)),
        ],
        out_specs=pl.BlockSpec((tm, tn), lambda i, j: (i, j)),
        out_shape=jax.ShapeDtypeStruct((T, n_cols), out_dtype),
        scratch_shapes=[pltpu.VMEM((tm, K), BF16)],
        compiler_params=_params("parallel", "arbitrary"),
        name="norm_matmul",
    )(x, g, w)


def _in_proj_kernel(x_ref, g_ref, w_ref, wfl_ref, cs_ref, cw_ref, cb_ref, gw_ref, gb_ref, lam_ref,
                    qkv_ref, fl_ref, aout_ref, xn_ref, axy_ref, xc_ref, tail_ref, hc_ref,
                    *, n_axy_tiles, chunk_starts, tiles_per_seq):
    i, j = pl.program_id(0), pl.program_id(1)

    @pl.when(j == 0)
    def _():
        xn = _rms(x_ref[...], g_ref[...]).astype(BF16)
        xn_ref[...] = xn
        fl_ref[...] = _dot(xn, wfl_ref[...])

    @pl.when(jnp.logical_and(j == 0, i % tiles_per_seq == 0))
    def _():
        tail_ref[...] = jnp.zeros_like(tail_ref)
        hc_ref[...] = jnp.zeros_like(hc_ref)

    @pl.when(j < n_axy_tiles)
    def _():
        axy_ref[j] = _dot(xn_ref[...], w_ref[...])

    for c, (r0, r1) in enumerate(zip(chunk_starts[:-1], chunk_starts[1:])):
        @pl.when(j == n_axy_tiles + c)
        def _(r0=r0, r1=r1):
            n_blocks = aout_ref.shape[1] // LRU_BLOCK_W
            half, cols = xn_ref.shape[0] // 2, w_ref.shape[1] // (n_blocks // 2)

            def project(n):
                rows = slice(n % 2 * half, (n % 2 + 1) * half)
                sl = slice(n // 2 * cols, (n // 2 + 1) * cols)
                qkv_ref[rows, sl] = (_dot(xn_ref[rows, :], w_ref[:, sl]) * cs_ref[:, sl]).astype(BF16)

            def store(sl, v):
                aout_ref[r0:r1, sl] = v.astype(aout_ref.dtype)

            _rglru_rows(axy_ref[0, r0:r1, :], lambda sl: axy_ref[1, r0:r1, sl], store, project, cw_ref[...],
                        cb_ref[...], gw_ref, gb_ref, lam_ref[...], xc_ref, tail_ref, hc_ref)


def _in_proj(x, g, w, w_fl, col_scale, conv_w, conv_b, gate_w, gate_b, lam, layer, batch, n_axy, n_qkv,
             *, tm=1024, tn=1024):
    T, K = x.shape
    na, nq = n_axy // tn, n_qkv // tn
    assert na == 2, "one column tile per RG-LRU branch half"
    lru_w = n_axy // 2
    align = 2 * SUBLANES
    rows = (tm // nq) // align * align
    chunk_starts = (0,) + tuple(tm - rows * (nq - 1 - c) for c in range(nq))
    return pl.pallas_call(
        functools.partial(_in_proj_kernel, n_axy_tiles=na, chunk_starts=chunk_starts, tiles_per_seq=T // batch // tm),
        grid=(T // tm, na + nq),
        in_specs=[
            pl.BlockSpec((tm, K), lambda i, j: (i, 0)),
            pl.BlockSpec((1, K), lambda i, j: (0, 0)),
            pl.BlockSpec((None, K, tn), lambda i, j: (layer, 0, j)),
            pl.BlockSpec((None, K, LANES), lambda i, j: (layer, 0, 0)),
            pl.BlockSpec((1, tn), lambda i, j: (0, jnp.maximum(j - na, 0))),
            pl.BlockSpec((CONV_W, lru_w), lambda i, j: (0, 0)),
            pl.BlockSpec((1, lru_w), lambda i, j: (0, 0)),
            pl.BlockSpec(gate_w.shape, lambda i, j: (0, 0, 0, 0)),
            pl.BlockSpec((2, lru_w), lambda i, j: (0, 0)),
            pl.BlockSpec((1, lru_w), lambda i, j: (0, 0)),
        ],
        out_specs=[
            pl.BlockSpec((tm, tn), lambda i, j: (i, jnp.maximum(j - na, 0))),
            pl.BlockSpec((tm, LANES), lambda i, j: (i, 0)),
            pl.BlockSpec((tm, lru_w), lambda i, j: (i, 0)),
        ],
        out_shape=[
            jax.ShapeDtypeStruct((T, n_qkv), BF16),
            jax.ShapeDtypeStruct((T, LANES), F32),
            jax.ShapeDtypeStruct((T, lru_w), BF16),
        ],
        scratch_shapes=[
            pltpu.VMEM((tm, K), BF16),
            pltpu.VMEM((na, tm, tn), F32),
            pltpu.VMEM((chunk_starts[1], lru_w), F32),
            pltpu.VMEM((SUBLANES, lru_w), F32),
            pltpu.VMEM((1, lru_w), F32),
        ],
        compiler_params=_params("arbitrary", "arbitrary", vmem_limit_bytes=FFN_VMEM_LIMIT_BYTES),
        name="in_proj",
    )(x, g, w, w_fl, col_scale, conv_w, conv_b, gate_w, gate_b, lam)


def _softplus(x):
    return jnp.maximum(x, 0.0) + jnp.log1p(jnp.exp(-jnp.abs(x)))


def _rglru_rows(x, ay_block, store_block, before_block, cw, cb, gw_ref, gb_ref, lam, xc_ref, tail_ref, hc_ref):
    ts = x.shape[0]
    xc = cb + cw[CONV_W - 1:CONV_W] * x
    for d in range(1, CONV_W):
        xc = xc + cw[CONV_W - 1 - d:CONV_W - d] * pltpu.roll(x, d, axis=0)
    xc_ref[0:ts, :] = xc
    x8 = x[0:SUBLANES]
    t8 = tail_ref[...]
    row8 = lax.broadcasted_iota(jnp.int32, x8.shape, 0)
    xc8 = cb + cw[CONV_W - 1:CONV_W] * x8
    for d in range(1, CONV_W):
        sh = jnp.where(row8 < d, pltpu.roll(t8, d, axis=0), pltpu.roll(x8, d, axis=0))
        xc8 = xc8 + cw[CONV_W - 1 - d:CONV_W - d] * sh
    xc_ref[0:SUBLANES, :] = xc8
    tail_ref[...] = x[ts - SUBLANES:ts]

    sp = _softplus(-lam)
    groups = ts // SUBLANES
    row = lax.broadcasted_iota(jnp.int32, (groups, SUBLANES, LRU_BLOCK_W), 1)
    for n in range(x.shape[1] // LRU_BLOCK_W):
        before_block(n)
        sl = slice(n * LRU_BLOCK_W, (n + 1) * LRU_BLOCK_W)
        xcn = xc_ref[0:ts, sl]
        xb = xcn.astype(BF16)
        r = jax.nn.sigmoid(_dot(xb, gw_ref[0, n]) + gb_ref[0:1, sl])
        i = jax.nn.sigmoid(_dot(xb, gw_ref[1, n]) + gb_ref[1:2, sl])
        log_a = -RGLRU_C * r * sp[:, sl]
        a = jnp.exp(log_a)
        b = jnp.sqrt(-jnp.tanh(log_a) * (a * a + 1.0)) * (i * xcn)
        a = a.reshape(groups, SUBLANES, LRU_BLOCK_W)
        b = b.reshape(groups, SUBLANES, LRU_BLOCK_W)
        d = 1
        while d < SUBLANES:
            a_sh = jnp.where(row < d, 1.0, pltpu.roll(a, d, axis=1))
            b_sh = jnp.where(row < d, 0.0, pltpu.roll(b, d, axis=1))
            b = a * b_sh + b
            a = a * a_sh
            d *= 2
        carry = hc_ref[:, sl]
        hs = []
        for gi in range(groups):
            hs.append(b[gi] + a[gi] * carry)
            carry = hs[-1][SUBLANES - 1:SUBLANES]
        hc_ref[:, sl] = carry
        store_block(sl, jnp.concatenate(hs, axis=0) * jax.nn.gelu(ay_block(sl)))


def _fox_prep_kernel(fl_ref, fb_ref, colb_ref, row_ref):
    x = fl_ref[...] + fb_ref[...]
    c = jnp.minimum(x, 0.0) - jnp.log1p(jnp.exp(-jnp.abs(x)))
    S = c.shape[0]
    row = lax.broadcasted_iota(jnp.int32, c.shape, 0)
    d = 1
    while d < S:
        c = c + jnp.where(row < d, 0.0, pltpu.roll(c, d, axis=0))
        d *= 2
    c = c * LOG2E
    for h in range(FOX_HEADS):
        colb_ref[h] = jnp.broadcast_to(c[:, h:h + 1], c.shape)
    for k in range(S // LANES):
        row_ref[:, k * LANES:(k + 1) * LANES] = c[k * LANES:(k + 1) * LANES, :].T[0:SUBLANES]


def _fox_prep(fl, f_bias, batch):
    T = fl.shape[0]
    S = T // batch
    return pl.pallas_call(
        _fox_prep_kernel,
        grid=(batch,),
        in_specs=[pl.BlockSpec((S, LANES), lambda b: (b, 0)), pl.BlockSpec((1, LANES), lambda b: (0, 0))],
        out_specs=[pl.BlockSpec((None, FOX_HEADS, S, LANES), lambda b: (b, 0, 0, 0)),
                   pl.BlockSpec((None, SUBLANES, S), lambda b: (b, 0, 0))],
        out_shape=[jax.ShapeDtypeStruct((batch, FOX_HEADS, S, LANES), F32),
                   jax.ShapeDtypeStruct((batch, SUBLANES, S), F32)],
        compiler_params=_params("parallel"),
        name="fox_prep",
    )(fl, f_bias)


def _t5_bucket(rel):
    nb = REL_BUCKETS // 2
    max_exact = nb // 2
    n = jnp.abs(rel)
    large = max_exact + (jnp.log(jnp.maximum(n, max_exact).astype(jnp.float32) / max_exact)
                         / math.log(REL_MAX_DIST / max_exact) * (nb - max_exact)).astype(jnp.int32)
    large = jnp.minimum(large, nb - 1)
    return jnp.where(rel > 0, nb, 0) + jnp.where(n < max_exact, n, large)


def _bias_tile_types(tq, tk):
    return -(-(tk - 1 + REL_MAX_DIST) // tq) + 1


def _bucket_tiles(tq, tk):
    nt = _bias_tile_types(tq, tk)
    t = jnp.arange(nt, dtype=jnp.int32)[:, None, None]
    c = jnp.arange(tk, dtype=jnp.int32)[None, :, None]
    r = jnp.arange(tq, dtype=jnp.int32)[None, None, :] + t * tq
    allowed = (c // CHUNK) <= (r // CHUNK)
    return jnp.where(allowed, _t5_bucket(c - r), -1)


def _bias_table_kernel(rb_ref, bucket_ref, o_ref):
    h = pl.program_id(0)
    bucket = bucket_ref[...]
    out = jnp.full(bucket.shape, NEG_INF, F32)
    for b in range(REL_BUCKETS):
        out = jnp.where(bucket == b, rb_ref[b, h] * LOG2E, out)
    o_ref[...] = out


def _bias_table(rel_bias, tq, tk):
    buckets = _bucket_tiles(tq, tk)
    nt = buckets.shape[0]
    H = rel_bias.shape[1]
    return pl.pallas_call(
        _bias_table_kernel,
        grid=(H, nt),
        in_specs=[
            pl.BlockSpec(memory_space=pltpu.SMEM),
            pl.BlockSpec((None, tk, tq), lambda h, t: (t, 0, 0)),
        ],
        out_specs=pl.BlockSpec((None, None, tk, tq), lambda h, t: (h, t, 0, 0)),
        out_shape=jax.ShapeDtypeStruct((H, nt, tk, tq), F32),
        compiler_params=_params("parallel", "parallel"),
        name="bias_table",
    )(rel_bias, buckets)


ONES_ROWS = 16
ATTN_HEADS_PER_STEP = 2


def _paired_flash_loop(pair, n_tiles, logits, vt1, sa_ref, sb_ref, m_ref, acc_ref):
    heads = range(m_ref.shape[1])
    n_steps = n_tiles + 1

    def step(w):
        if isinstance(w, int) and w < 2:
            return w, (pair, n_tiles - 1 - pair)[w], True
        sel = (w - 2 >= pair).astype(jnp.int32)
        return sel, w - 2 - sel * pair, False

    def fill(s_ref, w):
        sel, tile, diagonal = step(w)
        for h in heads:
            s_ref[h] = logits(h, sel, tile, diagonal)

    def update(s_ref, w):
        sel, tile, _ = step(w)
        for h in heads:
            s, m = s_ref[h], m_ref[sel, h]
            m_new = jnp.maximum(m, jnp.max(s, axis=0, keepdims=True))
            p = jnp.exp2(s - m_new).astype(BF16)
            acc_ref[sel, h] = jnp.exp2(m - m_new) * acc_ref[sel, h] + _dot(vt1(h, tile), p)
            m_ref[sel, h] = m_new

    m_ref[...] = jnp.full(m_ref.shape, NEG_INF, F32)
    acc_ref[...] = jnp.zeros(acc_ref.shape, F32)
    fill(sa_ref, 0)
    fill(sb_ref, 1)
    update(sa_ref, 0)
    fill(sa_ref, 2)
    update(sb_ref, 1)

    def two_steps(i, _):
        w = 3 + 2 * i
        fill(sb_ref, w)
        update(sa_ref, w - 1)
        fill(sa_ref, w + 1)
        update(sb_ref, w)
        return 0

    lax.fori_loop(0, (n_steps - 3) // 2, two_steps, 0)
    update(sa_ref, n_steps - 1)


def _flash_scratch(n_heads, tk, n, dv):
    return [pltpu.VMEM((n_heads, tk, n), F32), pltpu.VMEM((n_heads, tk, n), F32),
            pltpu.VMEM((2, n_heads, 1, n), F32), pltpu.VMEM((2, n_heads, dv + ONES_ROWS, n), F32)]


def _store_transposed(vt_ref, v_ref):
    n_heads, n_tiles, rows, tk = vt_ref.shape
    dv = rows - ONES_ROWS
    eye = (lax.broadcasted_iota(jnp.int32, (dv, dv), 0) == lax.broadcasted_iota(jnp.int32, (dv, dv), 1)).astype(BF16)
    for h in range(n_heads):
        for j in range(n_tiles):
            vt_ref[h, j, :dv, :] = _dot_nt(eye, v_ref[j * tk:(j + 1) * tk, h * dv:(h + 1) * dv]).astype(BF16)
            vt_ref[h, j, dv:, :] = jnp.ones((ONES_ROWS, tk), BF16)


def _diff_attn_kernel(q_ref, k_ref, v_ref, bias_ref, dl_ref, sg_ref, o_ref, vt_ref, qq_ref, sa_ref, sb_ref, m_ref,
                      acc_ref, *, tq, nt, lam_init):
    pair = pl.program_id(2)
    n_tiles = k_ref.shape[0] // tq

    @pl.when(pair == 0)
    def _():
        _store_transposed(vt_ref, v_ref)

    nh = ATTN_HEADS_PER_STEP
    q0s = (pl.multiple_of(pair * tq, tq), pl.multiple_of((n_tiles - 1 - pair) * tq, tq))
    lane = lax.broadcasted_iota(jnp.int32, (tq, LANES), 1)
    for sel in range(2):
        for h in range(nh):
            q = q_ref[pl.ds(q0s[sel], tq), h * LANES:(h + 1) * LANES]
            qq_ref[sel, h, :tq, :] = jnp.where(lane < DIFF_DH, q, 0)
            qq_ref[sel, h, tq:, :] = jnp.where(lane >= DIFF_DH, q, 0)

    def logits(h, sel, tile, diagonal):
        del diagonal
        k0 = pl.multiple_of(tile * tq, tq)
        q0 = q0s[0] + sel * (q0s[1] - q0s[0])
        s = _dot_nt(k_ref[pl.ds(k0, tq), h * LANES:(h + 1) * LANES], qq_ref[sel, h])
        bias = bias_ref[h, jnp.minimum((q0 - k0) // tq, nt - 1)]
        return s + jnp.concatenate([bias, bias], axis=1)

    _paired_flash_loop(pair, n_tiles, logits, lambda h, tile: vt_ref[h, tile], sa_ref, sb_ref, m_ref, acc_ref)
    dl = dl_ref[...]
    lam = (jnp.exp(jnp.sum(dl[0:1] * dl[1:2], axis=-1, keepdims=True))
           - jnp.exp(jnp.sum(dl[2:3] * dl[3:4], axis=-1, keepdims=True)) + lam_init)
    for sel in range(2):
        for h in range(nh):
            o = acc_ref[sel, h, :DIFF_VD, :] / acc_ref[sel, h, DIFF_VD:DIFF_VD + 1, :]
            d = o[:, :tq] - lam * o[:, tq:]
            y = d * lax.rsqrt(jnp.mean(d * d, axis=0, keepdims=True) + NORM_EPS) * sg_ref[...]
            o_ref[pl.ds(q0s[sel], tq), h * DIFF_VD:(h + 1) * DIFF_VD] = (y * (1.0 - lam_init)).T.astype(o_ref.dtype)


def _diff_attn(qkv, bias_table, diff_lambda, subln_g, batch, lam_init, *, tq):
    T = qkv.shape[0]
    S = T // batch
    nq = S // tq
    nt = bias_table.shape[1]
    nh = ATTN_HEADS_PER_STEP
    G = DIFF_HEADS // nh
    W = nh * LANES
    return pl.pallas_call(
        functools.partial(_diff_attn_kernel, tq=tq, nt=nt, lam_init=lam_init),
        grid=(batch, G, nq // 2),
        in_specs=[
            pl.BlockSpec((S, W), lambda b, g, i: (b, g)),
            pl.BlockSpec((S, W), lambda b, g, i: (b, G + g)),
            pl.BlockSpec((S, W), lambda b, g, i: (b, 2 * G + g)),
            pl.BlockSpec((nh, nt, tq, tq), lambda b, g, i: (g, 0, 0, 0)),
            pl.BlockSpec(diff_lambda.shape, lambda b, g, i: (0, 0)),
            pl.BlockSpec((DIFF_VD, 1), lambda b, g, i: (0, 0)),
        ],
        out_specs=pl.BlockSpec((S, nh * DIFF_VD), lambda b, g, i: (b, g)),
        out_shape=jax.ShapeDtypeStruct((T, DIFF_HEADS * DIFF_VD), BF16),
        scratch_shapes=([pltpu.VMEM((nh, nq, DIFF_VD + ONES_ROWS, tq), BF16), pltpu.VMEM((2, nh, 2 * tq, LANES), BF16)]
                        + _flash_scratch(nh, tq, 2 * tq, DIFF_VD)),
        compiler_params=_params("parallel", "parallel", "arbitrary"),
        name="diff_attn",
    )(qkv, qkv, qkv, bias_table, diff_lambda, subln_g)


def _fox_attn_kernel(q_ref, k_ref, v_ref, ck_ref, cq_ref, o_ref, vt_ref, sa_ref, sb_ref, m_ref, acc_ref, *, tq):
    pair = pl.program_id(2)
    n_tiles = k_ref.shape[0] // tq

    @pl.when(pair == 0)
    def _():
        _store_transposed(vt_ref, v_ref)

    nh = ATTN_HEADS_PER_STEP
    h0 = pl.program_id(1) * nh
    q_tiles = (pair, n_tiles - 1 - pair)
    q0s = tuple(pl.multiple_of(t * tq, tq) for t in q_tiles)
    cqs = [[cq_ref[q_tiles[sel], pl.ds(h0 + h, 1), :] for h in range(nh)] for sel in range(2)]

    def logits(h, sel, tile, diagonal):
        k0 = pl.multiple_of(tile * tq, tq)
        if diagonal:
            q0, cq = q0s[sel], cqs[sel][h]
        else:
            q0 = q0s[0] + sel * (q0s[1] - q0s[0])
            cq = jnp.where(sel == 1, cqs[1][h], cqs[0][h])
        q = q_ref[pl.ds(q0, tq), h * LANES:(h + 1) * LANES]
        ck = ck_ref[h, pl.ds(k0, tq), :]
        s = _dot_nt(k_ref[pl.ds(k0, tq), h * LANES:(h + 1) * LANES], q)
        s = s + (cq - jnp.concatenate([ck] * (tq // LANES), axis=1))
        if diagonal:
            s = jnp.where(lax.broadcasted_iota(jnp.int32, s.shape, 0) <= lax.broadcasted_iota(jnp.int32, s.shape, 1),
                          s, NEG_INF)
        return s

    _paired_flash_loop(pair, n_tiles, logits, lambda h, tile: vt_ref[h, tile], sa_ref, sb_ref, m_ref, acc_ref)
    for sel in range(2):
        for h in range(nh):
            o = acc_ref[sel, h, :FOX_DH, :] / acc_ref[sel, h, FOX_DH:FOX_DH + 1, :]
            o_ref[pl.ds(q0s[sel], tq), h * FOX_DH:(h + 1) * FOX_DH] = o.T.astype(o_ref.dtype)


def _fox_attn(qkv, cum_colb, cum_row, batch, *, tq):
    T = qkv.shape[0]
    S = T // batch
    nq = S // tq
    nh = ATTN_HEADS_PER_STEP
    G = FOX_HEADS // nh
    W = nh * LANES
    base = 3 * DIFF_HEADS // nh
    return pl.pallas_call(
        functools.partial(_fox_attn_kernel, tq=tq),
        grid=(batch, G, nq // 2),
        in_specs=[
            pl.BlockSpec((S, W), lambda b, g, i: (b, base + g)),
            pl.BlockSpec((S, W), lambda b, g, i: (b, base + G + g)),
            pl.BlockSpec((S, W), lambda b, g, i: (b, base + 2 * G + g)),
            pl.BlockSpec((None, nh, S, LANES), lambda b, g, i: (b, g, 0, 0)),
            pl.BlockSpec((None, nq, SUBLANES, tq), lambda b, g, i: (b, 0, 0, 0)),
        ],
        out_specs=pl.BlockSpec((S, nh * FOX_DH), lambda b, g, i: (b, g)),
        out_shape=jax.ShapeDtypeStruct((T, FOX_HEADS * FOX_DH), BF16),
        scratch_shapes=([pltpu.VMEM((nh, nq, FOX_DH + ONES_ROWS, tq), BF16)] + _flash_scratch(nh, tq, tq, FOX_DH)),
        compiler_params=_params("parallel", "parallel", "arbitrary"),
        name="fox_attn",
    )(qkv, qkv, qkv, cum_colb, cum_row)


def _xattn_kernel(h_ref, gq_ref, go_ref, wq_ref, k_ref, v_ref, wo_ref, o_ref, *, n_heads):
    x = h_ref[...]
    dh = x.shape[1] // n_heads
    xn = _rms(x, gq_ref[...]).astype(BF16)
    q = (_dot(xn, wq_ref[...]) * (LOG2E * dh ** -0.5)).astype(BF16)
    outs = []
    for hd in range(n_heads):
        sl = slice(hd * dh, (hd + 1) * dh)
        s = _dot_nt(q[:, sl], k_ref[:, sl])
        p = jnp.exp2(s - jnp.max(s, axis=-1, keepdims=True))
        l = jnp.sum(p, axis=-1, keepdims=True)
        outs.append((_dot(p.astype(BF16), v_ref[:, sl]) / l).astype(BF16))
    y = _dot(jnp.concatenate(outs, axis=1), wo_ref[...])
    o_ref[...] = x + _rms(y, go_ref[...])


def _xattn(h, g_q, g_o, w_q, kv, w_o, layer, batch, *, tm=512):
    T, D = h.shape
    M = kv.shape[0] // batch
    ns = T // batch // tm
    resident = pl.Buffered(1)
    return pl.pallas_call(
        functools.partial(_xattn_kernel, n_heads=XATTN_HEADS),
        grid=(batch, ns),
        in_specs=[
            pl.BlockSpec((tm, D), lambda b, i: (b * ns + i, 0)),
            pl.BlockSpec((1, D), lambda b, i: (0, 0)),
            pl.BlockSpec((1, D), lambda b, i: (0, 0)),
            pl.BlockSpec((None, D, D), lambda b, i: (layer, 0, 0), pipeline_mode=resident),
            pl.BlockSpec((M, D), lambda b, i: (b, 0)),
            pl.BlockSpec((M, D), lambda b, i: (b, 1)),
            pl.BlockSpec((None, D, D), lambda b, i: (layer, 0, 0), pipeline_mode=resident),
        ],
        out_specs=pl.BlockSpec((tm, D), lambda b, i: (b * ns + i, 0)),
        out_shape=jax.ShapeDtypeStruct((T, D), F32),
        compiler_params=_params("parallel", "arbitrary"),
        name="xattn",
    )(h, g_q, g_o, w_q, kv, kv, w_o)


def _proj_norm_res_kernel(*refs, n_in):
    x_refs, w_refs = refs[:n_in], refs[n_in:2 * n_in]
    g_ref, h_ref, o_ref = refs[2 * n_in:]
    y = _dot(x_refs[0][...], w_refs[0][...])
    for x_ref, w_ref in zip(x_refs[1:], w_refs[1:]):
        y = y + _dot(x_ref[...], w_ref[...])
    o_ref[...] = h_ref[...] + _rms(y, g_ref[...])


def _proj_norm_res(xs, w, layer, g, h, *, tm=512):
    T, N = h.shape
    n_in = len(xs)
    x_specs, w_specs, row0 = [], [], 0
    for x in xs:
        k = x.shape[1]
        x_specs.append(pl.BlockSpec((tm, k), lambda i: (i, 0)))
        w_specs.append(pl.BlockSpec((None, k, N), functools.partial(lambda i, r: (layer, r, 0), r=row0 // k)))
        row0 += k
    return pl.pallas_call(
        functools.partial(_proj_norm_res_kernel, n_in=n_in),
        grid=(T // tm,),
        in_specs=x_specs + w_specs + [pl.BlockSpec((1, N), lambda i: (0, 0)), pl.BlockSpec((tm, N), lambda i: (i, 0))],
        out_specs=pl.BlockSpec((tm, N), lambda i: (i, 0)),
        out_shape=jax.ShapeDtypeStruct((T, N), F32),
        compiler_params=_params("parallel"),
        name="proj_norm_res",
    )(*xs, *([w] * n_in), g, h)


def kernel(x, mem, norm_g, ffn1_w_in, ffn1_w_out, w_in, conv_w, conv_b, lru_gate_w, lru_gate_b, lru_lambda,
           diff_lambda, diff_subln_g, fox_f_bias, rel_bias, w_out, xattn_w_q, xattn_w_kv, xattn_w_o,
           ffn2_w_in, ffn2_w_out):
    B, S, D = x.shape
    depth = norm_g.shape[0]
    T = B * S
    lru_w = conv_w.shape[-1]
    n_axy = 2 * lru_w
    n_qkv = 3 * DIFF_HEADS * DIFF_VD + 3 * FOX_HEADS * FOX_DH
    attn_tq = 512

    h = x.astype(F32).reshape(T, D)
    mem2 = mem.astype(F32).reshape(B * mem.shape[1], D)
    ones_g = jnp.ones((1, D), F32)

    bf = lambda a: a.astype(BF16)
    ffn1_w_in, ffn1_w_out, ffn2_w_in, ffn2_w_out = bf(ffn1_w_in), bf(ffn1_w_out), bf(ffn2_w_in), bf(ffn2_w_out)
    w_in_b, w_out_b, gate_w_b = bf(w_in), bf(w_out), bf(lru_gate_w)
    w_q_b, w_kv_b, w_o_b = bf(xattn_w_q), bf(xattn_w_kv), bf(xattn_w_o)
    w_fl = jnp.pad(w_in_b[:, :, n_axy + n_qkv:], ((0, 0), (0, 0), (0, LANES - FOX_HEADS)))
    f_bias = jnp.pad(fox_f_bias, ((0, 0), (0, LANES - FOX_HEADS)))
    n_dq, n_fq = DIFF_HEADS * 2 * DIFF_DH, FOX_HEADS * FOX_DH
    fq0 = 3 * DIFF_HEADS * DIFF_VD
    col_scale = jnp.ones((1, n_qkv), F32)
    col_scale = col_scale.at[:, :n_dq].set(LOG2E * DIFF_DH ** -0.5).at[:, fq0:fq0 + n_fq].set(LOG2E * FOX_DH ** -0.5)

    bias_table = _bias_table(rel_bias, attn_tq, attn_tq)

    for l in range(depth):
        g = norm_g[l]
        lam_init = 0.8 - 0.6 * math.exp(-0.3 * l)
        h = _ffn(h, g[0:1], g[1:2], ffn1_w_in, ffn1_w_out, l)

        qkv, fl, a_out = _in_proj(h, g[2:3], w_in_b, w_fl, col_scale, conv_w[l], conv_b[l][None], gate_w_b[l],
                                  lru_gate_b[l], lru_lambda[l][None], l, B, n_axy, n_qkv)
        cum_colb, cum_row = _fox_prep(fl, f_bias[l][None], B)
        cum_row = cum_row.reshape(B, SUBLANES, S // attn_tq, attn_tq).transpose(0, 2, 1, 3)
        d_out = _diff_attn(qkv, bias_table, diff_lambda[l], diff_subln_g[l][:, None], B, lam_init, tq=attn_tq)
        f_out = _fox_attn(qkv, cum_colb, cum_row, B, tq=attn_tq)
        h = _proj_norm_res([a_out, d_out, f_out], w_out_b, l, g[3:4], h)

        kv = _norm_matmul(mem2, ones_g, w_kv_b, l, 2 * D, BF16, normalize=False)
        h = _xattn(h, g[4:5], g[5:6], w_q_b, kv, w_o_b, l, B)

        h = _ffn(h, g[6:7], g[7:8], ffn2_w_in, ffn2_w_out, l)
    return h.reshape(B, S, D).astype(x.dtype)
```

```python
import functools
import math

import jax
import jax.numpy as jnp
import numpy as np
from jax import lax
from jax.experimental import pallas as pl
from jax.experimental.pallas import tpu as pltpu

F32 = jnp.float32
BF16 = jnp.bfloat16

CHUNK = 64
CONV_W = 4
LRU_BLOCK_W = 128
RGLRU_C = 8.0
DIFF_HEADS = 4
DIFF_DH = 64
DIFF_VD = 128
FOX_HEADS = 4
FOX_DH = 128
XATTN_HEADS = 4
REL_BUCKETS = 32
REL_MAX_DIST = 128
NORM_EPS = 1e-6
NEG_INF = -1e30
FFN_RESIDUAL_WEIGHT = 0.5
LOG2E = math.log2(math.e)

LANES = 128
SUBLANES = 8
V7X_VMEM_BYTES = 64 * 1024 * 1024
VMEM_LIMIT_BYTES = (V7X_VMEM_BYTES * 3) // 4
FFN_VMEM_LIMIT_BYTES = (V7X_VMEM_BYTES * 7) // 8


def _params(*semantics, vmem_limit_bytes=VMEM_LIMIT_BYTES):
    return pltpu.CompilerParams(dimension_semantics=semantics, vmem_limit_bytes=vmem_limit_bytes)


def _rms(x, g):
    return x * lax.rsqrt(jnp.mean(x * x, axis=-1, keepdims=True) + NORM_EPS) * g


def _dot(a, b):
    return jnp.dot(a, b, preferred_element_type=F32)


def _dot_nt(a, b):
    return lax.dot_general(a, b, (((1,), (1,)), ((), ())), preferred_element_type=F32)


FFN_EDGE_ROWS = 512


def _ffn_kernel(x_ref, gpre_ref, gpost_ref, wg_ref, wu_ref, wo_ref, o_ref, xn_ref):
    j = pl.program_id(1)
    last = pl.num_programs(1) - 1
    edge_chunks = [slice(r, r + FFN_EDGE_ROWS) for r in range(0, x_ref.shape[0], FFN_EDGE_ROWS)]

    def hidden(xn):
        gate = _dot(xn, wg_ref[...])
        up = _dot(xn, wu_ref[...])
        act = (gate * jax.nn.sigmoid(gate) * up).astype(BF16)
        return _dot(act, wo_ref[...])

    @pl.when(j == 0)
    def _():
        g = gpre_ref[...]
        for rows in edge_chunks:
            xn = _rms(x_ref[rows, :], g).astype(BF16)
            xn_ref[rows, :] = xn
            o_ref[rows, :] = hidden(xn)

    @pl.when(jnp.logical_and(j > 0, j < last))
    def _():
        o_ref[...] += hidden(xn_ref[...])

    @pl.when(j == last)
    def _():
        g = FFN_RESIDUAL_WEIGHT * gpost_ref[...]
        for rows in edge_chunks:
            y = o_ref[rows, :] + hidden(xn_ref[rows, :])
            o_ref[rows, :] = x_ref[rows, :] + _rms(y, g)


def _ffn(h, g_pre, g_post, w_in, w_out, layer, *, tm=1024, tf=256):
    T, D = h.shape
    F = w_out.shape[1]
    nf = F // tf
    return pl.pallas_call(
        _ffn_kernel,
        grid=(T // tm, nf),
        in_specs=[
            pl.BlockSpec((tm, D), lambda i, j: (i, 0)),
            pl.BlockSpec((1, D), lambda i, j: (0, 0)),
            pl.BlockSpec((1, D), lambda i, j: (0, 0)),
            pl.BlockSpec((None, D, tf), lambda i, j: (layer, 0, j)),
            pl.BlockSpec((None, D, tf), lambda i, j: (layer, 0, j + nf)),
            pl.BlockSpec((None, tf, D), lambda i, j: (layer, j, 0)),
        ],
        out_specs=pl.BlockSpec((tm, D), lambda i, j: (i, 0)),
        out_shape=jax.ShapeDtypeStruct((T, D), F32),
        scratch_shapes=[pltpu.VMEM((tm, D), BF16)],
        compiler_params=_params("parallel", "arbitrary", vmem_limit_bytes=FFN_VMEM_LIMIT_BYTES),
        name="ffn",
    )(h, g_pre, g_post, w_in, w_in, w_out)


def _norm_matmul_kernel(x_ref, g_ref, w_ref, o_ref, xn_ref, *, normalize):
    @pl.when(pl.program_id(1) == 0)
    def _():
        x = x_ref[...]
        if normalize:
            x = _rms(x, g_ref[...])
        xn_ref[...] = x.astype(BF16)

    o_ref[...] = _dot(xn_ref[...], w_ref[...]).astype(o_ref.dtype)


def _norm_matmul(x, g, w, layer, n_cols, out_dtype, *, normalize=True, tm=512, tn=1024):
    T, K = x.shape
    tn = min(tn, n_cols)
    return pl.pallas_call(
        functools.partial(_norm_matmul_kernel, normalize=normalize),
        grid=(T // tm, n_cols // tn),
        in_specs=[
            pl.BlockSpec((tm, K), lambda i, j: (i, 0)),
            pl.BlockSpec((1, K), lambda i, j: (0, 0)),
            pl.BlockSpec((None, K, tn), lambda i, j: (layer, 0, j)),
        ],
        out_specs=pl.BlockSpec((tm, tn), lambda i, j: (i, j)),
        out_shape=jax.ShapeDtypeStruct((T, n_cols), out_dtype),
        scratch_shapes=[pltpu.VMEM((tm, K), BF16)],
        compiler_params=_params("parallel", "arbitrary"),
        name="norm_matmul",
    )(x, g, w)


def _in_proj_kernel(x_ref, g_ref, w_ref, wfl_ref, cs_ref, cw_ref, cb_ref, gw_ref, gb_ref, lam_ref,
                    qkv_ref, fl_ref, aout_ref, xn_ref, axy_ref, xc_ref, tail_ref, hc_ref,
                    *, n_axy_tiles, chunk_starts, tiles_per_seq):
    i, j = pl.program_id(0), pl.program_id(1)

    @pl.when(j == 0)
    def _():
        xn = _rms(x_ref[...], g_ref[...]).astype(BF16)
        xn_ref[...] = xn
        fl_ref[...] = _dot(xn, wfl_ref[...])

    @pl.when(jnp.logical_and(j == 0, i % tiles_per_seq == 0))
    def _():
        tail_ref[...] = jnp.zeros_like(tail_ref)
        hc_ref[...] = jnp.zeros_like(hc_ref)

    @pl.when(j < n_axy_tiles)
    def _():
        axy_ref[j] = _dot(xn_ref[...], w_ref[...])

    for c, (r0, r1) in enumerate(zip(chunk_starts[:-1], chunk_starts[1:])):
        @pl.when(j == n_axy_tiles + c)
        def _(r0=r0, r1=r1):
            n_blocks = aout_ref.shape[1] // LRU_BLOCK_W
            half, cols = xn_ref.shape[0] // 2, w_ref.shape[1] // (n_blocks // 2)

            def project(n):
                rows = slice(n % 2 * half, (n % 2 + 1) * half)
                sl = slice(n // 2 * cols, (n // 2 + 1) * cols)
                qkv_ref[rows, sl] = (_dot(xn_ref[rows, :], w_ref[:, sl]) * cs_ref[:, sl]).astype(BF16)

            def store(sl, v):
                aout_ref[r0:r1, sl] = v.astype(aout_ref.dtype)

            _rglru_rows(axy_ref[0, r0:r1, :], lambda sl: axy_ref[1, r0:r1, sl], store, project, cw_ref[...],
                        cb_ref[...], gw_ref, gb_ref, lam_ref[...], xc_ref, tail_ref, hc_ref)


def _in_proj(x, g, w, w_fl, col_scale, conv_w, conv_b, gate_w, gate_b, lam, layer, batch, n_axy, n_qkv,
             *, tm=1024, tn=1024):
    T, K = x.shape
    na, nq = n_axy // tn, n_qkv // tn
    assert na == 2, "one column tile per RG-LRU branch half"
    lru_w = n_axy // 2
    align = 2 * SUBLANES
    rows = (tm // nq) // align * align
    chunk_starts = (0,) + tuple(tm - rows * (nq - 1 - c) for c in range(nq))
    return pl.pallas_call(
        functools.partial(_in_proj_kernel, n_axy_tiles=na, chunk_starts=chunk_starts, tiles_per_seq=T // batch // tm),
        grid=(T // tm, na + nq),
        in_specs=[
            pl.BlockSpec((tm, K), lambda i, j: (i, 0)),
            pl.BlockSpec((1, K), lambda i, j: (0, 0)),
            pl.BlockSpec((None, K, tn), lambda i, j: (layer, 0, j)),
            pl.BlockSpec((None, K, LANES), lambda i, j: (layer, 0, 0)),
            pl.BlockSpec((1, tn), lambda i, j: (0, jnp.maximum(j - na, 0))),
            pl.BlockSpec((CONV_W, lru_w), lambda i, j: (0, 0)),
            pl.BlockSpec((1, lru_w), lambda i, j: (0, 0)),
            pl.BlockSpec(gate_w.shape, lambda i, j: (0, 0, 0, 0)),
            pl.BlockSpec((2, lru_w), lambda i, j: (0, 0)),
            pl.BlockSpec((1, lru_w), lambda i, j: (0, 0)),
        ],
        out_specs=[
            pl.BlockSpec((tm, tn), lambda i, j: (i, jnp.maximum(j - na, 0))),
            pl.BlockSpec((tm, LANES), lambda i, j: (i, 0)),
            pl.BlockSpec((tm, lru_w), lambda i, j: (i, 0)),
        ],
        out_shape=[
            jax.ShapeDtypeStruct((T, n_qkv), BF16),
            jax.ShapeDtypeStruct((T, LANES), F32),
            jax.ShapeDtypeStruct((T, lru_w), BF16),
        ],
        scratch_shapes=[
            pltpu.VMEM((tm, K), BF16),
            pltpu.VMEM((na, tm, tn), F32),
            pltpu.VMEM((chunk_starts[1], lru_w), F32),
            pltpu.VMEM((SUBLANES, lru_w), F32),
            pltpu.VMEM((1, lru_w), F32),
        ],
        compiler_params=_params("arbitrary", "arbitrary", vmem_limit_bytes=FFN_VMEM_LIMIT_BYTES),
        name="in_proj",
    )(x, g, w, w_fl, col_scale, conv_w, conv_b, gate_w, gate_b, lam)


def _softplus(x):
    return jnp.maximum(x, 0.0) + jnp.log1p(jnp.exp(-jnp.abs(x)))


def _rglru_rows(x, ay_block, store_block, before_block, cw, cb, gw_ref, gb_ref, lam, xc_ref, tail_ref, hc_ref):
    ts = x.shape[0]
    xc = cb + cw[CONV_W - 1:CONV_W] * x
    for d in range(1, CONV_W):
        xc = xc + cw[CONV_W - 1 - d:CONV_W - d] * pltpu.roll(x, d, axis=0)
    xc_ref[0:ts, :] = xc
    x8 = x[0:SUBLANES]
    t8 = tail_ref[...]
    row8 = lax.broadcasted_iota(jnp.int32, x8.shape, 0)
    xc8 = cb + cw[CONV_W - 1:CONV_W] * x8
    for d in range(1, CONV_W):
        sh = jnp.where(row8 < d, pltpu.roll(t8, d, axis=0), pltpu.roll(x8, d, axis=0))
        xc8 = xc8 + cw[CONV_W - 1 - d:CONV_W - d] * sh
    xc_ref[0:SUBLANES, :] = xc8
    tail_ref[...] = x[ts - SUBLANES:ts]

    sp = _softplus(-lam)
    groups = ts // SUBLANES
    row = lax.broadcasted_iota(jnp.int32, (groups, SUBLANES, LRU_BLOCK_W), 1)
    for n in range(x.shape[1] // LRU_BLOCK_W):
        before_block(n)
        sl = slice(n * LRU_BLOCK_W, (n + 1) * LRU_BLOCK_W)
        xcn = xc_ref[0:ts, sl]
        xb = xcn.astype(BF16)
        r = jax.nn.sigmoid(_dot(xb, gw_ref[0, n]) + gb_ref[0:1, sl])
        i = jax.nn.sigmoid(_dot(xb, gw_ref[1, n]) + gb_ref[1:2, sl])
        log_a = -RGLRU_C * r * sp[:, sl]
        a = jnp.exp(log_a)
        b = jnp.sqrt(-jnp.tanh(log_a) * (a * a + 1.0)) * (i * xcn)
        a = a.reshape(groups, SUBLANES, LRU_BLOCK_W)
        b = b.reshape(groups, SUBLANES, LRU_BLOCK_W)
        d = 1
        while d < SUBLANES:
            a_sh = jnp.where(row < d, 1.0, pltpu.roll(a, d, axis=1))
            b_sh = jnp.where(row < d, 0.0, pltpu.roll(b, d, axis=1))
            b = a * b_sh + b
            a = a * a_sh
            d *= 2
        carry = hc_ref[:, sl]
        hs = []
        for gi in range(groups):
            hs.append(b[gi] + a[gi] * carry)
            carry = hs[-1][SUBLANES - 1:SUBLANES]
        hc_ref[:, sl] = carry
        store_block(sl, jnp.concatenate(hs, axis=0) * jax.nn.gelu(ay_block(sl)))


def _fox_prep_kernel(fl_ref, fb_ref, colb_ref, row_ref):
    x = fl_ref[...] + fb_ref[...]
    c = jnp.minimum(x, 0.0) - jnp.log1p(jnp.exp(-jnp.abs(x)))
    S = c.shape[0]
    row = lax.broadcasted_iota(jnp.int32, c.shape, 0)
    d = 1
    while d < S:
        c = c + jnp.where(row < d, 0.0, pltpu.roll(c, d, axis=0))
        d *= 2
    c = c * LOG2E
    for h in range(FOX_HEADS):
        colb_ref[h] = jnp.broadcast_to(c[:, h:h + 1], c.shape)
    for k in range(S // LANES):
        row_ref[:, k * LANES:(k + 1) * LANES] = c[k * LANES:(k + 1) * LANES, :].T[0:SUBLANES]


def _fox_prep(fl, f_bias, batch):
    T = fl.shape[0]
    S = T // batch
    return pl.pallas_call(
        _fox_prep_kernel,
        grid=(batch,),
        in_specs=[pl.BlockSpec((S, LANES), lambda b: (b, 0)), pl.BlockSpec((1, LANES), lambda b: (0, 0))],
        out_specs=[pl.BlockSpec((None, FOX_HEADS, S, LANES), lambda b: (b, 0, 0, 0)),
                   pl.BlockSpec((None, SUBLANES, S), lambda b: (b, 0, 0))],
        out_shape=[jax.ShapeDtypeStruct((batch, FOX_HEADS, S, LANES), F32),
                   jax.ShapeDtypeStruct((batch, SUBLANES, S), F32)],
        compiler_params=_params("parallel"),
        name="fox_prep",
    )(fl, f_bias)


def _t5_bucket(rel):
    nb = REL_BUCKETS // 2
    max_exact = nb // 2
    n = jnp.abs(rel)
    large = max_exact + (jnp.log(jnp.maximum(n, max_exact).astype(jnp.float32) / max_exact)
                         / math.log(REL_MAX_DIST / max_exact) * (nb - max_exact)).astype(jnp.int32)
    large = jnp.minimum(large, nb - 1)
    return jnp.where(rel > 0, nb, 0) + jnp.where(n < max_exact, n, large)


def _bias_tile_types(tq, tk):
    return -(-(tk - 1 + REL_MAX_DIST) // tq) + 1


def _bucket_tiles(tq, tk):
    nt = _bias_tile_types(tq, tk)
    t = jnp.arange(nt, dtype=jnp.int32)[:, None, None]
    c = jnp.arange(tk, dtype=jnp.int32)[None, :, None]
    r = jnp.arange(tq, dtype=jnp.int32)[None, None, :] + t * tq
    allowed = (c // CHUNK) <= (r // CHUNK)
    return jnp.where(allowed, _t5_bucket(c - r), -1)


def _bias_table_kernel(rb_ref, bucket_ref, o_ref):
    h = pl.program_id(0)
    bucket = bucket_ref[...]
    out = jnp.full(bucket.shape, NEG_INF, F32)
    for b in range(REL_BUCKETS):
        out = jnp.where(bucket == b, rb_ref[b, h] * LOG2E, out)
    o_ref[...] = out


def _bias_table(rel_bias, tq, tk):
    buckets = _bucket_tiles(tq, tk)
    nt = buckets.shape[0]
    H = rel_bias.shape[1]
    return pl.pallas_call(
        _bias_table_kernel,
        grid=(H, nt),
        in_specs=[
            pl.BlockSpec(memory_space=pltpu.SMEM),
            pl.BlockSpec((None, tk, tq), lambda h, t: (t, 0, 0)),
        ],
        out_specs=pl.BlockSpec((None, None, tk, tq), lambda h, t: (h, t, 0, 0)),
        out_shape=jax.ShapeDtypeStruct((H, nt, tk, tq), F32),
        compiler_params=_params("parallel", "parallel"),
        name="bias_table",
    )(rel_bias, buckets)


ONES_ROWS = 16
ATTN_HEADS_PER_STEP = 2


def _paired_flash_loop(pair, n_tiles, logits, vt1, sa_ref, sb_ref, m_ref, acc_ref):
    heads = range(m_ref.shape[1])
    n_steps = n_tiles + 1

    def step(w):
        if isinstance(w, int) and w < 2:
            return w, (pair, n_tiles - 1 - pair)[w], True
        sel = (w - 2 >= pair).astype(jnp.int32)
        return sel, w - 2 - sel * pair, False

    def fill(s_ref, w):
        sel, tile, diagonal = step(w)
        for h in heads:
            s_ref[h] = logits(h, sel, tile, diagonal)

    def update(s_ref, w):
        sel, tile, _ = step(w)
        for h in heads:
            s, m = s_ref[h], m_ref[sel, h]
            m_new = jnp.maximum(m, jnp.max(s, axis=0, keepdims=True))
            p = jnp.exp2(s - m_new).astype(BF16)
            acc_ref[sel, h] = jnp.exp2(m - m_new) * acc_ref[sel, h] + _dot(vt1(h, tile), p)
            m_ref[sel, h] = m_new

    m_ref[...] = jnp.full(m_ref.shape, NEG_INF, F32)
    acc_ref[...] = jnp.zeros(acc_ref.shape, F32)
    fill(sa_ref, 0)
    fill(sb_ref, 1)
    update(sa_ref, 0)
    fill(sa_ref, 2)
    update(sb_ref, 1)

    def two_steps(i, _):
        w = 3 + 2 * i
        fill(sb_ref, w)
        update(sa_ref, w - 1)
        fill(sa_ref, w + 1)
        update(sb_ref, w)
        return 0

    lax.fori_loop(0, (n_steps - 3) // 2, two_steps, 0)
    update(sa_ref, n_steps - 1)


def _flash_scratch(n_heads, tk, n, dv):
    return [pltpu.VMEM((n_heads, tk, n), F32), pltpu.VMEM((n_heads, tk, n), F32),
            pltpu.VMEM((2, n_heads, 1, n), F32), pltpu.VMEM((2, n_heads, dv + ONES_ROWS, n), F32)]


def _store_transposed(vt_ref, v_ref):
    n_heads, n_tiles, rows, tk = vt_ref.shape
    dv = rows - ONES_ROWS
    eye = (lax.broadcasted_iota(jnp.int32, (dv, dv), 0) == lax.broadcasted_iota(jnp.int32, (dv, dv), 1)).astype(BF16)
    for h in range(n_heads):
        for j in range(n_tiles):
            vt_ref[h, j, :dv, :] = _dot_nt(eye, v_ref[j * tk:(j + 1) * tk, h * dv:(h + 1) * dv]).astype(BF16)
            vt_ref[h, j, dv:, :] = jnp.ones((ONES_ROWS, tk), BF16)


def _diff_attn_kernel(q_ref, k_ref, v_ref, bias_ref, dl_ref, sg_ref, o_ref, vt_ref, qq_ref, sa_ref, sb_ref, m_ref,
                      acc_ref, *, tq, nt, lam_init):
    pair = pl.program_id(2)
    n_tiles = k_ref.shape[0] // tq

    @pl.when(pair == 0)
    def _():
        _store_transposed(vt_ref, v_ref)

    nh = ATTN_HEADS_PER_STEP
    q0s = (pl.multiple_of(pair * tq, tq), pl.multiple_of((n_tiles - 1 - pair) * tq, tq))
    lane = lax.broadcasted_iota(jnp.int32, (tq, LANES), 1)
    for sel in range(2):
        for h in range(nh):
            q = q_ref[pl.ds(q0s[sel], tq), h * LANES:(h + 1) * LANES]
            qq_ref[sel, h, :tq, :] = jnp.where(lane < DIFF_DH, q, 0)
            qq_ref[sel, h, tq:, :] = jnp.where(lane >= DIFF_DH, q, 0)

    def logits(h, sel, tile, diagonal):
        del diagonal
        k0 = pl.multiple_of(tile * tq, tq)
        q0 = q0s[0] + sel * (q0s[1] - q0s[0])
        s = _dot_nt(k_ref[pl.ds(k0, tq), h * LANES:(h + 1) * LANES], qq_ref[sel, h])
        bias = bias_ref[h, jnp.minimum((q0 - k0) // tq, nt - 1)]
        return s + jnp.concatenate([bias, bias], axis=1)

    _paired_flash_loop(pair, n_tiles, logits, lambda h, tile: vt_ref[h, tile], sa_ref, sb_ref, m_ref, acc_ref)
    dl = dl_ref[...]
    lam = (jnp.exp(jnp.sum(dl[0:1] * dl[1:2], axis=-1, keepdims=True))
           - jnp.exp(jnp.sum(dl[2:3] * dl[3:4], axis=-1, keepdims=True)) + lam_init)
    for sel in range(2):
        for h in range(nh):
            o = acc_ref[sel, h, :DIFF_VD, :] / acc_ref[sel, h, DIFF_VD:DIFF_VD + 1, :]
            d = o[:, :tq] - lam * o[:, tq:]
            y = d * lax.rsqrt(jnp.mean(d * d, axis=0, keepdims=True) + NORM_EPS) * sg_ref[...]
            o_ref[pl.ds(q0s[sel], tq), h * DIFF_VD:(h + 1) * DIFF_VD] = (y * (1.0 - lam_init)).T.astype(o_ref.dtype)


def _diff_attn(qkv, bias_table, diff_lambda, subln_g, batch, lam_init, *, tq):
    T = qkv.shape[0]
    S = T // batch
    nq = S // tq
    nt = bias_table.shape[1]
    nh = ATTN_HEADS_PER_STEP
    G = DIFF_HEADS // nh
    W = nh * LANES
    return pl.pallas_call(
        functools.partial(_diff_attn_kernel, tq=tq, nt=nt, lam_init=lam_init),
        grid=(batch, G, nq // 2),
        in_specs=[
            pl.BlockSpec((S, W), lambda b, g, i: (b, g)),
            pl.BlockSpec((S, W), lambda b, g, i: (b, G + g)),
            pl.BlockSpec((S, W), lambda b, g, i: (b, 2 * G + g)),
            pl.BlockSpec((nh, nt, tq, tq), lambda b, g, i: (g, 0, 0, 0)),
            pl.BlockSpec(diff_lambda.shape, lambda b, g, i: (0, 0)),
            pl.BlockSpec((DIFF_VD, 1), lambda b, g, i: (0, 0)),
        ],
        out_specs=pl.BlockSpec((S, nh * DIFF_VD), lambda b, g, i: (b, g)),
        out_shape=jax.ShapeDtypeStruct((T, DIFF_HEADS * DIFF_VD), BF16),
        scratch_shapes=([pltpu.VMEM((nh, nq, DIFF_VD + ONES_ROWS, tq), BF16), pltpu.VMEM((2, nh, 2 * tq, LANES), BF16)]
                        + _flash_scratch(nh, tq, 2 * tq, DIFF_VD)),
        compiler_params=_params("parallel", "parallel", "arbitrary"),
        name="diff_attn",
    )(qkv, qkv, qkv, bias_table, diff_lambda, subln_g)


def _fox_attn_kernel(q_ref, k_ref, v_ref, ck_ref, cq_ref, o_ref, vt_ref, sa_ref, sb_ref, m_ref, acc_ref, *, tq):
    pair = pl.program_id(2)
    n_tiles = k_ref.shape[0] // tq

    @pl.when(pair == 0)
    def _():
        _store_transposed(vt_ref, v_ref)

    nh = ATTN_HEADS_PER_STEP
    h0 = pl.program_id(1) * nh
    q_tiles = (pair, n_tiles - 1 - pair)
    q0s = tuple(pl.multiple_of(t * tq, tq) for t in q_tiles)
    cqs = [[cq_ref[q_tiles[sel], pl.ds(h0 + h, 1), :] for h in range(nh)] for sel in range(2)]

    def logits(h, sel, tile, diagonal):
        k0 = pl.multiple_of(tile * tq, tq)
        if diagonal:
            q0, cq = q0s[sel], cqs[sel][h]
        else:
            q0 = q0s[0] + sel * (q0s[1] - q0s[0])
            cq = jnp.where(sel == 1, cqs[1][h], cqs[0][h])
        q = q_ref[pl.ds(q0, tq), h * LANES:(h + 1) * LANES]
        ck = ck_ref[h, pl.ds(k0, tq), :]
        s = _dot_nt(k_ref[pl.ds(k0, tq), h * LANES:(h + 1) * LANES], q)
        s = s + (cq - jnp.concatenate([ck] * (tq // LANES), axis=1))
        if diagonal:
            s = jnp.where(lax.broadcasted_iota(jnp.int32, s.shape, 0) <= lax.broadcasted_iota(jnp.int32, s.shape, 1),
                          s, NEG_INF)
        return s

    _paired_flash_loop(pair, n_tiles, logits, lambda h, tile: vt_ref[h, tile], sa_ref, sb_ref, m_ref, acc_ref)
    for sel in range(2):
        for h in range(nh):
            o = acc_ref[sel, h, :FOX_DH, :] / acc_ref[sel, h, FOX_DH:FOX_DH + 1, :]
            o_ref[pl.ds(q0s[sel], tq), h * FOX_DH:(h + 1) * FOX_DH] = o.T.astype(o_ref.dtype)


def _fox_attn(qkv, cum_colb, cum_row, batch, *, tq):
    T = qkv.shape[0]
    S = T // batch
    nq = S // tq
    nh = ATTN_HEADS_PER_STEP
    G = FOX_HEADS // nh
    W = nh * LANES
    base = 3 * DIFF_HEADS // nh
    return pl.pallas_call(
        functools.partial(_fox_attn_kernel, tq=tq),
        grid=(batch, G, nq // 2),
        in_specs=[
            pl.BlockSpec((S, W), lambda b, g, i: (b, base + g)),
            pl.BlockSpec((S, W), lambda b, g, i: (b, base + G + g)),
            pl.BlockSpec((S, W), lambda b, g, i: (b, base + 2 * G + g)),
            pl.BlockSpec((None, nh, S, LANES), lambda b, g, i: (b, g, 0, 0)),
            pl.BlockSpec((None, nq, SUBLANES, tq), lambda b, g, i: (b, 0, 0, 0)),
        ],
        out_specs=pl.BlockSpec((S, nh * FOX_DH), lambda b, g, i: (b, g)),
        out_shape=jax.ShapeDtypeStruct((T, FOX_HEADS * FOX_DH), BF16),
        scratch_shapes=([pltpu.VMEM((nh, nq, FOX_DH + ONES_ROWS, tq), BF16)] + _flash_scratch(nh, tq, tq, FOX_DH)),
        compiler_params=_params("parallel", "parallel", "arbitrary"),
        name="fox_attn",
    )(qkv, qkv, qkv, cum_colb, cum_row)


def _xattn_kernel(h_ref, gq_ref, go_ref, wq_ref, k_ref, v_ref, wo_ref, o_ref, *, n_heads):
    x = h_ref[...]
    dh = x.shape[1] // n_heads
    xn = _rms(x, gq_ref[...]).astype(BF16)
    q = (_dot(xn, wq_ref[...]) * (LOG2E * dh ** -0.5)).astype(BF16)
    outs = []
    for hd in range(n_heads):
        sl = slice(hd * dh, (hd + 1) * dh)
        s = _dot_nt(q[:, sl], k_ref[:, sl])
        p = jnp.exp2(s - jnp.max(s, axis=-1, keepdims=True))
        l = jnp.sum(p, axis=-1, keepdims=True)
        outs.append((_dot(p.astype(BF16), v_ref[:, sl]) / l).astype(BF16))
    y = _dot(jnp.concatenate(outs, axis=1), wo_ref[...])
    o_ref[...] = x + _rms(y, go_ref[...])


def _xattn(h, g_q, g_o, w_q, kv, w_o, layer, batch, *, tm=512):
    T, D = h.shape
    M = kv.shape[0] // batch
    ns = T // batch // tm
    resident = pl.Buffered(1)
    return pl.pallas_call(
        functools.partial(_xattn_kernel, n_heads=XATTN_HEADS),
        grid=(batch, ns),
        in_specs=[
            pl.BlockSpec((tm, D), lambda b, i: (b * ns + i, 0)),
            pl.BlockSpec((1, D), lambda b, i: (0, 0)),
            pl.BlockSpec((1, D), lambda b, i: (0, 0)),
            pl.BlockSpec((None, D, D), lambda b, i: (layer, 0, 0), pipeline_mode=resident),
            pl.BlockSpec((M, D), lambda b, i: (b, 0)),
            pl.BlockSpec((M, D), lambda b, i: (b, 1)),
            pl.BlockSpec((None, D, D), lambda b, i: (layer, 0, 0), pipeline_mode=resident),
        ],
        out_specs=pl.BlockSpec((tm, D), lambda b, i: (b * ns + i, 0)),
        out_shape=jax.ShapeDtypeStruct((T, D), F32),
        compiler_params=_params("parallel", "arbitrary"),
        name="xattn",
    )(h, g_q, g_o, w_q, kv, kv, w_o)


def _proj_norm_res_kernel(*refs, n_in):
    x_refs, w_refs = refs[:n_in], refs[n_in:2 * n_in]
    g_ref, h_ref, o_ref = refs[2 * n_in:]
    y = _dot(x_refs[0][...], w_refs[0][...])
    for x_ref, w_ref in zip(x_refs[1:], w_refs[1:]):
        y = y + _dot(x_ref[...], w_ref[...])
    o_ref[...] = h_ref[...] + _rms(y, g_ref[...])


def _proj_norm_res(xs, w, layer, g, h, *, tm=512):
    T, N = h.shape
    n_in = len(xs)
    x_specs, w_specs, row0 = [], [], 0
    for x in xs:
        k = x.shape[1]
        x_specs.append(pl.BlockSpec((tm, k), lambda i: (i, 0)))
        w_specs.append(pl.BlockSpec((None, k, N), functools.partial(lambda i, r: (layer, r, 0), r=row0 // k)))
        row0 += k
    return pl.pallas_call(
        functools.partial(_proj_norm_res_kernel, n_in=n_in),
        grid=(T // tm,),
        in_specs=x_specs + w_specs + [pl.BlockSpec((1, N), lambda i: (0, 0)), pl.BlockSpec((tm, N), lambda i: (i, 0))],
        out_specs=pl.BlockSpec((tm, N), lambda i: (i, 0)),
        out_shape=jax.ShapeDtypeStruct((T, N), F32),
        compiler_params=_params("parallel"),
        name="proj_norm_res",
    )(*xs, *([w] * n_in), g, h)


def kernel(x, mem, norm_g, ffn1_w_in, ffn1_w_out, w_in, conv_w, conv_b, lru_gate_w, lru_gate_b, lru_lambda,
           diff_lambda, diff_subln_g, fox_f_bias, rel_bias, w_out, xattn_w_q, xattn_w_kv, xattn_w_o,
           ffn2_w_in, ffn2_w_out):
    B, S, D = x.shape
    depth = norm_g.shape[0]
    T = B * S
    lru_w = conv_w.shape[-1]
    n_axy = 2 * lru_w
    n_qkv = 3 * DIFF_HEADS * DIFF_VD + 3 * FOX_HEADS * FOX_DH
    attn_tq = 512

    h = x.astype(F32).reshape(T, D)
    mem2 = mem.astype(F32).reshape(B * mem.shape[1], D)
    ones_g = jnp.ones((1, D), F32)

    bf = lambda a: a.astype(BF16)
    ffn1_w_in, ffn1_w_out, ffn2_w_in, ffn2_w_out = bf(ffn1_w_in), bf(ffn1_w_out), bf(ffn2_w_in), bf(ffn2_w_out)
    w_in_b, w_out_b, gate_w_b = bf(w_in), bf(w_out), bf(lru_gate_w)
    w_q_b, w_kv_b, w_o_b = bf(xattn_w_q), bf(xattn_w_kv), bf(xattn_w_o)
    w_fl = jnp.pad(w_in_b[:, :, n_axy + n_qkv:], ((0, 0), (0, 0), (0, LANES - FOX_HEADS)))
    f_bias = jnp.pad(fox_f_bias, ((0, 0), (0, LANES - FOX_HEADS)))
    n_dq, n_fq = DIFF_HEADS * 2 * DIFF_DH, FOX_HEADS * FOX_DH
    fq0 = 3 * DIFF_HEADS * DIFF_VD
    col_scale = jnp.ones((1, n_qkv), F32)
    col_scale = col_scale.at[:, :n_dq].set(LOG2E * DIFF_DH ** -0.5).at[:, fq0:fq0 + n_fq].set(LOG2E * FOX_DH ** -0.5)

    bias_table = _bias_table(rel_bias, attn_tq, attn_tq)

    for l in range(depth):
        g = norm_g[l]
        lam_init = 0.8 - 0.6 * math.exp(-0.3 * l)
        h = _ffn(h, g[0:1], g[1:2], ffn1_w_in, ffn1_w_out, l)

        qkv, fl, a_out = _in_proj(h, g[2:3], w_in_b, w_fl, col_scale, conv_w[l], conv_b[l][None], gate_w_b[l],
                                  lru_gate_b[l], lru_lambda[l][None], l, B, n_axy, n_qkv)
        cum_colb, cum_row = _fox_prep(fl, f_bias[l][None], B)
        cum_row = cum_row.reshape(B, SUBLANES, S // attn_tq, attn_tq).transpose(0, 2, 1, 3)
        d_out = _diff_attn(qkv, bias_table, diff_lambda[l], diff_subln_g[l][:, None], B, lam_init, tq=attn_tq)
        f_out = _fox_attn(qkv, cum_colb, cum_row, B, tq=attn_tq)
        h = _proj_norm_res([a_out, d_out, f_out], w_out_b, l, g[3:4], h)

        kv = _norm_matmul(mem2, ones_g, w_kv_b, l, 2 * D, BF16, normalize=False)
        h = _xattn(h, g[4:5], g[5:6], w_q_b, kv, w_o_b, l, B)

        h = _ffn(h, g[6:7], g[7:8], ffn2_w_in, ffn2_w_out, l)
    return h.reshape(B, S, D).astype(x.dtype)
```

```python
import functools
import math

import jax
import jax.numpy as jnp
import numpy as np
from jax import lax
from jax.experimental import pallas as pl
from jax.experimental.pallas import tpu as pltpu

F32 = jnp.float32
BF16 = jnp.bfloat16

CHUNK = 64
CONV_W = 4
LRU_BLOCK_W = 128
RGLRU_C = 8.0
DIFF_HEADS = 4
DIFF_DH = 64
DIFF_VD = 128
FOX_HEADS = 4
FOX_DH = 128
XATTN_HEADS = 4
REL_BUCKETS = 32
REL_MAX_DIST = 128
NORM_EPS = 1e-6
NEG_INF = -1e30
FFN_RESIDUAL_WEIGHT = 0.5
LOG2E = math.log2(math.e)

LANES = 128
SUBLANES = 8
V7X_VMEM_BYTES = 64 * 1024 * 1024
VMEM_LIMIT_BYTES = (V7X_VMEM_BYTES * 3) // 4
FFN_VMEM_LIMIT_BYTES = (V7X_VMEM_BYTES * 7) // 8


def _params(*semantics, vmem_limit_bytes=VMEM_LIMIT_BYTES):
    return pltpu.CompilerParams(dimension_semantics=semantics, vmem_limit_bytes=vmem_limit_bytes)


def _rms(x, g):
    return x * lax.rsqrt(jnp.mean(x * x, axis=-1, keepdims=True) + NORM_EPS) * g


def _dot(a, b):
    return jnp.dot(a, b, preferred_element_type=F32)


def _dot_nt(a, b):
    return lax.dot_general(a, b, (((1,), (1,)), ((), ())), preferred_element_type=F32)


FFN_EDGE_ROWS = 512


def _ffn_kernel(x_ref, gpre_ref, gpost_ref, wg_ref, wu_ref, wo_ref, o_ref, xn_ref):
    j = pl.program_id(1)
    last = pl.num_programs(1) - 1
    edge_chunks = [slice(r, r + FFN_EDGE_ROWS) for r in range(0, x_ref.shape[0], FFN_EDGE_ROWS)]

    def hidden(xn):
        gate = _dot(xn, wg_ref[...])
        up = _dot(xn, wu_ref[...])
        act = (gate * jax.nn.sigmoid(gate) * up).astype(BF16)
        return _dot(act, wo_ref[...])

    @pl.when(j == 0)
    def _():
        g = gpre_ref[...]
        for rows in edge_chunks:
            xn = _rms(x_ref[rows, :], g).astype(BF16)
            xn_ref[rows, :] = xn
            o_ref[rows, :] = hidden(xn)

    @pl.when(jnp.logical_and(j > 0, j < last))
    def _():
        o_ref[...] += hidden(xn_ref[...])

    @pl.when(j == last)
    def _():
        g = FFN_RESIDUAL_WEIGHT * gpost_ref[...]
        for rows in edge_chunks:
            y = o_ref[rows, :] + hidden(xn_ref[rows, :])
            o_ref[rows, :] = x_ref[rows, :] + _rms(y, g)


def _ffn(h, g_pre, g_post, w_in, w_out, layer, *, tm=1024, tf=256):
    T, D = h.shape
    F = w_out.shape[1]
    nf = F // tf
    return pl.pallas_call(
        _ffn_kernel,
        grid=(T // tm, nf),
        in_specs=[
            pl.BlockSpec((tm, D), lambda i, j: (i, 0)),
            pl.BlockSpec((1, D), lambda i, j: (0, 0)),
            pl.BlockSpec((1, D), lambda i, j: (0, 0)),
            pl.BlockSpec((None, D, tf), lambda i, j: (layer, 0, j)),
            pl.BlockSpec((None, D, tf), lambda i, j: (layer, 0, j + nf)),
            pl.BlockSpec((None, tf, D), lambda i, j: (layer, j, 0)),
        ],
        out_specs=pl.BlockSpec((tm, D), lambda i, j: (i, 0)),
        out_shape=jax.ShapeDtypeStruct((T, D), F32),
        scratch_shapes=[pltpu.VMEM((tm, D), BF16)],
        compiler_params=_params("parallel", "arbitrary", vmem_limit_bytes=FFN_VMEM_LIMIT_BYTES),
        name="ffn",
    )(h, g_pre, g_post, w_in, w_in, w_out)


def _norm_matmul_kernel(x_ref, g_ref, w_ref, o_ref, xn_ref, *, normalize):
    @pl.when(pl.program_id(1) == 0)
    def _():
        x = x_ref[...]
        if normalize:
            x = _rms(x, g_ref[...])
        xn_ref[...] = x.astype(BF16)

    o_ref[...] = _dot(xn_ref[...], w_ref[...]).astype(o_ref.dtype)


def _norm_matmul(x, g, w, layer, n_cols, out_dtype, *, normalize=True, tm=512, tn=1024):
    T, K = x.shape
    tn = min(tn, n_cols)
    return pl.pallas_call(
        functools.partial(_norm_matmul_kernel, normalize=normalize),
        grid=(T // tm, n_cols // tn),
        in_specs=[
            pl.BlockSpec((tm, K), lambda i, j: (i, 0)),
            pl.BlockSpec((1, K), lambda i, j: (0, 0)),
            pl.BlockSpec((None, K, tn), lambda i, j: (layer, 0, j)),
        ],
        out_specs=pl.BlockSpec((tm, tn), lambda i, j: (i, j)),
        out_shape=jax.ShapeDtypeStruct((T, n_cols), out_dtype),
        scratch_shapes=[pltpu.VMEM((tm, K), BF16)],
        compiler_params=_params("parallel", "arbitrary"),
        name="norm_matmul",
    )(x, g, w)


def _in_proj_kernel(x_ref, g_ref, w_ref, wfl_ref, cs_ref, cw_ref, cb_ref, gw_ref, gb_ref, lam_ref,
                    qkv_ref, fl_ref, aout_ref, xn_ref, axy_ref, xc_ref, tail_ref, hc_ref,
                    *, n_axy_tiles, chunk_starts, tiles_per_seq):
    i, j = pl.program_id(0), pl.program_id(1)

    @pl.when(j == 0)
    def _():
        g = g_ref[...]
        half = x_ref.shape[0] // 2
        for rows in (slice(0, half), slice(half, 2 * half)):
            xn = _rms(x_ref[rows, :], g).astype(BF16)
            xn_ref[rows, :] = xn
            fl_ref[rows, :] = _dot(xn, wfl_ref[...])
            axy_ref[0, rows, :] = _dot(xn, w_ref[...])

    @pl.when(jnp.logical_and(j == 0, i % tiles_per_seq == 0))
    def _():
        tail_ref[...] = jnp.zeros_like(tail_ref)
        hc_ref[...] = jnp.zeros_like(hc_ref)

    @pl.when(jnp.logical_and(j > 0, j < n_axy_tiles))
    def _():
        axy_ref[j] = _dot(xn_ref[...], w_ref[...])

    for c, (r0, r1) in enumerate(zip(chunk_starts[:-1], chunk_starts[1:])):
        @pl.when(j == n_axy_tiles + c)
        def _(r0=r0, r1=r1):
            n_blocks = aout_ref.shape[1] // LRU_BLOCK_W
            half, cols = xn_ref.shape[0] // 2, w_ref.shape[1] // (n_blocks // 2)

            def project(n):
                rows = slice(n % 2 * half, (n % 2 + 1) * half)
                sl = slice(n // 2 * cols, (n // 2 + 1) * cols)
                qkv_ref[rows, sl] = (_dot(xn_ref[rows, :], w_ref[:, sl]) * cs_ref[:, sl]).astype(BF16)

            def store(sl, v):
                aout_ref[r0:r1, sl] = v.astype(aout_ref.dtype)

            _rglru_rows(axy_ref[0, r0:r1, :], lambda sl: axy_ref[1, r0:r1, sl], store, project, cw_ref[...],
                        cb_ref[...], gw_ref, gb_ref, lam_ref[...], xc_ref, tail_ref, hc_ref)


def _in_proj(x, g, w, w_fl, col_scale, conv_w, conv_b, gate_w, gate_b, lam, layer, batch, n_axy, n_qkv,
             *, tm=1024, tn=1024):
    T, K = x.shape
    na, nq = n_axy // tn, n_qkv // tn
    assert na == 2, "one column tile per RG-LRU branch half"
    lru_w = n_axy // 2
    align = 2 * SUBLANES
    rows = (tm // nq) // align * align
    chunk_starts = (0,) + tuple(tm - rows * (nq - 1 - c) for c in range(nq))
    return pl.pallas_call(
        functools.partial(_in_proj_kernel, n_axy_tiles=na, chunk_starts=chunk_starts, tiles_per_seq=T // batch // tm),
        grid=(T // tm, na + nq),
        in_specs=[
            pl.BlockSpec((tm, K), lambda i, j: (i, 0)),
            pl.BlockSpec((1, K), lambda i, j: (0, 0)),
            pl.BlockSpec((None, K, tn), lambda i, j: (layer, 0, j)),
            pl.BlockSpec((None, K, LANES), lambda i, j: (layer, 0, 0)),
            pl.BlockSpec((1, tn), lambda i, j: (0, jnp.maximum(j - na, 0))),
            pl.BlockSpec((CONV_W, lru_w), lambda i, j: (0, 0)),
            pl.BlockSpec((1, lru_w), lambda i, j: (0, 0)),
            pl.BlockSpec(gate_w.shape, lambda i, j: (0, 0, 0, 0)),
            pl.BlockSpec((2, lru_w), lambda i, j: (0, 0)),
            pl.BlockSpec((1, lru_w), lambda i, j: (0, 0)),
        ],
        out_specs=[
            pl.BlockSpec((tm, tn), lambda i, j: (i, jnp.maximum(j - na, 0))),
            pl.BlockSpec((tm, LANES), lambda i, j: (i, 0)),
            pl.BlockSpec((tm, lru_w), lambda i, j: (i, 0)),
        ],
        out_shape=[
            jax.ShapeDtypeStruct((T, n_qkv), BF16),
            jax.ShapeDtypeStruct((T, LANES), F32),
            jax.ShapeDtypeStruct((T, lru_w), BF16),
        ],
        scratch_shapes=[
            pltpu.VMEM((tm, K), BF16),
            pltpu.VMEM((na, tm, tn), F32),
            pltpu.VMEM((chunk_starts[1], lru_w), F32),
            pltpu.VMEM((SUBLANES, lru_w), F32),
            pltpu.VMEM((1, lru_w), F32),
        ],
        compiler_params=_params("arbitrary", "arbitrary", vmem_limit_bytes=FFN_VMEM_LIMIT_BYTES),
        name="in_proj",
    )(x, g, w, w_fl, col_scale, conv_w, conv_b, gate_w, gate_b, lam)


def _softplus(x):
    return jnp.maximum(x, 0.0) + jnp.log1p(jnp.exp(-jnp.abs(x)))


def _rglru_rows(x, ay_block, store_block, before_block, cw, cb, gw_ref, gb_ref, lam, xc_ref, tail_ref, hc_ref):
    ts = x.shape[0]
    xc = cb + cw[CONV_W - 1:CONV_W] * x
    for d in range(1, CONV_W):
        xc = xc + cw[CONV_W - 1 - d:CONV_W - d] * pltpu.roll(x, d, axis=0)
    xc_ref[0:ts, :] = xc
    x8 = x[0:SUBLANES]
    t8 = tail_ref[...]
    row8 = lax.broadcasted_iota(jnp.int32, x8.shape, 0)
    xc8 = cb + cw[CONV_W - 1:CONV_W] * x8
    for d in range(1, CONV_W):
        sh = jnp.where(row8 < d, pltpu.roll(t8, d, axis=0), pltpu.roll(x8, d, axis=0))
        xc8 = xc8 + cw[CONV_W - 1 - d:CONV_W - d] * sh
    xc_ref[0:SUBLANES, :] = xc8
    tail_ref[...] = x[ts - SUBLANES:ts]

    sp = _softplus(-lam)
    groups = ts // SUBLANES
    row = lax.broadcasted_iota(jnp.int32, (groups, SUBLANES, LRU_BLOCK_W), 1)
    for n in range(x.shape[1] // LRU_BLOCK_W):
        before_block(n)
        sl = slice(n * LRU_BLOCK_W, (n + 1) * LRU_BLOCK_W)
        xcn = xc_ref[0:ts, sl]
        xb = xcn.astype(BF16)
        r = jax.nn.sigmoid(_dot(xb, gw_ref[0, n]) + gb_ref[0:1, sl])
        i = jax.nn.sigmoid(_dot(xb, gw_ref[1, n]) + gb_ref[1:2, sl])
        log_a = -RGLRU_C * r * sp[:, sl]
        a = jnp.exp(log_a)
        b = jnp.sqrt(-jnp.tanh(log_a) * (a * a + 1.0)) * (i * xcn)
        a = a.reshape(groups, SUBLANES, LRU_BLOCK_W)
        b = b.reshape(groups, SUBLANES, LRU_BLOCK_W)
        d = 1
        while d < SUBLANES:
            a_sh = jnp.where(row < d, 1.0, pltpu.roll(a, d, axis=1))
            b_sh = jnp.where(row < d, 0.0, pltpu.roll(b, d, axis=1))
            b = a * b_sh + b
            a = a * a_sh
            d *= 2
        carry = hc_ref[:, sl]
        hs = []
        for gi in range(groups):
            hs.append(b[gi] + a[gi] * carry)
            carry = hs[-1][SUBLANES - 1:SUBLANES]
        hc_ref[:, sl] = carry
        store_block(sl, jnp.concatenate(hs, axis=0) * jax.nn.gelu(ay_block(sl)))


def _fox_prep_kernel(fl_ref, fb_ref, colb_ref, row_ref):
    x = fl_ref[...] + fb_ref[...]
    c = jnp.minimum(x, 0.0) - jnp.log1p(jnp.exp(-jnp.abs(x)))
    S = c.shape[0]
    row = lax.broadcasted_iota(jnp.int32, c.shape, 0)
    d = 1
    while d < S:
        c = c + jnp.where(row < d, 0.0, pltpu.roll(c, d, axis=0))
        d *= 2
    c = c * LOG2E
    for h in range(FOX_HEADS):
        colb_ref[h] = jnp.broadcast_to(c[:, h:h + 1], c.shape)
    for k in range(S // LANES):
        row_ref[:, k * LANES:(k + 1) * LANES] = c[k * LANES:(k + 1) * LANES, :].T[0:SUBLANES]


def _fox_prep(fl, f_bias, batch):
    T = fl.shape[0]
    S = T // batch
    return pl.pallas_call(
        _fox_prep_kernel,
        grid=(batch,),
        in_specs=[pl.BlockSpec((S, LANES), lambda b: (b, 0)), pl.BlockSpec((1, LANES), lambda b: (0, 0))],
        out_specs=[pl.BlockSpec((None, FOX_HEADS, S, LANES), lambda b: (b, 0, 0, 0)),
                   pl.BlockSpec((None, SUBLANES, S), lambda b: (b, 0, 0))],
        out_shape=[jax.ShapeDtypeStruct((batch, FOX_HEADS, S, LANES), F32),
                   jax.ShapeDtypeStruct((batch, SUBLANES, S), F32)],
        compiler_params=_params("parallel"),
        name="fox_prep",
    )(fl, f_bias)


def _t5_bucket(rel):
    nb = REL_BUCKETS // 2
    max_exact = nb // 2
    n = jnp.abs(rel)
    large = max_exact + (jnp.log(jnp.maximum(n, max_exact).astype(jnp.float32) / max_exact)
                         / math.log(REL_MAX_DIST / max_exact) * (nb - max_exact)).astype(jnp.int32)
    large = jnp.minimum(large, nb - 1)
    return jnp.where(rel > 0, nb, 0) + jnp.where(n < max_exact, n, large)


def _bias_tile_types(tq, tk):
    return -(-(tk - 1 + REL_MAX_DIST) // tq) + 1


def _bucket_tiles(tq, tk):
    nt = _bias_tile_types(tq, tk)
    t = jnp.arange(nt, dtype=jnp.int32)[:, None, None]
    c = jnp.arange(tk, dtype=jnp.int32)[None, :, None]
    r = jnp.arange(tq, dtype=jnp.int32)[None, None, :] + t * tq
    allowed = (c // CHUNK) <= (r // CHUNK)
    return jnp.where(allowed, _t5_bucket(c - r), -1)


def _bias_table_kernel(rb_ref, bucket_ref, o_ref):
    h = pl.program_id(0)
    bucket = bucket_ref[...]
    out = jnp.full(bucket.shape, NEG_INF, F32)
    for b in range(REL_BUCKETS):
        out = jnp.where(bucket == b, rb_ref[b, h] * LOG2E, out)
    o_ref[...] = out


def _bias_table(rel_bias, tq, tk):
    buckets = _bucket_tiles(tq, tk)
    nt = buckets.shape[0]
    H = rel_bias.shape[1]
    return pl.pallas_call(
        _bias_table_kernel,
        grid=(H, nt),
        in_specs=[
            pl.BlockSpec(memory_space=pltpu.SMEM),
            pl.BlockSpec((None, tk, tq), lambda h, t: (t, 0, 0)),
        ],
        out_specs=pl.BlockSpec((None, None, tk, tq), lambda h, t: (h, t, 0, 0)),
        out_shape=jax.ShapeDtypeStruct((H, nt, tk, tq), F32),
        compiler_params=_params("parallel", "parallel"),
        name="bias_table",
    )(rel_bias, buckets)


ONES_ROWS = 16
ATTN_HEADS_PER_STEP = 2


def _paired_flash_loop(pair, n_tiles, logits, vt1, sa_ref, sb_ref, m_ref, acc_ref):
    heads = range(m_ref.shape[1])
    n_steps = n_tiles + 1

    def step(w):
        if isinstance(w, int) and w < 2:
            return w, (pair, n_tiles - 1 - pair)[w], True
        sel = (w - 2 >= pair).astype(jnp.int32)
        return sel, w - 2 - sel * pair, False

    def fill(s_ref, w):
        sel, tile, diagonal = step(w)
        for h in heads:
            s_ref[h] = logits(h, sel, tile, diagonal)

    def update(s_ref, w):
        sel, tile, _ = step(w)
        for h in heads:
            s, m = s_ref[h], m_ref[sel, h]
            m_new = jnp.maximum(m, jnp.max(s, axis=0, keepdims=True))
            p = jnp.exp2(s - m_new).astype(BF16)
            acc_ref[sel, h] = jnp.exp2(m - m_new) * acc_ref[sel, h] + _dot(vt1(h, tile), p)
            m_ref[sel, h] = m_new

    m_ref[...] = jnp.full(m_ref.shape, NEG_INF, F32)
    acc_ref[...] = jnp.zeros(acc_ref.shape, F32)
    fill(sa_ref, 0)
    fill(sb_ref, 1)
    update(sa_ref, 0)
    fill(sa_ref, 2)
    update(sb_ref, 1)

    def two_steps(i, _):
        w = 3 + 2 * i
        fill(sb_ref, w)
        update(sa_ref, w - 1)
        fill(sa_ref, w + 1)
        update(sb_ref, w)
        return 0

    lax.fori_loop(0, (n_steps - 3) // 2, two_steps, 0)
    update(sa_ref, n_steps - 1)


def _flash_scratch(n_heads, tk, n, dv):
    return [pltpu.VMEM((n_heads, tk, n), F32), pltpu.VMEM((n_heads, tk, n), F32),
            pltpu.VMEM((2, n_heads, 1, n), F32), pltpu.VMEM((2, n_heads, dv + ONES_ROWS, n), F32)]


def _store_transposed(vt_ref, v_ref):
    n_heads, n_tiles, rows, tk = vt_ref.shape
    dv = rows - ONES_ROWS
    eye = (lax.broadcasted_iota(jnp.int32, (dv, dv), 0) == lax.broadcasted_iota(jnp.int32, (dv, dv), 1)).astype(BF16)
    for h in range(n_heads):
        for j in range(n_tiles):
            vt_ref[h, j, :dv, :] = _dot_nt(eye, v_ref[j * tk:(j + 1) * tk, h * dv:(h + 1) * dv]).astype(BF16)
            vt_ref[h, j, dv:, :] = jnp.ones((ONES_ROWS, tk), BF16)


def _diff_attn_kernel(q_ref, k_ref, v_ref, bias_ref, dl_ref, sg_ref, o_ref, vt_ref, qq_ref, sa_ref, sb_ref, m_ref,
                      acc_ref, *, tq, nt, lam_init):
    pair = pl.program_id(2)
    n_tiles = k_ref.shape[0] // tq

    @pl.when(pair == 0)
    def _():
        _store_transposed(vt_ref, v_ref)

    nh = ATTN_HEADS_PER_STEP
    q0s = (pl.multiple_of(pair * tq, tq), pl.multiple_of((n_tiles - 1 - pair) * tq, tq))
    lane = lax.broadcasted_iota(jnp.int32, (tq, LANES), 1)
    for sel in range(2):
        for h in range(nh):
            q = q_ref[pl.ds(q0s[sel], tq), h * LANES:(h + 1) * LANES]
            qq_ref[sel, h, :tq, :] = jnp.where(lane < DIFF_DH, q, 0)
            qq_ref[sel, h, tq:, :] = jnp.where(lane >= DIFF_DH, q, 0)

    def logits(h, sel, tile, diagonal):
        del diagonal
        k0 = pl.multiple_of(tile * tq, tq)
        q0 = q0s[0] + sel * (q0s[1] - q0s[0])
        s = _dot_nt(k_ref[pl.ds(k0, tq), h * LANES:(h + 1) * LANES], qq_ref[sel, h])
        bias = bias_ref[h, jnp.minimum((q0 - k0) // tq, nt - 1)]
        return s + jnp.concatenate([bias, bias], axis=1)

    _paired_flash_loop(pair, n_tiles, logits, lambda h, tile: vt_ref[h, tile], sa_ref, sb_ref, m_ref, acc_ref)
    dl = dl_ref[...]
    lam = (jnp.exp(jnp.sum(dl[0:1] * dl[1:2], axis=-1, keepdims=True))
           - jnp.exp(jnp.sum(dl[2:3] * dl[3:4], axis=-1, keepdims=True)) + lam_init)
    for sel in range(2):
        for h in range(nh):
            o = acc_ref[sel, h, :DIFF_VD, :] / acc_ref[sel, h, DIFF_VD:DIFF_VD + 1, :]
            d = o[:, :tq] - lam * o[:, tq:]
            y = d * lax.rsqrt(jnp.mean(d * d, axis=0, keepdims=True) + NORM_EPS) * sg_ref[...]
            o_ref[pl.ds(q0s[sel], tq), h * DIFF_VD:(h + 1) * DIFF_VD] = (y * (1.0 - lam_init)).T.astype(o_ref.dtype)


def _diff_attn(qkv, bias_table, diff_lambda, subln_g, batch, lam_init, *, tq):
    T = qkv.shape[0]
    S = T // batch
    nq = S // tq
    nt = bias_table.shape[1]
    nh = ATTN_HEADS_PER_STEP
    G = DIFF_HEADS // nh
    W = nh * LANES
    return pl.pallas_call(
        functools.partial(_diff_attn_kernel, tq=tq, nt=nt, lam_init=lam_init),
        grid=(batch, G, nq // 2),
        in_specs=[
            pl.BlockSpec((S, W), lambda b, g, i: (b, g)),
            pl.BlockSpec((S, W), lambda b, g, i: (b, G + g)),
            pl.BlockSpec((S, W), lambda b, g, i: (b, 2 * G + g)),
            pl.BlockSpec((nh, nt, tq, tq), lambda b, g, i: (g, 0, 0, 0)),
            pl.BlockSpec(diff_lambda.shape, lambda b, g, i: (0, 0)),
            pl.BlockSpec((DIFF_VD, 1), lambda b, g, i: (0, 0)),
        ],
        out_specs=pl.BlockSpec((S, nh * DIFF_VD), lambda b, g, i: (b, g)),
        out_shape=jax.ShapeDtypeStruct((T, DIFF_HEADS * DIFF_VD), BF16),
        scratch_shapes=([pltpu.VMEM((nh, nq, DIFF_VD + ONES_ROWS, tq), BF16), pltpu.VMEM((2, nh, 2 * tq, LANES), BF16)]
                        + _flash_scratch(nh, tq, 2 * tq, DIFF_VD)),
        compiler_params=_params("parallel", "parallel", "arbitrary"),
        name="diff_attn",
    )(qkv, qkv, qkv, bias_table, diff_lambda, subln_g)


def _fox_attn_kernel(q_ref, k_ref, v_ref, ck_ref, cq_ref, o_ref, vt_ref, sa_ref, sb_ref, m_ref, acc_ref, *, tq):
    pair = pl.program_id(2)
    n_tiles = k_ref.shape[0] // tq

    @pl.when(pair == 0)
    def _():
        _store_transposed(vt_ref, v_ref)

    nh = ATTN_HEADS_PER_STEP
    h0 = pl.program_id(1) * nh
    q_tiles = (pair, n_tiles - 1 - pair)
    q0s = tuple(pl.multiple_of(t * tq, tq) for t in q_tiles)
    cqs = [[cq_ref[q_tiles[sel], pl.ds(h0 + h, 1), :] for h in range(nh)] for sel in range(2)]

    def logits(h, sel, tile, diagonal):
        k0 = pl.multiple_of(tile * tq, tq)
        if diagonal:
            q0, cq = q0s[sel], cqs[sel][h]
        else:
            q0 = q0s[0] + sel * (q0s[1] - q0s[0])
            cq = jnp.where(sel == 1, cqs[1][h], cqs[0][h])
        q = q_ref[pl.ds(q0, tq), h * LANES:(h + 1) * LANES]
        ck = ck_ref[h, pl.ds(k0, tq), :]
        s = _dot_nt(k_ref[pl.ds(k0, tq), h * LANES:(h + 1) * LANES], q)
        s = s + (cq - jnp.concatenate([ck] * (tq // LANES), axis=1))
        if diagonal:
            s = jnp.where(lax.broadcasted_iota(jnp.int32, s.shape, 0) <= lax.broadcasted_iota(jnp.int32, s.shape, 1),
                          s, NEG_INF)
        return s

    _paired_flash_loop(pair, n_tiles, logits, lambda h, tile: vt_ref[h, tile], sa_ref, sb_ref, m_ref, acc_ref)
    for sel in range(2):
        for h in range(nh):
            o = acc_ref[sel, h, :FOX_DH, :] / acc_ref[sel, h, FOX_DH:FOX_DH + 1, :]
            o_ref[pl.ds(q0s[sel], tq), h * FOX_DH:(h + 1) * FOX_DH] = o.T.astype(o_ref.dtype)


def _fox_attn(qkv, cum_colb, cum_row, batch, *, tq):
    T = qkv.shape[0]
    S = T // batch
    nq = S // tq
    nh = ATTN_HEADS_PER_STEP
    G = FOX_HEADS // nh
    W = nh * LANES
    base = 3 * DIFF_HEADS // nh
    return pl.pallas_call(
        functools.partial(_fox_attn_kernel, tq=tq),
        grid=(batch, G, nq // 2),
        in_specs=[
            pl.BlockSpec((S, W), lambda b, g, i: (b, base + g)),
            pl.BlockSpec((S, W), lambda b, g, i: (b, base + G + g)),
            pl.BlockSpec((S, W), lambda b, g, i: (b, base + 2 * G + g)),
            pl.BlockSpec((None, nh, S, LANES), lambda b, g, i: (b, g, 0, 0)),
            pl.BlockSpec((None, nq, SUBLANES, tq), lambda b, g, i: (b, 0, 0, 0)),
        ],
        out_specs=pl.BlockSpec((S, nh * FOX_DH), lambda b, g, i: (b, g)),
        out_shape=jax.ShapeDtypeStruct((T, FOX_HEADS * FOX_DH), BF16),
        scratch_shapes=([pltpu.VMEM((nh, nq, FOX_DH + ONES_ROWS, tq), BF16)] + _flash_scratch(nh, tq, tq, FOX_DH)),
        compiler_params=_params("parallel", "parallel", "arbitrary"),
        name="fox_attn",
    )(qkv, qkv, qkv, cum_colb, cum_row)


def _xattn_kernel(h_ref, gq_ref, go_ref, wq_ref, k_ref, v_ref, wo_ref, o_ref, *, n_heads):
    x = h_ref[...]
    dh = x.shape[1] // n_heads
    xn = _rms(x, gq_ref[...]).astype(BF16)
    q = (_dot(xn, wq_ref[...]) * (LOG2E * dh ** -0.5)).astype(BF16)
    outs = []
    for hd in range(n_heads):
        sl = slice(hd * dh, (hd + 1) * dh)
        s = _dot_nt(q[:, sl], k_ref[:, sl])
        p = jnp.exp2(s - jnp.max(s, axis=-1, keepdims=True))
        l = jnp.sum(p, axis=-1, keepdims=True)
        outs.append((_dot(p.astype(BF16), v_ref[:, sl]) / l).astype(BF16))
    y = _dot(jnp.concatenate(outs, axis=1), wo_ref[...])
    o_ref[...] = x + _rms(y, go_ref[...])


def _kv_proj_kernel(x_ref, w_ref, o_ref):
    o_ref[...] = _dot(x_ref[...].astype(BF16), w_ref[...]).astype(o_ref.dtype)


def _kv_proj(mem, w, *, tn=1024):
    R, K = mem.shape
    L, _, N = w.shape
    return pl.pallas_call(
        _kv_proj_kernel,
        grid=(L, N // tn),
        in_specs=[pl.BlockSpec((R, K), lambda l, j: (0, 0)), pl.BlockSpec((None, K, tn), lambda l, j: (l, 0, j))],
        out_specs=pl.BlockSpec((None, R, tn), lambda l, j: (l, 0, j)),
        out_shape=jax.ShapeDtypeStruct((L, R, N), BF16),
        compiler_params=_params("parallel", "parallel"),
        name="kv_proj",
    )(mem, w)


def _xattn(h, g_q, g_o, w_q, kv, w_o, layer, batch, *, tm=512):
    T, D = h.shape
    M = kv.shape[1] // batch
    ns = T // batch // tm
    resident = pl.Buffered(1)
    return pl.pallas_call(
        functools.partial(_xattn_kernel, n_heads=XATTN_HEADS),
        grid=(batch, ns),
        in_specs=[
            pl.BlockSpec((tm, D), lambda b, i: (b * ns + i, 0)),
            pl.BlockSpec((1, D), lambda b, i: (0, 0)),
            pl.BlockSpec((1, D), lambda b, i: (0, 0)),
            pl.BlockSpec((None, D, D), lambda b, i: (layer, 0, 0), pipeline_mode=resident),
            pl.BlockSpec((None, M, D), lambda b, i: (layer, b, 0)),
            pl.BlockSpec((None, M, D), lambda b, i: (layer, b, 1)),
            pl.BlockSpec((None, D, D), lambda b, i: (layer, 0, 0), pipeline_mode=resident),
        ],
        out_specs=pl.BlockSpec((tm, D), lambda b, i: (b * ns + i, 0)),
        out_shape=jax.ShapeDtypeStruct((T, D), F32),
        compiler_params=_params("parallel", "arbitrary"),
        name="xattn",
    )(h, g_q, g_o, w_q, kv, kv, w_o)


def _proj_norm_res_kernel(*refs, n_in):
    x_refs, w_refs = refs[:n_in], refs[n_in:2 * n_in]
    g_ref, h_ref, o_ref = refs[2 * n_in:]
    y = _dot(x_refs[0][...], w_refs[0][...])
    for x_ref, w_ref in zip(x_refs[1:], w_refs[1:]):
        y = y + _dot(x_ref[...], w_ref[...])
    o_ref[...] = h_ref[...] + _rms(y, g_ref[...])


def _proj_norm_res(xs, w, layer, g, h, *, tm=512):
    T, N = h.shape
    n_in = len(xs)
    x_specs, w_specs, row0 = [], [], 0
    for x in xs:
        k = x.shape[1]
        x_specs.append(pl.BlockSpec((tm, k), lambda i: (i, 0)))
        w_specs.append(pl.BlockSpec((None, k, N), functools.partial(lambda i, r: (layer, r, 0), r=row0 // k)))
        row0 += k
    return pl.pallas_call(
        functools.partial(_proj_norm_res_kernel, n_in=n_in),
        grid=(T // tm,),
        in_specs=x_specs + w_specs + [pl.BlockSpec((1, N), lambda i: (0, 0)), pl.BlockSpec((tm, N), lambda i: (i, 0))],
        out_specs=pl.BlockSpec((tm, N), lambda i: (i, 0)),
        out_shape=jax.ShapeDtypeStruct((T, N), F32),
        compiler_params=_params("parallel"),
        name="proj_norm_res",
    )(*xs, *([w] * n_in), g, h)


def kernel(x, mem, norm_g, ffn1_w_in, ffn1_w_out, w_in, conv_w, conv_b, lru_gate_w, lru_gate_b, lru_lambda,
           diff_lambda, diff_subln_g, fox_f_bias, rel_bias, w_out, xattn_w_q, xattn_w_kv, xattn_w_o,
           ffn2_w_in, ffn2_w_out):
    B, S, D = x.shape
    depth = norm_g.shape[0]
    T = B * S
    lru_w = conv_w.shape[-1]
    n_axy = 2 * lru_w
    n_qkv = 3 * DIFF_HEADS * DIFF_VD + 3 * FOX_HEADS * FOX_DH
    attn_tq = 512

    h = x.astype(F32).reshape(T, D)
    mem2 = mem.astype(F32).reshape(B * mem.shape[1], D)
    ones_g = jnp.ones((1, D), F32)

    bf = lambda a: a.astype(BF16)
    ffn1_w_in, ffn1_w_out, ffn2_w_in, ffn2_w_out = bf(ffn1_w_in), bf(ffn1_w_out), bf(ffn2_w_in), bf(ffn2_w_out)
    w_in_b, w_out_b, gate_w_b = bf(w_in), bf(w_out), bf(lru_gate_w)
    w_q_b, w_kv_b, w_o_b = bf(xattn_w_q), bf(xattn_w_kv), bf(xattn_w_o)
    w_fl = jnp.pad(w_in_b[:, :, n_axy + n_qkv:], ((0, 0), (0, 0), (0, LANES - FOX_HEADS)))
    f_bias = jnp.pad(fox_f_bias, ((0, 0), (0, LANES - FOX_HEADS)))
    n_dq, n_fq = DIFF_HEADS * 2 * DIFF_DH, FOX_HEADS * FOX_DH
    fq0 = 3 * DIFF_HEADS * DIFF_VD
    col_scale = jnp.ones((1, n_qkv), F32)
    col_scale = col_scale.at[:, :n_dq].set(LOG2E * DIFF_DH ** -0.5).at[:, fq0:fq0 + n_fq].set(LOG2E * FOX_DH ** -0.5)

    bias_table = _bias_table(rel_bias, attn_tq, attn_tq)
    kv = _kv_proj(mem2, w_kv_b)

    for l in range(depth):
        g = norm_g[l]
        lam_init = 0.8 - 0.6 * math.exp(-0.3 * l)
        h = _ffn(h, g[0:1], g[1:2], ffn1_w_in, ffn1_w_out, l)

        qkv, fl, a_out = _in_proj(h, g[2:3], w_in_b, w_fl, col_scale, conv_w[l], conv_b[l][None], gate_w_b[l],
                                  lru_gate_b[l], lru_lambda[l][None], l, B, n_axy, n_qkv)
        cum_colb, cum_row = _fox_prep(fl, f_bias[l][None], B)
        cum_row = cum_row.reshape(B, SUBLANES, S // attn_tq, attn_tq).transpose(0, 2, 1, 3)
        d_out = _diff_attn(qkv, bias_table, diff_lambda[l], diff_subln_g[l][:, None], B, lam_init, tq=attn_tq)
        f_out = _fox_attn(qkv, cum_colb, cum_row, B, tq=attn_tq)
        h = _proj_norm_res([a_out, d_out, f_out], w_out_b, l, g[3:4], h)

        h = _xattn(h, g[4:5], g[5:6], w_q_b, kv, w_o_b, l, B)

        h = _ffn(h, g[6:7], g[7:8], ffn2_w_in, ffn2_w_out, l)
    return h.reshape(B, S, D).astype(x.dtype)
```
